```python
import math
import jax, jax.numpy as jnp
from jax import lax
import numpy as np

D_MODEL = 1024
BATCH = 2
SEQ = 8192
DEPTH = 1
DEC_BATCH = 128
DEC_SEQ = 4
PAST_LEN = 2048
PAGE_SIZE = 128

MIX_WIDTH = D_MODEL
ATTN_WIDTH = MIX_WIDTH // 2
REC_WIDTH = MIX_WIDTH - ATTN_WIDTH
A_HEADS = 4
A_HD = ATTN_WIDTH // (2 * A_HEADS)
A_VD = 2 * A_HD
R_HEADS = 4
R_DK = REC_WIDTH // R_HEADS
R_DV = REC_WIDTH // R_HEADS
REC_CHUNK = 64
N_BUCKETS = 32
MAX_DISTANCE = 128
D_FF = ((8 * D_MODEL // 3 + 127) // 128) * 128
CONV_W = 3
Q_BLOCK = 128
EPS = 1e-6
IN_COLS = 3 * ATTN_WIDTH + 4 * REC_WIDTH
SPLITS = [ATTN_WIDTH, 2 * ATTN_WIDTH, 3 * ATTN_WIDTH,
          3 * ATTN_WIDTH + REC_WIDTH, 3 * ATTN_WIDTH + 2 * REC_WIDTH, 3 * ATTN_WIDTH + 3 * REC_WIDTH]

kernel_name = 'hymba_diffattn_hgrn2_convglu_step'


def rms_norm(x, g):
    xf = x.astype(jnp.float32)
    y = xf * lax.rsqrt(jnp.mean(xf * xf, axis=-1, keepdims=True) + EPS)
    return (y * g.astype(jnp.float32)).astype(x.dtype)


def rel_bucket(q_pos, k_pos):
    n = jnp.maximum(q_pos[:, None] - k_pos[None, :], 0)
    max_exact = N_BUCKETS // 2
    nf = jnp.maximum(n, 1).astype(jnp.float32)
    large = max_exact + (jnp.log(nf / max_exact) / math.log(MAX_DISTANCE / max_exact)
                         * (N_BUCKETS - max_exact)).astype(jnp.int32)
    large = jnp.minimum(large, N_BUCKETS - 1)
    return jnp.where(n < max_exact, n, large)


def diff_attn_block(q, k, v, q_pos, k_pos, rel_bias, lam):
    s = jnp.einsum('bqhmd,bkhmd->bhmqk', q, k).astype(jnp.float32) * (A_HD ** -0.5)
    bias = jnp.transpose(rel_bias.astype(jnp.float32)[rel_bucket(q_pos, k_pos)], (2, 0, 1))
    mask = k_pos[None, :] <= q_pos[:, None]
    s = jnp.where(mask, s + bias[None, :, None], -jnp.inf)
    p = jax.nn.softmax(s, axis=-1)
    a = p[:, :, 0] - lam * p[:, :, 1]
    return jnp.einsum('bhqk,bkhv->bqhv', a.astype(v.dtype), v)


def attn_prompt(q, k, v, rel_bias, lam):
    B, T = q.shape[0], q.shape[1]
    nqb = T // Q_BLOCK
    pos = jnp.arange(T, dtype=jnp.int32)
    qb = jnp.moveaxis(q.reshape(B, nqb, Q_BLOCK, A_HEADS, 2, A_HD), 1, 0)
    pb = pos.reshape(nqb, Q_BLOCK)
    ob = lax.map(lambda a: diff_attn_block(a[0], k, v, a[1], pos, rel_bias, lam), (qb, pb))
    return jnp.moveaxis(ob, 0, 1).reshape(B, T, A_HEADS, A_VD)


def hgrn_chunk(S, q, logf, kk, v):
    L = q.shape[2]
    b = jnp.cumsum(logf, axis=2)
    causal = jnp.tril(jnp.ones((L, L), dtype=bool))
    diff = b[:, :, :, None, :] - b[:, :, None, :, :]
    dec = jnp.exp(jnp.where(causal[None, None, :, :, None], diff, -jnp.inf))
    A = jnp.einsum('bhtk,bhsk,bhtsk->bhts', q, kk, dec)
    o = jnp.einsum('bhts,bhsv->bhtv', A, v) + jnp.einsum('bhtk,bhkv->bhtv', q * jnp.exp(b), S)
    b_last = b[:, :, -1:, :]
    S_new = jnp.exp(b_last[:, :, 0, :])[..., None] * S + jnp.einsum('bhsk,bhsv->bhkv', kk * jnp.exp(b_last - b), v)
    return S_new, o


def hgrn_mixer(rq, rf, ri, rg, S0, lb, g_norm, chunk):
    B, T = rq.shape[0], rq.shape[1]
    f = lb + (1.0 - lb) * jax.nn.sigmoid(rf.astype(jnp.float32))
    logf = jnp.log(f)
    kk = 1.0 - f
    heads = lambda t: t.astype(jnp.float32).reshape(B, T, R_HEADS, -1).transpose(0, 2, 1, 3)
    nC = T // chunk
    to_chunks = lambda t: jnp.moveaxis(t.reshape(B, R_HEADS, nC, chunk, t.shape[-1]), 2, 0)
    xs = (to_chunks(heads(rq)), to_chunks(heads(logf)), to_chunks(heads(kk)), to_chunks(heads(ri)))
    S_fin, o = lax.scan(lambda S, a: hgrn_chunk(S, *a), S0.astype(jnp.float32), xs)
    o = jnp.moveaxis(o, 0, 2).reshape(B, R_HEADS, T, R_DV).transpose(0, 2, 1, 3)
    o = rms_norm(o, g_norm) * jax.nn.silu(rg.astype(jnp.float32).reshape(B, T, R_HEADS, R_DV))
    return o.reshape(B, T, REC_WIDTH).astype(rq.dtype), S_fin.astype(S0.dtype)


def project(h, w, qk_g):
    B, T = h.shape[0], h.shape[1]
    z = h @ w
    aq, ak, av, rq, rf, ri, rg = jnp.split(z, SPLITS, axis=-1)
    aq = rms_norm(aq.reshape(B, T, A_HEADS, 2, A_HD), qk_g[0])
    ak = rms_norm(ak.reshape(B, T, A_HEADS, 2, A_HD), qk_g[1])
    av = av.reshape(B, T, A_HEADS, A_VD)
    return aq, ak, av, rq, rf, ri, rg


def mixer_out(o_attn, o_rec, subln, lam_init, w_out):
    B, T = o_attn.shape[0], o_attn.shape[1]
    oa = (rms_norm(o_attn, subln) * (1.0 - lam_init)).reshape(B, T, ATTN_WIDTH)
    return jnp.concatenate([oa, o_rec.astype(oa.dtype)], axis=-1) @ w_out


def conv_ffn(h, buf, w_gate, w_up, conv_w, conv_b, w_down):
    T = h.shape[1]
    a = h @ w_gate
    ext = jnp.concatenate([buf.astype(a.dtype), a], axis=1)
    c = conv_b + sum(conv_w[j] * ext[:, j:j + T] for j in range(CONV_W))
    y = (jax.nn.silu(c) * (h @ w_up)) @ w_down
    return y, ext[:, -(CONV_W - 1):]


def setup_inputs(seed: int = 0) -> dict:
    key = jax.random.key(seed)
    ks = jax.random.split(key, 24)
    n_pages = PAST_LEN // PAGE_SIZE
    used = DEC_BATCH * n_pages
    n_phys = used + max(1, used // 4)
    nrm = lambda k, s, sc: jax.random.normal(k, s, jnp.float32) * sc
    page_table = jax.random.permutation(ks[0], n_phys)[:used].reshape(DEC_BATCH, n_pages).astype(jnp.int32)
    return {
        'x_prompt': nrm(ks[1], (BATCH, SEQ, D_MODEL), 1.0),
        'x_sample': nrm(ks[2], (DEC_BATCH, DEC_SEQ, D_MODEL), 1.0),
        'cache_k': nrm(ks[3], (DEPTH, n_phys, PAGE_SIZE, A_HEADS, 2, A_HD), 1.0),
        'cache_v': nrm(ks[4], (DEPTH, n_phys, PAGE_SIZE, A_HEADS, A_VD), 1.0),
        'page_table': page_table,
        'state_hgrn': nrm(ks[5], (DEPTH, DEC_BATCH, R_HEADS, R_DK, R_DV), 0.3),
        'state_conv': nrm(ks[6], (DEPTH, DEC_BATCH, CONV_W - 1, D_FF), 1.0),
        'rel_bias': nrm(ks[7], (N_BUCKETS, A_HEADS), 0.5),
        'norm1_g': 1.0 + nrm(ks[8], (DEPTH, D_MODEL), 0.02),
        'w_in': nrm(ks[9], (DEPTH, D_MODEL, IN_COLS), D_MODEL ** -0.5),
        'qk_norm_g': 1.0 + nrm(ks[10], (DEPTH, 2, 2, A_HD), 0.02),
        'lambda_qk': nrm(ks[11], (DEPTH, 4, A_HD), 0.1),
        'subln_g': 1.0 + nrm(ks[12], (DEPTH, A_VD), 0.02),
        'hgrn_lb': nrm(ks[13], (DEPTH + 1, REC_WIDTH), 0.1),
        'hgrn_onorm_g': 1.0 + nrm(ks[14], (DEPTH, R_DV), 0.02),
        'w_out': nrm(ks[15], (DEPTH, MIX_WIDTH, D_MODEL), MIX_WIDTH ** -0.5),
        'norm2_g': 1.0 + nrm(ks[16], (DEPTH, D_MODEL), 0.02),
        'w_gate': nrm(ks[17], (DEPTH, D_MODEL, D_FF), D_MODEL ** -0.5),
        'w_up': nrm(ks[18], (DEPTH, D_MODEL, D_FF), D_MODEL ** -0.5),
        'conv_w': nrm(ks[19], (DEPTH, CONV_W, D_FF), CONV_W ** -0.5),
        'conv_b': nrm(ks[20], (DEPTH, D_FF), 0.01),
        'w_down': nrm(ks[21], (DEPTH, D_FF, D_MODEL), D_FF ** -0.5),
    }


def reference(x_prompt, x_sample, cache_k, cache_v, page_table, state_hgrn, state_conv, rel_bias,
              norm1_g, w_in, qk_norm_g, lambda_qk, subln_g, hgrn_lb, hgrn_onorm_g, w_out,
              norm2_g, w_gate, w_up, conv_w, conv_b, w_down):
    xp, xs = x_prompt, x_sample
    Bp, Tp = xp.shape[0], xp.shape[1]
    Bs, Ts = xs.shape[0], xs.shape[1]
    past_len = page_table.shape[1] * cache_k.shape[2]
    lb_all = jnp.cumsum(jax.nn.softmax(hgrn_lb.astype(jnp.float32), axis=0), axis=0)
    kp_l, vp_l, hp_l, cp_l, ks_l, vs_l, hs_l, cs_l = [], [], [], [], [], [], [], []
    for l in range(DEPTH):
        lam_init = 0.8 - 0.6 * math.exp(-0.3 * l)
        lp = lambda_qk[l].astype(jnp.float32)
        lam = jnp.exp(jnp.sum(lp[0] * lp[1])) - jnp.exp(jnp.sum(lp[2] * lp[3])) + lam_init
        lb = lb_all[l]
        h = rms_norm(xp, norm1_g[l])
        aq, ak, av, rq, rf, ri, rg = project(h, w_in[l], qk_norm_g[l])
        o_att = attn_prompt(aq, ak, av, rel_bias, lam)
        s0 = jnp.zeros((Bp, R_HEADS, R_DK, R_DV), xp.dtype)
        o_rec, S_p = hgrn_mixer(rq, rf, ri, rg, s0, lb, hgrn_onorm_g[l], REC_CHUNK)
        xp = xp + mixer_out(o_att, o_rec, subln_g[l], lam_init, w_out[l])
        buf0 = jnp.zeros((Bp, CONV_W - 1, D_FF), xp.dtype)
        y, cbuf_p = conv_ffn(rms_norm(xp, norm2_g[l]), buf0, w_gate[l], w_up[l], conv_w[l], conv_b[l], w_down[l])
        xp = xp + y
        kp_l.append(ak); vp_l.append(av); hp_l.append(S_p); cp_l.append(cbuf_p)
        h = rms_norm(xs, norm1_g[l])
        aq, ak, av, rq, rf, ri, rg = project(h, w_in[l], qk_norm_g[l])
        past_k = cache_k[l][page_table].reshape(Bs, past_len, A_HEADS, 2, A_HD)
        past_v = cache_v[l][page_table].reshape(Bs, past_len, A_HEADS, A_VD)
        k_all = jnp.concatenate([past_k.astype(ak.dtype), ak], axis=1)
        v_all = jnp.concatenate([past_v.astype(av.dtype), av], axis=1)
        q_pos = past_len + jnp.arange(Ts, dtype=jnp.int32)
        k_pos = jnp.arange(past_len + Ts, dtype=jnp.int32)
        o_att = diff_attn_block(aq, k_all, v_all, q_pos, k_pos, rel_bias, lam)
        o_rec, S_s = hgrn_mixer(rq, rf, ri, rg, state_hgrn[l], lb, hgrn_onorm_g[l], Ts)
        xs = xs + mixer_out(o_att, o_rec, subln_g[l], lam_init, w_out[l])
        y, cbuf_s = conv_ffn(rms_norm(xs, norm2_g[l]), state_conv[l], w_gate[l], w_up[l], conv_w[l], conv_b[l], w_down[l])
        xs = xs + y
        ks_l.append(ak); vs_l.append(av); hs_l.append(S_s); cs_l.append(cbuf_s)
    return (xp, xs, jnp.stack(kp_l), jnp.stack(vp_l), jnp.stack(hp_l), jnp.stack(cp_l),
            jnp.stack(ks_l), jnp.stack(vs_l), jnp.stack(hs_l), jnp.stack(cs_l))
```

```python
import functools
import math

import numpy as np
import jax
import jax.numpy as jnp
from jax import lax
from jax.experimental import pallas as pl
from jax.experimental.pallas import tpu as pltpu

F32 = jnp.float32
BF16 = jnp.bfloat16

EPS = 1e-6
N_BUCKETS = 32
MAX_DISTANCE = 128
CONV_W = 3
NEG = -1e30
LANES = 128
SUBLANES = 8

TQ = 256
HG_CHUNK = 128
HG_BLOCK = 512
SAMPLE_PAD = 8
FF_TILE = 256
VMEM_LIMIT = 56 * 1024 * 1024


def _const_spec(shape, index_map):
    return pl.BlockSpec(shape, index_map, pipeline_mode=pl.Buffered(1))


def _nt(a, b):
    return lax.dot_general(a, b, (((1,), (1,)), ((), ())), preferred_element_type=F32)


def _tn(a, b):
    return lax.dot_general(a, b, (((0,), (0,)), ((), ())), preferred_element_type=F32)


def _nn(a, b):
    return jnp.dot(a, b, preferred_element_type=F32)


def _rel_bucket_np(n):
    n = np.maximum(n, 0)
    max_exact = N_BUCKETS // 2
    nf = np.maximum(n, 1).astype(np.float32)
    large = max_exact + (np.log(nf / np.float32(max_exact)) / np.float32(math.log(MAX_DISTANCE / max_exact))
                         * np.float32(N_BUCKETS - max_exact)).astype(np.int32)
    large = np.minimum(large, N_BUCKETS - 1)
    return np.where(n < max_exact, n, large).astype(np.int32)


def _tables_kernel(relb_ref, lq_ref, hlb_ref, bkp_ref, bks_ref,
                   biasp_ref, biass_ref, lam_ref, lb_ref, *, n_heads, lam_init):
    for h in range(n_heads):
        far = relb_ref[N_BUCKETS - 1, h]
        for kind in range(2):
            bk = bkp_ref[kind]
            acc = jnp.where(bk < 0, NEG, 0.0).astype(F32)
            for b in range(N_BUCKETS - 1):
                acc = jnp.where(bk == b, relb_ref[b, h] - far, acc)
            biasp_ref[h, kind] = acc
            rows = bks_ref.shape[1] // n_heads
            bk = bks_ref[kind, h * rows:(h + 1) * rows, :]
            acc = jnp.where(bk < 0, NEG, 0.0).astype(F32)
            for b in range(N_BUCKETS - 1):
                acc = jnp.where(bk == b, relb_ref[b, h] - far, acc)
            biass_ref[kind, h * rows:(h + 1) * rows, :] = acc
    lq = lq_ref[...].astype(F32)
    s1 = jnp.sum(lq[0:1] * lq[1:2], axis=1, keepdims=True)
    s2 = jnp.sum(lq[2:3] * lq[3:4], axis=1, keepdims=True)
    lam = jnp.exp(s1) - jnp.exp(s2) + lam_init
    lam_ref[...] = jnp.broadcast_to(lam, lam_ref.shape)
    hl = hlb_ref[...].astype(F32)
    mx = jnp.max(hl, axis=0, keepdims=True)
    e = jnp.exp(hl - mx)
    lb_ref[...] = e[0:1] / jnp.sum(e, axis=0, keepdims=True)


def _tables(rel_bias, lambda_qk_l, hgrn_lb, n_heads, n_tok, page_size, lam_init):
    r = np.arange(TQ)[:, None]
    c = np.arange(TQ)[None, :]
    diag = np.where(c <= r, _rel_bucket_np(r - c), -1)
    prev = _rel_bucket_np(TQ + r - c)
    bkp = jnp.asarray(np.stack([diag, prev]).astype(np.int32))
    t = np.tile(np.arange(SAMPLE_PAD), 2 * n_heads)[:, None]
    cc = np.arange(page_size)[None, :]
    last_page = _rel_bucket_np(page_size + t - cc)
    new_page = np.where(cc <= np.minimum(t, n_tok - 1), _rel_bucket_np(t - cc), -1)
    bks = jnp.asarray(np.stack([last_page, new_page]).astype(np.int32))
    rows_s = 2 * n_heads * SAMPLE_PAD
    kern = functools.partial(_tables_kernel, n_heads=n_heads, lam_init=lam_init)
    return pl.pallas_call(
        kern,
        out_shape=(jax.ShapeDtypeStruct((n_heads, 2, TQ, TQ), F32),
                   jax.ShapeDtypeStruct((2, rows_s, page_size), F32),
                   jax.ShapeDtypeStruct((SUBLANES, LANES), F32),
                   jax.ShapeDtypeStruct((1, hgrn_lb.shape[1]), F32)),
        in_specs=[pl.BlockSpec(memory_space=pltpu.SMEM),
                  pl.BlockSpec(memory_space=pltpu.VMEM),
                  pl.BlockSpec(memory_space=pltpu.VMEM),
                  pl.BlockSpec(memory_space=pltpu.VMEM),
                  pl.BlockSpec(memory_space=pltpu.VMEM)],
        name="tables",
    )(rel_bias, lambda_qk_l, hgrn_lb, bkp, bks)


def _proj_kernel(x_ref, g1_ref, w_ref, qg_ref, kg_ref, bd_ref, *out_refs, aw, rw, q_scale, emit_bf16):
    if emit_bf16:
        q_ref, kf_ref, vf_ref, kb_ref, vb_ref, r_ref = out_refs
    else:
        q_ref, kf_ref, vf_ref, r_ref = out_refs
    x = x_ref[...].astype(F32)
    ms = jnp.mean(x * x, axis=-1, keepdims=True)
    h = (x * lax.rsqrt(ms + EPS) * g1_ref[...]).astype(BF16)

    def seg_norm(z, g):
        msq = _nn((z * z).astype(BF16), bd_ref[...])
        return z * lax.rsqrt(msq + EPS) * g

    zq = _nn(h, w_ref[:, 0:aw])
    q_ref[...] = (seg_norm(zq, qg_ref[...]) * q_scale).astype(q_ref.dtype)
    zk = _nn(h, w_ref[:, aw:2 * aw])
    kn = seg_norm(zk, kg_ref[...])
    kf_ref[...] = kn
    zv = _nn(h, w_ref[:, 2 * aw:3 * aw])
    vf_ref[...] = zv
    if emit_bf16:
        kb_ref[...] = kn.astype(BF16)
        vb_ref[...] = zv.astype(BF16)
    for g in range(4):
        z = _nn(h, w_ref[:, 3 * aw + g * rw:3 * aw + (g + 1) * rw])
        r_ref[:, g * rw:(g + 1) * rw] = z.astype(r_ref.dtype)


def _proj(x2, g1, w_bf, qg, kg, bd, aw, rw, q_scale, emit_bf16, q_dtype, r_dtype, tm):
    n, d = x2.shape
    kern = functools.partial(_proj_kernel, aw=aw, rw=rw, q_scale=q_scale, emit_bf16=emit_bf16)
    row = lambda i: (i, 0)
    const = lambda i: (0, 0)
    out_shape = [jax.ShapeDtypeStruct((n, aw), q_dtype),
                 jax.ShapeDtypeStruct((n, aw), F32),
                 jax.ShapeDtypeStruct((n, aw), F32)]
    out_specs = [pl.BlockSpec((tm, aw), row)] * 3
    if emit_bf16:
        out_shape += [jax.ShapeDtypeStruct((n, aw), BF16)] * 2
        out_specs += [pl.BlockSpec((tm, aw), row)] * 2
    out_shape.append(jax.ShapeDtypeStruct((n, 4 * rw), r_dtype))
    out_specs.append(pl.BlockSpec((tm, 4 * rw), row))
    return pl.pallas_call(
        kern,
        grid=(n // tm,),
        in_specs=[pl.BlockSpec((tm, d), row),
                  pl.BlockSpec((1, d), const),
                  _const_spec(w_bf.shape, const),
                  pl.BlockSpec((1, aw), const),
                  pl.BlockSpec((1, aw), const),
                  pl.BlockSpec((aw, aw), const)],
        out_specs=out_specs,
        out_shape=out_shape,
        compiler_params=pltpu.CompilerParams(dimension_semantics=("arbitrary",),
                                             vmem_limit_bytes=VMEM_LIMIT),
        name="proj_bf16" if emit_bf16 else "proj_f32",
    )(x2, g1, w_bf, qg, kg, bd)


def _attn_kernel(lam_ref, q_ref, k_ref, v_ref, bias_ref, sg_ref, o_ref, qs_sc, m_sc, l_sc, acc_sc,
                 *, hd, out_scale):
    i = pl.program_id(2)
    q = q_ref[0]
    lane = lax.broadcasted_iota(jnp.int32, q.shape, 1)
    zero = jnp.zeros_like(q)
    qs_sc[0:TQ, :] = jnp.where(lane < hd, q, zero)
    qs_sc[TQ:2 * TQ, :] = jnp.where(lane >= hd, q, zero)
    m_sc[...] = jnp.full(m_sc.shape, NEG, F32)
    l_sc[...] = jnp.zeros(l_sc.shape, F32)
    acc_sc[...] = jnp.zeros(acc_sc.shape, F32)

    def step(j, bias):
        start = pl.multiple_of(j * TQ, TQ)
        kj = k_ref[0, pl.ds(start, TQ), :]
        vj = v_ref[0, pl.ds(start, TQ), :]
        s = _nt(qs_sc[...], kj)
        if bias is not None:
            s = (s.reshape(2, TQ, TQ) + bias[None]).reshape(2 * TQ, TQ)
        m_old = m_sc[...]
        m_new = jnp.maximum(m_old, jnp.max(s, axis=1, keepdims=True))
        alpha = jnp.exp(m_old - m_new)
        p = jnp.exp(s - m_new)
        l_sc[...] = alpha * l_sc[...] + jnp.sum(p, axis=1, keepdims=True)
        acc_sc[...] = alpha * acc_sc[...] + _nn(p.astype(BF16), vj)
        m_sc[...] = m_new

    def pure(j, carry):
        step(j, None)
        return carry

    lax.fori_loop(0, jnp.maximum(i - 1, 0), pure, 0)

    @pl.when(i > 0)
    def _():
        step(i - 1, bias_ref[0, 1])

    step(i, bias_ref[0, 0])

    inv = 1.0 / l_sc[...]
    a = acc_sc[...] * inv
    o = a[0:TQ] - lam_ref[0, 0] * a[TQ:2 * TQ]
    ms = jnp.mean(o * o, axis=-1, keepdims=True)
    o_ref[0] = (o * lax.rsqrt(ms + EPS) * (sg_ref[...] * out_scale)).astype(o_ref.dtype)


def _attn(lam, q, k, v, bias, sg, n_heads, hd, out_scale):
    b, t, aw = q.shape
    vd = aw // n_heads
    kern = functools.partial(_attn_kernel, hd=hd, out_scale=out_scale)
    return pl.pallas_call(
        kern,
        grid=(b, n_heads, t // TQ),
        in_specs=[pl.BlockSpec(memory_space=pltpu.SMEM),
                  pl.BlockSpec((1, TQ, vd), lambda bi, h, i: (bi, i, h)),
                  pl.BlockSpec((1, t, vd), lambda bi, h, i: (bi, 0, h)),
                  pl.BlockSpec((1, t, vd), lambda bi, h, i: (bi, 0, h)),
                  pl.BlockSpec((1, 2, TQ, TQ), lambda bi, h, i: (h, 0, 0, 0)),
                  pl.BlockSpec((1, vd), lambda bi, h, i: (0, 0))],
        out_specs=pl.BlockSpec((1, TQ, vd), lambda bi, h, i: (bi, i, h)),
        out_shape=jax.ShapeDtypeStruct((b, t, aw), BF16),
        scratch_shapes=[pltpu.VMEM((2 * TQ, vd), BF16),
                        pltpu.VMEM((2 * TQ, 1), F32),
                        pltpu.VMEM((2 * TQ, 1), F32),
                        pltpu.VMEM((2 * TQ, vd), F32)],
        compiler_params=pltpu.CompilerParams(dimension_semantics=("arbitrary",) * 3,
                                             vmem_limit_bytes=VMEM_LIMIT),
        name="attn_prompt",
    )(lam, q, k, v, bias, sg)


def _decode_kernel(pt_ref, lam_ref, q_ref, kn_ref, vn_ref, bias_ref, sg_ref, *refs,
                   n_pages, n_heads, hd, out_scale):
    k_refs = refs[:n_pages]
    v_refs = refs[n_pages:2 * n_pages]
    o_ref = refs[2 * n_pages]
    del pt_ref
    q8 = q_ref[...].astype(F32)
    aw = q8.shape[1]
    vd = aw // n_heads
    ps = k_refs[0].shape[1]
    lane = lax.broadcasted_iota(jnp.int32, q8.shape, 1)
    pieces = [jnp.where((lane >= hm * hd) & (lane < (hm + 1) * hd), q8, 0.0) for hm in range(2 * n_heads)]
    qbd = jnp.concatenate(pieces, axis=0).astype(BF16)
    s_list = []
    for p in range(n_pages):
        s = _nt(qbd, k_refs[p][0].astype(BF16))
        if p == n_pages - 1:
            s = s + bias_ref[0]
        s_list.append(s)
    pad = jnp.zeros((ps - SAMPLE_PAD, aw), F32)
    k_new = jnp.concatenate([kn_ref[...].astype(F32), pad], axis=0).astype(BF16)
    v_new = jnp.concatenate([vn_ref[...].astype(F32), pad], axis=0).astype(BF16)
    s_list.append(_nt(qbd, k_new) + bias_ref[1])
    s_all = jnp.concatenate(s_list, axis=1)
    m = jnp.max(s_all, axis=1, keepdims=True)
    e = jnp.exp(s_all - m)
    l = jnp.sum(e, axis=1, keepdims=True)
    eb = e.astype(BF16)
    acc = _nn(eb[:, n_pages * ps:(n_pages + 1) * ps], v_new)
    for p in range(n_pages):
        acc = acc + _nn(eb[:, p * ps:(p + 1) * ps], v_refs[p][0].astype(BF16))
    acc = acc * (1.0 / l)
    lam = lam_ref[0, 0]
    rows = 2 * SAMPLE_PAD
    outs = []
    for h in range(n_heads):
        blk = acc[h * rows:(h + 1) * rows, h * vd:(h + 1) * vd]
        o = blk[0:SAMPLE_PAD] - lam * blk[SAMPLE_PAD:rows]
        ms = jnp.mean(o * o, axis=-1, keepdims=True)
        outs.append(o * lax.rsqrt(ms + EPS) * (sg_ref[...] * out_scale))
    o_ref[...] = jnp.concatenate(outs, axis=1).astype(o_ref.dtype)


def _decode(page_table, lam, q, kn, vn, bias, sg, ck, cv, n_heads, hd, out_scale):
    n_seq, n_pages = page_table.shape
    _, ps, aw = ck.shape
    vd = aw // n_heads
    kern = functools.partial(_decode_kernel, n_pages=n_pages, n_heads=n_heads, hd=hd, out_scale=out_scale)
    row = lambda s, pt: (s, 0)
    page_specs = [pl.BlockSpec((1, ps, aw), functools.partial(lambda s, pt, p: (pt[s, p], 0, 0), p=p))
                  for p in range(n_pages)]
    grid_spec = pltpu.PrefetchScalarGridSpec(
        num_scalar_prefetch=1,
        grid=(n_seq,),
        in_specs=[pl.BlockSpec(memory_space=pltpu.SMEM),
                  pl.BlockSpec((SAMPLE_PAD, aw), row),
                  pl.BlockSpec((SAMPLE_PAD, aw), row),
                  pl.BlockSpec((SAMPLE_PAD, aw), row),
                  pl.BlockSpec(bias.shape, lambda s, pt: (0, 0, 0)),
                  pl.BlockSpec((1, vd), lambda s, pt: (0, 0))] + page_specs + page_specs,
        out_specs=pl.BlockSpec((SAMPLE_PAD, aw), row),
    )
    return pl.pallas_call(
        kern,
        grid_spec=grid_spec,
        out_shape=jax.ShapeDtypeStruct((n_seq * SAMPLE_PAD, aw), F32),
        compiler_params=pltpu.CompilerParams(dimension_semantics=("arbitrary",),
                                             vmem_limit_bytes=VMEM_LIMIT),
        name="attn_decode",
    )(page_table, lam, q, kn, vn, bias, sg, *([ck] * n_pages), *([cv] * n_pages))


def _mid_reference(b, level):
    c, w = b.shape
    p = 1 << level
    half = p // 2
    if p >= SUBLANES:
        bp = b.reshape(c // p, p, w)
        return jnp.broadcast_to(bp[:, half - 1:half, :], (c // p, p, w)).reshape(c, w)
    b8 = b.reshape(c // SUBLANES, SUBLANES, w)
    rig = lax.broadcasted_iota(jnp.int32, b8.shape, 1)
    out = None
    for start in range(SUBLANES - p, -1, -p):
        row = jnp.broadcast_to(b8[:, start + half - 1:start + half, :], b8.shape)
        out = row if out is None else jnp.where(rig < start + p, row, out)
    return out.reshape(c, w)


def _hgrn_kernel(r_ref, lb_ref, gn_ref, tri_ref, lev_ref, o_ref, sfin_ref, st_sc, *, n_heads, kd):
    t = pl.program_id(1)
    c = HG_CHUNK
    rw = n_heads * kd
    n_levels = c.bit_length() - 1

    @pl.when(t == 0)
    def _():
        st_sc[...] = jnp.zeros(st_sc.shape, F32)

    def chunk(ci, carry):
        r0 = pl.multiple_of(ci * c, c)
        lev = lev_ref[...]
        tri = tri_ref[...]
        for h in range(n_heads):
            cols = slice(h * kd, (h + 1) * kd)
            q = r_ref[0, pl.ds(r0, c), cols].astype(F32)
            rf = r_ref[0, pl.ds(r0, c), rw + h * kd:rw + (h + 1) * kd].astype(F32)
            v = r_ref[0, pl.ds(r0, c), 2 * rw + h * kd:2 * rw + (h + 1) * kd].astype(BF16)
            rg = r_ref[0, pl.ds(r0, c), 3 * rw + h * kd:3 * rw + (h + 1) * kd].astype(F32)
            lb = lb_ref[:, cols]
            f = lb + (1.0 - lb) * jax.nn.sigmoid(rf)
            logf = jnp.log(f)
            kk = 1.0 - f
            hi = logf.astype(BF16)
            lo = (logf - hi.astype(F32)).astype(BF16)
            b = _nn(tri, hi) + _nn(tri, lo)
            a = jnp.where(lev == 0, _nt(q.astype(BF16), kk.astype(BF16)), 0.0)
            for level in range(1, n_levels + 1):
                e = jnp.exp(-jnp.abs(b - _mid_reference(b, level)))
                pm = _nt((q * e).astype(BF16), (kk * e).astype(BF16))
                a = a + jnp.where(lev == level, pm, 0.0)
            st = st_sc[h]
            o = _nn(a.astype(BF16), v) + _nt((q * jnp.exp(b)).astype(BF16), st.astype(BF16))
            b_last = b[c - 1:c, :]
            k_hat = (kk * jnp.exp(b_last - b)).astype(BF16)
            st_sc[h] = jnp.exp(b_last) * st + _tn(v, k_hat)
            ms = jnp.mean(o * o, axis=-1, keepdims=True)
            on = o * lax.rsqrt(ms + EPS) * gn_ref[...] * (rg * jax.nn.sigmoid(rg))
            o_ref[0, pl.ds(r0, c), cols] = on.astype(o_ref.dtype)
        return carry

    lax.fori_loop(0, HG_BLOCK // c, chunk, 0)

    @pl.when(t == pl.num_programs(1) - 1)
    def _():
        for h in range(n_heads):
            sfin_ref[0, h] = st_sc[h].T


def _hgrn_prompt(r, lb, gn, n_heads, kd):
    b, t, _ = r.shape
    c = HG_CHUNK
    idx = np.arange(c)
    x = idx[:, None] ^ idx[None, :]
    lev = np.where(idx[:, None] > idx[None, :], np.floor(np.log2(np.maximum(x, 1))).astype(np.int32) + 1, -1)
    lev = np.where(idx[:, None] == idx[None, :], 0, lev).astype(np.int32)
    tri = (idx[:, None] >= idx[None, :]).astype(np.float32)
    kern = functools.partial(_hgrn_kernel, n_heads=n_heads, kd=kd)
    return pl.pallas_call(
        kern,
        grid=(b, t // HG_BLOCK),
        in_specs=[pl.BlockSpec((1, HG_BLOCK, r.shape[2]), lambda bi, ti: (bi, ti, 0)),
                  pl.BlockSpec((1, n_heads * kd), lambda bi, ti: (0, 0)),
                  pl.BlockSpec((1, kd), lambda bi, ti: (0, 0)),
                  pl.BlockSpec((c, c), lambda bi, ti: (0, 0)),
                  pl.BlockSpec((c, c), lambda bi, ti: (0, 0))],
        out_specs=[pl.BlockSpec((1, HG_BLOCK, n_heads * kd), lambda bi, ti: (bi, ti, 0)),
                   pl.BlockSpec((1, n_heads, kd, kd), lambda bi, ti: (bi, 0, 0, 0))],
        out_shape=[jax.ShapeDtypeStruct((b, t, n_heads * kd), BF16),
                   jax.ShapeDtypeStruct((b, n_heads, kd, kd), F32)],
        scratch_shapes=[pltpu.VMEM((n_heads, kd, kd), F32)],
        compiler_params=pltpu.CompilerParams(dimension_semantics=("arbitrary", "arbitrary"),
                                             vmem_limit_bytes=VMEM_LIMIT),
        name="hgrn_prompt",
    )(r, lb, gn, jnp.asarray(tri, BF16), jnp.asarray(lev))


def _hgrn_step_kernel(r_ref, lb_ref, gn_ref, s_ref, o_ref, so_ref, *, n_heads, kd, n_tok, n_seq):
    rw = n_heads * kd
    rowid = lax.broadcasted_iota(jnp.int32, (SAMPLE_PAD, kd), 0)
    for g in range(n_seq):
        rows = slice(g * SAMPLE_PAD, (g + 1) * SAMPLE_PAD)
        for h in range(n_heads):
            cols = slice(h * kd, (h + 1) * kd)
            q = r_ref[rows, h * kd:(h + 1) * kd]
            rf = r_ref[rows, rw + h * kd:rw + (h + 1) * kd]
            v = r_ref[rows, 2 * rw + h * kd:2 * rw + (h + 1) * kd]
            rg = r_ref[rows, 3 * rw + h * kd:3 * rw + (h + 1) * kd]
            lb = lb_ref[:, cols]
            f = lb + (1.0 - lb) * jax.nn.sigmoid(rf)
            ft = f.T
            kt = (1.0 - f).T
            qt = q.T
            s = s_ref[g, h]
            o = jnp.zeros((SAMPLE_PAD, kd), F32)
            for tok in range(n_tok):
                s = ft[:, tok:tok + 1] * s + kt[:, tok:tok + 1] * v[tok:tok + 1, :]
                ot = jnp.sum(qt[:, tok:tok + 1] * s, axis=0, keepdims=True)
                o = jnp.where(rowid == tok, ot, o)
            so_ref[g, h] = s
            ms = jnp.mean(o * o, axis=-1, keepdims=True)
            on = o * lax.rsqrt(ms + EPS) * gn_ref[...] * (rg * jax.nn.sigmoid(rg))
            o_ref[rows, cols] = on.astype(o_ref.dtype)


def _hgrn_sample(r, lb, gn, state, n_heads, kd, n_tok, seq_per_step=2):
    n_seq = state.shape[0]
    kern = functools.partial(_hgrn_step_kernel, n_heads=n_heads, kd=kd, n_tok=n_tok, n_seq=seq_per_step)
    rows = seq_per_step * SAMPLE_PAD
    return pl.pallas_call(
        kern,
        grid=(n_seq // seq_per_step,),
        in_specs=[pl.BlockSpec((rows, r.shape[1]), lambda i: (i, 0)),
                  pl.BlockSpec((1, n_heads * kd), lambda i: (0, 0)),
                  pl.BlockSpec((1, kd), lambda i: (0, 0)),
                  pl.BlockSpec((seq_per_step, n_heads, kd, kd), lambda i: (i, 0, 0, 0))],
        out_specs=[pl.BlockSpec((rows, n_heads * kd), lambda i: (i, 0)),
                   pl.BlockSpec((seq_per_step, n_heads, kd, kd), lambda i: (i, 0, 0, 0))],
        out_shape=[jax.ShapeDtypeStruct((n_seq * SAMPLE_PAD, n_heads * kd), F32),
                   jax.ShapeDtypeStruct(state.shape, F32)],
        compiler_params=pltpu.CompilerParams(dimension_semantics=("arbitrary",),
                                             vmem_limit_bytes=VMEM_LIMIT),
        name="hgrn_sample",
    )(r, lb, gn, state)


def _ffn_kernel(*refs, sample, aw):
    if sample:
        (x_ref, oa_ref, or_ref, wo_ref, g2_ref, wg_ref, wu_ref, cw_ref, cb_ref, wd_ref, prev_ref,
         y_ref, a_ref, x1_sc, h2_sc, acc_sc) = refs
    else:
        (x_ref, oa_ref, or_ref, wo_ref, g2_ref, wg_ref, wu_ref, cw_ref, cb_ref, wd_ref,
         y_ref, a_ref, x1_sc, h2_sc, acc_sc, carry_sc) = refs

        @pl.when(pl.program_id(1) == 0)
        def _():
            carry_sc[...] = jnp.zeros(carry_sc.shape, F32)

    x = x_ref[...].astype(F32)
    x1 = x + _nn(oa_ref[...].astype(BF16), wo_ref[0:aw, :]) + _nn(or_ref[...].astype(BF16), wo_ref[aw:, :])
    x1_sc[...] = x1
    ms = jnp.mean(x1 * x1, axis=-1, keepdims=True)
    h2_sc[...] = (x1 * lax.rsqrt(ms + EPS) * g2_ref[...]).astype(BF16)
    acc_sc[...] = jnp.zeros(acc_sc.shape, F32)
    tm = x.shape[0]
    n_f = wg_ref.shape[0]

    def ftile(fi, carry):
        h2 = h2_sc[...]
        a = _nn(h2, wg_ref[fi])
        u = _nn(h2, wu_ref[fi])
        rowid = lax.broadcasted_iota(jnp.int32, a.shape, 0)
        am1 = pltpu.roll(a, 1, 0)
        am2 = pltpu.roll(a, 2, 0)
        if sample:
            rig = rowid & (SAMPLE_PAD - 1)
            prev = prev_ref[fi]
            am1 = jnp.where(rig == 0, pltpu.roll(prev, tm - 1, 0), am1)
            am2 = jnp.where(rig < 2, prev, am2)
            a_ref[fi] = a
        else:
            last = carry_sc[fi]
            am1 = jnp.where(rowid == 0, last[SUBLANES - 1:SUBLANES, :], am1)
            am2 = jnp.where(rowid == 0, last[SUBLANES - 2:SUBLANES - 1, :],
                            jnp.where(rowid == 1, last[SUBLANES - 1:SUBLANES, :], am2))
            tail = a[tm - SUBLANES:tm, :]
            carry_sc[fi] = tail
            a_ref[0, fi] = tail
        cw = cw_ref[fi]
        cv = cb_ref[fi] + cw[0:1, :] * am2 + cw[1:2, :] * am1 + cw[2:3, :] * a
        gate = (cv * jax.nn.sigmoid(cv) * u).astype(BF16)
        acc_sc[...] += _nn(gate, wd_ref[fi])
        return carry

    lax.fori_loop(0, n_f, ftile, 0)
    y_ref[...] = x1_sc[...] + acc_sc[...]


def _ffn(x2, oa, orec, wo, g2, wg3, wu3, cw3, cb3, wd3, prev3, tm, n_batch):
    n, d = x2.shape
    aw = oa.shape[1]
    n_f, _, tf = wg3.shape
    sample = prev3 is not None
    kern = functools.partial(_ffn_kernel, sample=sample, aw=aw)
    scratch = [pltpu.VMEM((tm, d), F32), pltpu.VMEM((tm, d), BF16), pltpu.VMEM((tm, d), F32)]
    if sample:
        grid = (n // tm,)
        row = lambda i: (i, 0)
        c2 = lambda i: (0, 0)
        c3 = lambda i: (0, 0, 0)
        extra_in = [pl.BlockSpec((n_f, tm, tf), lambda i: (0, i, 0))]
        extra_args = [prev3]
        a_shape = jax.ShapeDtypeStruct((n_f, n, tf), F32)
        a_spec = pl.BlockSpec((n_f, tm, tf), lambda i: (0, i, 0))
        sem = ("arbitrary",)
    else:
        per_b = n // n_batch // tm
        grid = (n_batch, per_b)
        row = lambda b, i: (b * per_b + i, 0)
        c2 = lambda b, i: (0, 0)
        c3 = lambda b, i: (0, 0, 0)
        extra_in = []
        extra_args = []
        a_shape = jax.ShapeDtypeStruct((n_batch, n_f, SUBLANES, tf), F32)
        a_spec = pl.BlockSpec((1, n_f, SUBLANES, tf), lambda b, i: (b, 0, 0, 0))
        scratch.append(pltpu.VMEM((n_f, SUBLANES, tf), F32))
        sem = ("arbitrary", "arbitrary")
    return pl.pallas_call(
        kern,
        grid=grid,
        in_specs=[pl.BlockSpec((tm, d), row),
                  pl.BlockSpec((tm, aw), row),
                  pl.BlockSpec((tm, aw), row),
                  _const_spec(wo.shape, c2),
                  pl.BlockSpec((1, d), c2),
                  _const_spec(wg3.shape, c3),
                  _const_spec(wu3.shape, c3),
                  pl.BlockSpec(cw3.shape, c3),
                  pl.BlockSpec(cb3.shape, c3),
                  _const_spec(wd3.shape, c3)] + extra_in,
        out_specs=[pl.BlockSpec((tm, d), row), a_spec],
        out_shape=[jax.ShapeDtypeStruct((n, d), F32), a_shape],
        scratch_shapes=scratch,
        compiler_params=pltpu.CompilerParams(dimension_semantics=sem, vmem_limit_bytes=VMEM_LIMIT),
        name="ffn_sample" if sample else "ffn_prompt",
    )(x2, oa, orec, wo, g2, wg3, wu3, cw3, cb3, wd3, *extra_args)


def kernel(x_prompt, x_sample, cache_k, cache_v, page_table, state_hgrn, state_conv, rel_bias, norm1_g, w_in,
           qk_norm_g, lambda_qk, subln_g, hgrn_lb, hgrn_onorm_g, w_out, norm2_g, w_gate, w_up, conv_w, conv_b,
           w_down):
    bp, tp, d = x_prompt.shape
    bs, ts, _ = x_sample.shape
    depth = w_in.shape[0]
    assert depth == 1 and CONV_W - 1 <= ts <= SAMPLE_PAD and conv_w.shape[1] == CONV_W
    n_heads, _, hd = cache_k.shape[3:]
    vd = cache_v.shape[4]
    aw = n_heads * vd
    r_heads, kd = state_hgrn.shape[2], state_hgrn.shape[3]
    rw = r_heads * kd
    dff = w_gate.shape[2]
    page_size = cache_k.shape[2]
    assert page_size >= MAX_DISTANCE and tp % TQ == 0 and tp % HG_BLOCK == 0 and dff % FF_TILE == 0
    n_f = dff // FF_TILE
    l = 0
    lam_init = 0.8 - 0.6 * math.exp(-0.3 * l)
    out_scale = 1.0 - lam_init

    biasp, biass, lam, lb = _tables(rel_bias, lambda_qk[l], hgrn_lb, n_heads, ts, page_size, lam_init)

    w_in_bf = w_in[l].astype(BF16)
    wo_bf = w_out[l].astype(BF16)
    wg3 = w_gate[l].astype(BF16).reshape(d, n_f, FF_TILE).transpose(1, 0, 2)
    wu3 = w_up[l].astype(BF16).reshape(d, n_f, FF_TILE).transpose(1, 0, 2)
    wd3 = w_down[l].astype(BF16).reshape(n_f, FF_TILE, d)
    cw3 = conv_w[l].reshape(CONV_W, n_f, FF_TILE).transpose(1, 0, 2)
    cb3 = conv_b[l].reshape(n_f, 1, FF_TILE)
    g1 = norm1_g[l].reshape(1, d)
    g2 = norm2_g[l].reshape(1, d)
    qg = jnp.tile(qk_norm_g[l, 0].reshape(1, 2 * hd), (1, n_heads))
    kg = jnp.tile(qk_norm_g[l, 1].reshape(1, 2 * hd), (1, n_heads))
    seg = np.arange(aw) // hd
    bd = jnp.asarray((seg[:, None] == seg[None, :]).astype(np.float32) / hd, BF16)
    sg = subln_g[l].reshape(1, vd)
    gn = hgrn_onorm_g[l].reshape(1, kd)
    q_scale = hd ** -0.5

    xp2 = x_prompt.reshape(bp * tp, d)
    qb, kf, vf, kb, vb, rp = _proj(xp2, g1, w_in_bf, qg, kg, bd, aw, rw, q_scale, True, BF16, BF16, 512)
    o_att = _attn(lam, qb.reshape(bp, tp, aw), kb.reshape(bp, tp, aw), vb.reshape(bp, tp, aw), biasp, sg,
                  n_heads, hd, out_scale)
    o_rec, s_p = _hgrn_prompt(rp.reshape(bp, tp, 4 * rw), lb, gn, r_heads, kd)
    yp, a_tail = _ffn(xp2, o_att.reshape(bp * tp, aw), o_rec.reshape(bp * tp, rw), wo_bf, g2, wg3, wu3, cw3, cb3,
                      wd3, None, 512, bp)
    k_prompt = kf.reshape(1, bp, tp, n_heads, 2, hd)
    v_prompt = vf.reshape(1, bp, tp, n_heads, vd)
    conv_prompt = a_tail.transpose(0, 2, 1, 3).reshape(bp, SUBLANES, dff)[:, SUBLANES - (CONV_W - 1):][None]

    xs_pad = jnp.pad(x_sample, ((0, 0), (0, SAMPLE_PAD - ts), (0, 0))).reshape(bs * SAMPLE_PAD, d)
    qs, ksf, vsf, rs = _proj(xs_pad, g1, w_in_bf, qg, kg, bd, aw, rw, q_scale, False, F32, F32, 512)
    ck = cache_k[l].reshape(cache_k.shape[1], page_size, aw)
    cv = cache_v[l].reshape(cache_v.shape[1], page_size, aw)
    o_att_s = _decode(page_table, lam, qs, ksf, vsf, biass, sg, ck, cv, n_heads, hd, out_scale)
    o_rec_s, s_s = _hgrn_sample(rs, lb, gn, state_hgrn[l], r_heads, kd, ts)
    prev = jnp.pad(state_conv[l], ((0, 0), (0, SAMPLE_PAD - (CONV_W - 1)), (0, 0)))
    prev3 = prev.reshape(bs * SAMPLE_PAD, n_f, FF_TILE).transpose(1, 0, 2)
    ys, a_s = _ffn(xs_pad, o_att_s, o_rec_s, wo_bf, g2, wg3, wu3, cw3, cb3, wd3, prev3, 256, 1)
    unpad = lambda z: z.reshape(bs, SAMPLE_PAD, -1)[:, :ts]
    y_sample = unpad(ys)
    k_sample = unpad(ksf).reshape(1, bs, ts, n_heads, 2, hd)
    v_sample = unpad(vsf).reshape(1, bs, ts, n_heads, vd)
    conv_sample = a_s.transpose(1, 0, 2).reshape(bs, SAMPLE_PAD, dff)[:, ts - (CONV_W - 1):ts][None]

    return (yp.reshape(bp, tp, d), y_sample, k_prompt, v_prompt, s_p[None], conv_prompt,
            k_sample, v_sample, s_s[None], conv_sample)
```

```python
import functools
import math

import numpy as np
import jax
import jax.numpy as jnp
from jax import lax
from jax.experimental import pallas as pl
from jax.experimental.pallas import tpu as pltpu

F32 = jnp.float32
BF16 = jnp.bfloat16

EPS = 1e-6
N_BUCKETS = 32
MAX_DISTANCE = 128
CONV_W = 3
NEG = -1e30
LOG2E = math.log2(math.e)
LANES = 128
SUBLANES = 8

TQ = 256
HG_CHUNK = 128
HG_BLOCK = 512
SAMPLE_PAD = 8
FF_TILE = 256
VMEM_LIMIT = 56 * 1024 * 1024


def _const_spec(shape, index_map):
    return pl.BlockSpec(shape, index_map, pipeline_mode=pl.Buffered(1))


def _nt(a, b):
    return lax.dot_general(a, b, (((1,), (1,)), ((), ())), preferred_element_type=F32)


def _tn(a, b):
    return lax.dot_general(a, b, (((0,), (0,)), ((), ())), preferred_element_type=F32)


def _nn(a, b):
    return jnp.dot(a, b, preferred_element_type=F32)


def _rel_bucket_np(n):
    n = np.maximum(n, 0)
    max_exact = N_BUCKETS // 2
    nf = np.maximum(n, 1).astype(np.float32)
    large = max_exact + (np.log(nf / np.float32(max_exact)) / np.float32(math.log(MAX_DISTANCE / max_exact))
                         * np.float32(N_BUCKETS - max_exact)).astype(np.int32)
    large = np.minimum(large, N_BUCKETS - 1)
    return np.where(n < max_exact, n, large).astype(np.int32)


def _tables_kernel(relb_ref, lq_ref, hlb_ref, bkp_ref, bks_ref,
                   biasp_ref, biass_ref, lam_ref, lb_ref, *, n_heads, lam_init):
    for h in range(n_heads):
        far = relb_ref[N_BUCKETS - 1, h]
        for kind in range(2):
            bk = bkp_ref[kind]
            acc = jnp.where(bk < 0, NEG, 0.0).astype(F32)
            for b in range(N_BUCKETS - 1):
                acc = jnp.where(bk == b, (relb_ref[b, h] - far) * LOG2E, acc)
            biasp_ref[h, kind] = acc
            rows = bks_ref.shape[1] // n_heads
            bk = bks_ref[kind, h * rows:(h + 1) * rows, :]
            acc = jnp.where(bk < 0, NEG, 0.0).astype(F32)
            for b in range(N_BUCKETS - 1):
                acc = jnp.where(bk == b, (relb_ref[b, h] - far) * LOG2E, acc)
            biass_ref[kind, h * rows:(h + 1) * rows, :] = acc
    lq = lq_ref[...].astype(F32)
    s1 = jnp.sum(lq[0:1] * lq[1:2], axis=1, keepdims=True)
    s2 = jnp.sum(lq[2:3] * lq[3:4], axis=1, keepdims=True)
    lam = jnp.exp(s1) - jnp.exp(s2) + lam_init
    lam_ref[...] = jnp.broadcast_to(lam, lam_ref.shape)
    hl = hlb_ref[...].astype(F32)
    mx = jnp.max(hl, axis=0, keepdims=True)
    e = jnp.exp(hl - mx)
    lb_ref[...] = e[0:1] / jnp.sum(e, axis=0, keepdims=True)


def _tables(rel_bias, lambda_qk_l, hgrn_lb, n_heads, n_tok, page_size, lam_init):
    c = np.arange(TQ)[:, None]
    r = np.arange(TQ)[None, :]
    diag = np.where(c <= r, _rel_bucket_np(r - c), -1)
    prev = _rel_bucket_np(TQ + r - c)
    bkp = jnp.asarray(np.stack([diag, prev]).astype(np.int32))
    t = np.tile(np.arange(SAMPLE_PAD), 2 * n_heads)[:, None]
    cc = np.arange(page_size)[None, :]
    last_page = _rel_bucket_np(page_size + t - cc)
    new_page = np.where(cc <= np.minimum(t, n_tok - 1), _rel_bucket_np(t - cc), -1)
    bks = jnp.asarray(np.stack([last_page, new_page]).astype(np.int32))
    rows_s = 2 * n_heads * SAMPLE_PAD
    kern = functools.partial(_tables_kernel, n_heads=n_heads, lam_init=lam_init)
    return pl.pallas_call(
        kern,
        out_shape=(jax.ShapeDtypeStruct((n_heads, 2, TQ, TQ), F32),
                   jax.ShapeDtypeStruct((2, rows_s, page_size), F32),
                   jax.ShapeDtypeStruct((SUBLANES, LANES), F32),
                   jax.ShapeDtypeStruct((1, hgrn_lb.shape[1]), F32)),
        in_specs=[pl.BlockSpec(memory_space=pltpu.SMEM),
                  pl.BlockSpec(memory_space=pltpu.VMEM),
                  pl.BlockSpec(memory_space=pltpu.VMEM),
                  pl.BlockSpec(memory_space=pltpu.VMEM),
                  pl.BlockSpec(memory_space=pltpu.VMEM)],
        name="tables",
    )(rel_bias, lambda_qk_l, hgrn_lb, bkp, bks)


def _proj_kernel(x_ref, g1_ref, w_ref, qg_ref, kg_ref, bd_ref, *out_refs, aw, rw, q_scale, emit_bf16):
    if emit_bf16:
        q_ref, kf_ref, vf_ref, kb_ref, vb_ref, r_ref = out_refs
    else:
        q_ref, kf_ref, vf_ref, r_ref = out_refs
    x = x_ref[...].astype(F32)
    ms = jnp.mean(x * x, axis=-1, keepdims=True)
    h = (x * lax.rsqrt(ms + EPS) * g1_ref[...]).astype(BF16)

    def seg_norm(z, g):
        msq = _nn((z * z).astype(BF16), bd_ref[...])
        return z * lax.rsqrt(msq + EPS) * g

    zq = _nn(h, w_ref[:, 0:aw])
    q_ref[...] = (seg_norm(zq, qg_ref[...]) * q_scale).astype(q_ref.dtype)
    zk = _nn(h, w_ref[:, aw:2 * aw])
    kn = seg_norm(zk, kg_ref[...])
    kf_ref[...] = kn
    zv = _nn(h, w_ref[:, 2 * aw:3 * aw])
    vf_ref[...] = zv
    if emit_bf16:
        kb_ref[...] = kn.astype(BF16)
        vb_ref[...] = zv.astype(BF16)
    for g in range(4):
        z = _nn(h, w_ref[:, 3 * aw + g * rw:3 * aw + (g + 1) * rw])
        r_ref[:, g * rw:(g + 1) * rw] = z.astype(r_ref.dtype)


def _proj(x2, g1, w_bf, qg, kg, bd, aw, rw, q_scale, emit_bf16, q_dtype, r_dtype, tm):
    n, d = x2.shape
    kern = functools.partial(_proj_kernel, aw=aw, rw=rw, q_scale=q_scale, emit_bf16=emit_bf16)
    row = lambda i: (i, 0)
    const = lambda i: (0, 0)
    out_shape = [jax.ShapeDtypeStruct((n, aw), q_dtype),
                 jax.ShapeDtypeStruct((n, aw), F32),
                 jax.ShapeDtypeStruct((n, aw), F32)]
    out_specs = [pl.BlockSpec((tm, aw), row)] * 3
    if emit_bf16:
        out_shape += [jax.ShapeDtypeStruct((n, aw), BF16)] * 2
        out_specs += [pl.BlockSpec((tm, aw), row)] * 2
    out_shape.append(jax.ShapeDtypeStruct((n, 4 * rw), r_dtype))
    out_specs.append(pl.BlockSpec((tm, 4 * rw), row))
    return pl.pallas_call(
        kern,
        grid=(n // tm,),
        in_specs=[pl.BlockSpec((tm, d), row),
                  pl.BlockSpec((1, d), const),
                  _const_spec(w_bf.shape, const),
                  pl.BlockSpec((1, aw), const),
                  pl.BlockSpec((1, aw), const),
                  pl.BlockSpec((aw, aw), const)],
        out_specs=out_specs,
        out_shape=out_shape,
        compiler_params=pltpu.CompilerParams(dimension_semantics=("arbitrary",),
                                             vmem_limit_bytes=VMEM_LIMIT),
        name="proj_bf16" if emit_bf16 else "proj_f32",
    )(x2, g1, w_bf, qg, kg, bd)


def _attn_kernel(lam_ref, q_ref, k_ref, v_ref, bias_ref, sg_ref, o_ref, qst_sc, vt_sc, s_sc, m_sc, l_sc, acc_sc,
                 *, hd, out_scale):
    i = pl.program_id(2)

    @pl.when(i == 0)
    def _():
        def xpose(c, carry):
            start = pl.multiple_of(c * TQ, TQ)
            vt_sc[c] = v_ref[0, pl.ds(start, TQ), :].astype(F32).T.astype(BF16)
            return carry
        lax.fori_loop(0, vt_sc.shape[0], xpose, 0)

    qt = q_ref[0].astype(F32).T
    row = lax.broadcasted_iota(jnp.int32, qt.shape, 0)
    qst_sc[...] = jnp.concatenate([jnp.where(row < hd, qt, 0.0), jnp.where(row >= hd, qt, 0.0)],
                                  axis=1).astype(BF16)
    m_sc[...] = jnp.full(m_sc.shape, NEG, F32)
    l_sc[...] = jnp.zeros(l_sc.shape, F32)
    acc_sc[...] = jnp.zeros(acc_sc.shape, F32)

    def scores(j):
        start = pl.multiple_of(j * TQ, TQ)
        return _nn(k_ref[0, pl.ds(start, TQ), :], qst_sc[...])

    s_sc[0] = scores(0)

    def step(j, kind, prefetch):
        slot = j & 1
        s = s_sc[slot]
        if prefetch:
            s_sc[1 - slot] = scores(j + 1)
        if kind is not None:
            bias = bias_ref[0, kind]
            s = jnp.concatenate([s[:, 0:TQ] + bias, s[:, TQ:2 * TQ] + bias], axis=1)
        m_old = m_sc[...]
        m_new = jnp.maximum(m_old, jnp.max(s, axis=0, keepdims=True))
        alpha = jnp.exp2(m_old - m_new)
        p = jnp.exp2(s - m_new)
        l_sc[...] = alpha * l_sc[...] + jnp.sum(p, axis=0, keepdims=True)
        acc_sc[...] = alpha * acc_sc[...] + _nn(vt_sc[j], p.astype(BF16))
        m_sc[...] = m_new

    def pure(j, carry):
        step(j, None, True)
        return carry

    lax.fori_loop(0, jnp.maximum(i - 1, 0), pure, 0)

    @pl.when(i > 0)
    def _():
        step(i - 1, 1, True)

    step(i, 0, False)

    a = acc_sc[...] * (1.0 / l_sc[...])
    ot = a[:, 0:TQ] - lam_ref[0, 0] * a[:, TQ:2 * TQ]
    ms = jnp.mean(ot * ot, axis=0, keepdims=True)
    o = (ot * lax.rsqrt(ms + EPS)).T
    o_ref[0] = (o * (sg_ref[...] * out_scale)).astype(o_ref.dtype)


def _attn(lam, q, k, v, bias, sg, n_heads, hd, out_scale):
    b, t, aw = q.shape
    vd = aw // n_heads
    kern = functools.partial(_attn_kernel, hd=hd, out_scale=out_scale)
    return pl.pallas_call(
        kern,
        grid=(b, n_heads, t // TQ),
        in_specs=[pl.BlockSpec(memory_space=pltpu.SMEM),
                  pl.BlockSpec((1, TQ, vd), lambda bi, h, i: (bi, i, h)),
                  pl.BlockSpec((1, t, vd), lambda bi, h, i: (bi, 0, h)),
                  pl.BlockSpec((1, t, vd), lambda bi, h, i: (bi, 0, h)),
                  pl.BlockSpec((1, 2, TQ, TQ), lambda bi, h, i: (h, 0, 0, 0)),
                  pl.BlockSpec((1, vd), lambda bi, h, i: (0, 0))],
        out_specs=pl.BlockSpec((1, TQ, vd), lambda bi, h, i: (bi, i, h)),
        out_shape=jax.ShapeDtypeStruct((b, t, aw), BF16),
        scratch_shapes=[pltpu.VMEM((vd, 2 * TQ), BF16),
                        pltpu.VMEM((t // TQ, vd, TQ), BF16),
                        pltpu.VMEM((2, TQ, 2 * TQ), F32),
                        pltpu.VMEM((1, 2 * TQ), F32),
                        pltpu.VMEM((1, 2 * TQ), F32),
                        pltpu.VMEM((vd, 2 * TQ), F32)],
        compiler_params=pltpu.CompilerParams(dimension_semantics=("arbitrary",) * 3,
                                             vmem_limit_bytes=VMEM_LIMIT),
        name="attn_prompt",
    )(lam, q, k, v, bias, sg)


def _decode_kernel(pt_ref, lam_ref, q_ref, kn_ref, vn_ref, bias_ref, sg_ref, *refs,
                   n_pages, n_heads, hd, out_scale):
    k_refs = refs[:n_pages]
    v_refs = refs[n_pages:2 * n_pages]
    o_ref = refs[2 * n_pages]
    del pt_ref
    q8 = q_ref[...].astype(F32)
    aw = q8.shape[1]
    vd = aw // n_heads
    ps = k_refs[0].shape[2]
    lane = lax.broadcasted_iota(jnp.int32, q8.shape, 1)
    pieces = [jnp.where((lane >= hm * hd) & (lane < (hm + 1) * hd), q8, 0.0) for hm in range(2 * n_heads)]
    qbd = jnp.concatenate(pieces, axis=0).astype(BF16)
    s_list = []
    for p in range(n_pages):
        s = _nn(qbd, k_refs[p][0].astype(BF16))
        if p == n_pages - 1:
            s = s + bias_ref[0]
        s_list.append(s)
    pad = jnp.zeros((ps - SAMPLE_PAD, aw), F32)
    k_new = jnp.concatenate([kn_ref[...].astype(F32), pad], axis=0).astype(BF16)
    v_new = jnp.concatenate([vn_ref[...].astype(F32), pad], axis=0).astype(BF16)
    s_list.append(_nt(qbd, k_new) + bias_ref[1])
    s_all = jnp.concatenate(s_list, axis=1)
    m = jnp.max(s_all, axis=1, keepdims=True)
    e = jnp.exp2(s_all - m)
    inv = 1.0 / jnp.sum(e, axis=1, keepdims=True)
    eb = e.astype(BF16)
    lam = lam_ref[0, 0]
    rows = 2 * SAMPLE_PAD
    outs = []
    for h in range(n_heads):
        eh = eb[h * rows:(h + 1) * rows]
        acc = _nn(eh[:, n_pages * ps:(n_pages + 1) * ps], v_new[:, h * vd:(h + 1) * vd])
        for p in range(n_pages):
            vh = v_refs[p][0, pl.ds(h, ps, stride=n_heads), :].astype(BF16)
            acc = acc + _nn(eh[:, p * ps:(p + 1) * ps], vh)
        acc = acc * inv[h * rows:(h + 1) * rows]
        o = acc[0:SAMPLE_PAD] - lam * acc[SAMPLE_PAD:rows]
        ms = jnp.mean(o * o, axis=-1, keepdims=True)
        outs.append(o * lax.rsqrt(ms + EPS) * (sg_ref[...] * out_scale))
    o_ref[...] = jnp.concatenate(outs, axis=1).astype(o_ref.dtype)


def _decode(page_table, lam, q, kn, vn, bias, sg, ck, cv, n_heads, hd, out_scale):
    n_seq, n_pages = page_table.shape
    _, aw, ps = ck.shape
    vd = aw // n_heads
    kern = functools.partial(_decode_kernel, n_pages=n_pages, n_heads=n_heads, hd=hd, out_scale=out_scale)
    row = lambda s, pt: (s, 0)
    page_idx = [functools.partial(lambda s, pt, p: (pt[s, p], 0, 0), p=p) for p in range(n_pages)]
    k_specs = [pl.BlockSpec((1, aw, ps), im) for im in page_idx]
    v_specs = [pl.BlockSpec((1,) + cv.shape[1:], im) for im in page_idx]
    grid_spec = pltpu.PrefetchScalarGridSpec(
        num_scalar_prefetch=1,
        grid=(n_seq,),
        in_specs=[pl.BlockSpec(memory_space=pltpu.SMEM),
                  pl.BlockSpec((SAMPLE_PAD, aw), row),
                  pl.BlockSpec((SAMPLE_PAD, aw), row),
                  pl.BlockSpec((SAMPLE_PAD, aw), row),
                  pl.BlockSpec(bias.shape, lambda s, pt: (0, 0, 0)),
                  pl.BlockSpec((1, vd), lambda s, pt: (0, 0))] + k_specs + v_specs,
        out_specs=pl.BlockSpec((SAMPLE_PAD, aw), row),
    )
    return pl.pallas_call(
        kern,
        grid_spec=grid_spec,
        out_shape=jax.ShapeDtypeStruct((n_seq * SAMPLE_PAD, aw), F32),
        compiler_params=pltpu.CompilerParams(dimension_semantics=("arbitrary",),
                                             vmem_limit_bytes=VMEM_LIMIT),
        name="attn_decode",
    )(page_table, lam, q, kn, vn, bias, sg, *([ck] * n_pages), *([cv] * n_pages))


def _mid_reference(b, level):
    c, w = b.shape
    p = 1 << level
    half = p // 2
    if p >= SUBLANES:
        bp = b.reshape(c // p, p, w)
        return jnp.broadcast_to(bp[:, half - 1:half, :], (c // p, p, w)).reshape(c, w)
    b8 = b.reshape(c // SUBLANES, SUBLANES, w)
    rig = lax.broadcasted_iota(jnp.int32, b8.shape, 1)
    out = None
    for start in range(SUBLANES - p, -1, -p):
        row = jnp.broadcast_to(b8[:, start + half - 1:start + half, :], b8.shape)
        out = row if out is None else jnp.where(rig < start + p, row, out)
    return out.reshape(c, w)


def _hgrn_kernel(r_ref, lb_ref, gn_ref, tri_ref, lev_ref, o_ref, sfin_ref, st_sc, *, n_heads, kd):
    t = pl.program_id(1)
    c = HG_CHUNK
    rw = n_heads * kd
    n_levels = c.bit_length() - 1

    @pl.when(t == 0)
    def _():
        st_sc[...] = jnp.zeros(st_sc.shape, F32)

    def chunk(ci, carry):
        r0 = pl.multiple_of(ci * c, c)
        lev = lev_ref[...]
        tri = tri_ref[...]
        for h in range(n_heads):
            cols = slice(h * kd, (h + 1) * kd)
            q = r_ref[0, pl.ds(r0, c), cols].astype(F32)
            rf = r_ref[0, pl.ds(r0, c), rw + h * kd:rw + (h + 1) * kd].astype(F32)
            v = r_ref[0, pl.ds(r0, c), 2 * rw + h * kd:2 * rw + (h + 1) * kd].astype(BF16)
            rg = r_ref[0, pl.ds(r0, c), 3 * rw + h * kd:3 * rw + (h + 1) * kd].astype(F32)
            lb = lb_ref[:, cols]
            f = lb + (1.0 - lb) * jax.nn.sigmoid(rf)
            logf = jnp.log(f)
            kk = 1.0 - f
            hi = logf.astype(BF16)
            lo = (logf - hi.astype(F32)).astype(BF16)
            b = _nn(tri, hi) + _nn(tri, lo)
            a = jnp.where(lev == 0, _nt(q.astype(BF16), kk.astype(BF16)), 0.0)
            for level in range(1, n_levels + 1):
                e = jnp.exp(-jnp.abs(b - _mid_reference(b, level)))
                pm = _nt((q * e).astype(BF16), (kk * e).astype(BF16))
                a = a + jnp.where(lev == level, pm, 0.0)
            st = st_sc[h]
            o = _nn(a.astype(BF16), v) + _nt((q * jnp.exp(b)).astype(BF16), st.astype(BF16))
            b_last = b[c - 1:c, :]
            k_hat = (kk * jnp.exp(b_last - b)).astype(BF16)
            st_sc[h] = jnp.exp(b_last) * st + _tn(v, k_hat)
            ms = jnp.mean(o * o, axis=-1, keepdims=True)
            on = o * lax.rsqrt(ms + EPS) * gn_ref[...] * (rg * jax.nn.sigmoid(rg))
            o_ref[0, pl.ds(r0, c), cols] = on.astype(o_ref.dtype)
        return carry

    lax.fori_loop(0, HG_BLOCK // c, chunk, 0)

    @pl.when(t == pl.num_programs(1) - 1)
    def _():
        for h in range(n_heads):
            sfin_ref[0, h] = st_sc[h].T


def _hgrn_prompt(r, lb, gn, n_heads, kd):
    b, t, _ = r.shape
    c = HG_CHUNK
    idx = np.arange(c)
    x = idx[:, None] ^ idx[None, :]
    lev = np.where(idx[:, None] > idx[None, :], np.floor(np.log2(np.maximum(x, 1))).astype(np.int32) + 1, -1)
    lev = np.where(idx[:, None] == idx[None, :], 0, lev).astype(np.int32)
    tri = (idx[:, None] >= idx[None, :]).astype(np.float32)
    kern = functools.partial(_hgrn_kernel, n_heads=n_heads, kd=kd)
    return pl.pallas_call(
        kern,
        grid=(b, t // HG_BLOCK),
        in_specs=[pl.BlockSpec((1, HG_BLOCK, r.shape[2]), lambda bi, ti: (bi, ti, 0)),
                  pl.BlockSpec((1, n_heads * kd), lambda bi, ti: (0, 0)),
                  pl.BlockSpec((1, kd), lambda bi, ti: (0, 0)),
                  pl.BlockSpec((c, c), lambda bi, ti: (0, 0)),
                  pl.BlockSpec((c, c), lambda bi, ti: (0, 0))],
        out_specs=[pl.BlockSpec((1, HG_BLOCK, n_heads * kd), lambda bi, ti: (bi, ti, 0)),
                   pl.BlockSpec((1, n_heads, kd, kd), lambda bi, ti: (bi, 0, 0, 0))],
        out_shape=[jax.ShapeDtypeStruct((b, t, n_heads * kd), BF16),
                   jax.ShapeDtypeStruct((b, n_heads, kd, kd), F32)],
        scratch_shapes=[pltpu.VMEM((n_heads, kd, kd), F32)],
        compiler_params=pltpu.CompilerParams(dimension_semantics=("arbitrary", "arbitrary"),
                                             vmem_limit_bytes=VMEM_LIMIT),
        name="hgrn_prompt",
    )(r, lb, gn, jnp.asarray(tri, BF16), jnp.asarray(lev))


def _hgrn_step_kernel(r_ref, lb_ref, gn_ref, s_ref, o_ref, so_ref, *, n_heads, kd, n_tok, n_seq):
    rw = n_heads * kd
    rowid = lax.broadcasted_iota(jnp.int32, (SAMPLE_PAD, kd), 0)
    for g in range(n_seq):
        rows = slice(g * SAMPLE_PAD, (g + 1) * SAMPLE_PAD)
        for h in range(n_heads):
            cols = slice(h * kd, (h + 1) * kd)
            q = r_ref[rows, h * kd:(h + 1) * kd]
            rf = r_ref[rows, rw + h * kd:rw + (h + 1) * kd]
            v = r_ref[rows, 2 * rw + h * kd:2 * rw + (h + 1) * kd]
            rg = r_ref[rows, 3 * rw + h * kd:3 * rw + (h + 1) * kd]
            lb = lb_ref[:, cols]
            f = lb + (1.0 - lb) * jax.nn.sigmoid(rf)
            ft = f.T
            kt = (1.0 - f).T
            qt = q.T
            s = s_ref[g, h]
            o = jnp.zeros((SAMPLE_PAD, kd), F32)
            for tok in range(n_tok):
                s = ft[:, tok:tok + 1] * s + kt[:, tok:tok + 1] * v[tok:tok + 1, :]
                ot = jnp.sum(qt[:, tok:tok + 1] * s, axis=0, keepdims=True)
                o = jnp.where(rowid == tok, ot, o)
            so_ref[g, h] = s
            ms = jnp.mean(o * o, axis=-1, keepdims=True)
            on = o * lax.rsqrt(ms + EPS) * gn_ref[...] * (rg * jax.nn.sigmoid(rg))
            o_ref[rows, cols] = on.astype(o_ref.dtype)


def _hgrn_sample(r, lb, gn, state, n_heads, kd, n_tok, seq_per_step=2):
    n_seq = state.shape[0]
    kern = functools.partial(_hgrn_step_kernel, n_heads=n_heads, kd=kd, n_tok=n_tok, n_seq=seq_per_step)
    rows = seq_per_step * SAMPLE_PAD
    return pl.pallas_call(
        kern,
        grid=(n_seq // seq_per_step,),
        in_specs=[pl.BlockSpec((rows, r.shape[1]), lambda i: (i, 0)),
                  pl.BlockSpec((1, n_heads * kd), lambda i: (0, 0)),
                  pl.BlockSpec((1, kd), lambda i: (0, 0)),
                  pl.BlockSpec((seq_per_step, n_heads, kd, kd), lambda i: (i, 0, 0, 0))],
        out_specs=[pl.BlockSpec((rows, n_heads * kd), lambda i: (i, 0)),
                   pl.BlockSpec((seq_per_step, n_heads, kd, kd), lambda i: (i, 0, 0, 0))],
        out_shape=[jax.ShapeDtypeStruct((n_seq * SAMPLE_PAD, n_heads * kd), F32),
                   jax.ShapeDtypeStruct(state.shape, F32)],
        compiler_params=pltpu.CompilerParams(dimension_semantics=("arbitrary",),
                                             vmem_limit_bytes=VMEM_LIMIT),
        name="hgrn_sample",
    )(r, lb, gn, state)


def _ffn_kernel(*refs, sample, aw):
    if sample:
        (x_ref, oa_ref, or_ref, wo_ref, g2_ref, wg_ref, wu_ref, cw_ref, cb_ref, wd_ref, prev_ref,
         y_ref, a_ref, x1_sc, h2_sc, acc_sc) = refs
    else:
        (x_ref, oa_ref, or_ref, wo_ref, g2_ref, wg_ref, wu_ref, cw_ref, cb_ref, wd_ref,
         y_ref, a_ref, x1_sc, h2_sc, acc_sc, carry_sc) = refs

        @pl.when(pl.program_id(1) == 0)
        def _():
            carry_sc[...] = jnp.zeros(carry_sc.shape, F32)

    x = x_ref[...].astype(F32)
    x1 = x + _nn(oa_ref[...].astype(BF16), wo_ref[0:aw, :]) + _nn(or_ref[...].astype(BF16), wo_ref[aw:, :])
    x1_sc[...] = x1
    ms = jnp.mean(x1 * x1, axis=-1, keepdims=True)
    h2_sc[...] = (x1 * lax.rsqrt(ms + EPS) * g2_ref[...]).astype(BF16)
    acc_sc[...] = jnp.zeros(acc_sc.shape, F32)
    tm = x.shape[0]
    n_f = wg_ref.shape[0]

    def ftile(fi, carry):
        h2 = h2_sc[...]
        a = _nn(h2, wg_ref[fi])
        u = _nn(h2, wu_ref[fi])
        rowid = lax.broadcasted_iota(jnp.int32, a.shape, 0)
        am1 = pltpu.roll(a, 1, 0)
        am2 = pltpu.roll(a, 2, 0)
        if sample:
            rig = rowid & (SAMPLE_PAD - 1)
            prev = prev_ref[fi]
            am1 = jnp.where(rig == 0, pltpu.roll(prev, tm - 1, 0), am1)
            am2 = jnp.where(rig < 2, prev, am2)
            a_ref[fi] = a
        else:
            last = carry_sc[fi]
            am1 = jnp.where(rowid == 0, last[SUBLANES - 1:SUBLANES, :], am1)
            am2 = jnp.where(rowid == 0, last[SUBLANES - 2:SUBLANES - 1, :],
                            jnp.where(rowid == 1, last[SUBLANES - 1:SUBLANES, :], am2))
            tail = a[tm - SUBLANES:tm, :]
            carry_sc[fi] = tail
            a_ref[0, fi] = tail
        cw = cw_ref[fi]
        cv = cb_ref[fi] + cw[0:1, :] * am2 + cw[1:2, :] * am1 + cw[2:3, :] * a
        gate = (cv * jax.nn.sigmoid(cv) * u).astype(BF16)
        acc_sc[...] += _nn(gate, wd_ref[fi])
        return carry

    lax.fori_loop(0, n_f, ftile, 0)
    y_ref[...] = x1_sc[...] + acc_sc[...]


def _ffn(x2, oa, orec, wo, g2, wg3, wu3, cw3, cb3, wd3, prev3, tm, n_batch):
    n, d = x2.shape
    aw = oa.shape[1]
    n_f, _, tf = wg3.shape
    sample = prev3 is not None
    kern = functools.partial(_ffn_kernel, sample=sample, aw=aw)
    scratch = [pltpu.VMEM((tm, d), F32), pltpu.VMEM((tm, d), BF16), pltpu.VMEM((tm, d), F32)]
    if sample:
        grid = (n // tm,)
        row = lambda i: (i, 0)
        c2 = lambda i: (0, 0)
        c3 = lambda i: (0, 0, 0)
        extra_in = [pl.BlockSpec((n_f, tm, tf), lambda i: (0, i, 0))]
        extra_args = [prev3]
        a_shape = jax.ShapeDtypeStruct((n_f, n, tf), F32)
        a_spec = pl.BlockSpec((n_f, tm, tf), lambda i: (0, i, 0))
        sem = ("arbitrary",)
    else:
        per_b = n // n_batch // tm
        grid = (n_batch, per_b)
        row = lambda b, i: (b * per_b + i, 0)
        c2 = lambda b, i: (0, 0)
        c3 = lambda b, i: (0, 0, 0)
        extra_in = []
        extra_args = []
        a_shape = jax.ShapeDtypeStruct((n_batch, n_f, SUBLANES, tf), F32)
        a_spec = pl.BlockSpec((1, n_f, SUBLANES, tf), lambda b, i: (b, 0, 0, 0))
        scratch.append(pltpu.VMEM((n_f, SUBLANES, tf), F32))
        sem = ("arbitrary", "arbitrary")
    return pl.pallas_call(
        kern,
        grid=grid,
        in_specs=[pl.BlockSpec((tm, d), row),
                  pl.BlockSpec((tm, aw), row),
                  pl.BlockSpec((tm, aw), row),
                  _const_spec(wo.shape, c2),
                  pl.BlockSpec((1, d), c2),
                  _const_spec(wg3.shape, c3),
                  _const_spec(wu3.shape, c3),
                  pl.BlockSpec(cw3.shape, c3),
                  pl.BlockSpec(cb3.shape, c3),
                  _const_spec(wd3.shape, c3)] + extra_in,
        out_specs=[pl.BlockSpec((tm, d), row), a_spec],
        out_shape=[jax.ShapeDtypeStruct((n, d), F32), a_shape],
        scratch_shapes=scratch,
        compiler_params=pltpu.CompilerParams(dimension_semantics=sem, vmem_limit_bytes=VMEM_LIMIT),
        name="ffn_sample" if sample else "ffn_prompt",
    )(x2, oa, orec, wo, g2, wg3, wu3, cw3, cb3, wd3, *extra_args)


def kernel(x_prompt, x_sample, cache_k, cache_v, page_table, state_hgrn, state_conv, rel_bias, norm1_g, w_in,
           qk_norm_g, lambda_qk, subln_g, hgrn_lb, hgrn_onorm_g, w_out, norm2_g, w_gate, w_up, conv_w, conv_b,
           w_down):
    bp, tp, d = x_prompt.shape
    bs, ts, _ = x_sample.shape
    depth = w_in.shape[0]
    assert depth == 1 and CONV_W - 1 <= ts <= SAMPLE_PAD and conv_w.shape[1] == CONV_W
    n_heads, _, hd = cache_k.shape[3:]
    vd = cache_v.shape[4]
    aw = n_heads * vd
    r_heads, kd = state_hgrn.shape[2], state_hgrn.shape[3]
    rw = r_heads * kd
    dff = w_gate.shape[2]
    page_size = cache_k.shape[2]
    assert page_size >= MAX_DISTANCE and tp % TQ == 0 and tp % HG_BLOCK == 0 and dff % FF_TILE == 0
    n_f = dff // FF_TILE
    l = 0
    lam_init = 0.8 - 0.6 * math.exp(-0.3 * l)
    out_scale = 1.0 - lam_init

    biasp, biass, lam, lb = _tables(rel_bias, lambda_qk[l], hgrn_lb, n_heads, ts, page_size, lam_init)

    w_in_bf = w_in[l].astype(BF16)
    wo_bf = w_out[l].astype(BF16)
    wg3 = w_gate[l].astype(BF16).reshape(d, n_f, FF_TILE).transpose(1, 0, 2)
    wu3 = w_up[l].astype(BF16).reshape(d, n_f, FF_TILE).transpose(1, 0, 2)
    wd3 = w_down[l].astype(BF16).reshape(n_f, FF_TILE, d)
    cw3 = conv_w[l].reshape(CONV_W, n_f, FF_TILE).transpose(1, 0, 2)
    cb3 = conv_b[l].reshape(n_f, 1, FF_TILE)
    g1 = norm1_g[l].reshape(1, d)
    g2 = norm2_g[l].reshape(1, d)
    qg = jnp.tile(qk_norm_g[l, 0].reshape(1, 2 * hd), (1, n_heads))
    kg = jnp.tile(qk_norm_g[l, 1].reshape(1, 2 * hd), (1, n_heads))
    seg = np.arange(aw) // hd
    bd = jnp.asarray((seg[:, None] == seg[None, :]).astype(np.float32) / hd, BF16)
    sg = subln_g[l].reshape(1, vd)
    gn = hgrn_onorm_g[l].reshape(1, kd)
    q_scale = hd ** -0.5 * LOG2E

    xp2 = x_prompt.reshape(bp * tp, d)
    qb, kf, vf, kb, vb, rp = _proj(xp2, g1, w_in_bf, qg, kg, bd, aw, rw, q_scale, True, BF16, BF16, 512)
    o_att = _attn(lam, qb.reshape(bp, tp, aw), kb.reshape(bp, tp, aw), vb.reshape(bp, tp, aw), biasp, sg,
                  n_heads, hd, out_scale)
    o_rec, s_p = _hgrn_prompt(rp.reshape(bp, tp, 4 * rw), lb, gn, r_heads, kd)
    yp, a_tail = _ffn(xp2, o_att.reshape(bp * tp, aw), o_rec.reshape(bp * tp, rw), wo_bf, g2, wg3, wu3, cw3, cb3,
                      wd3, None, 512, bp)
    k_prompt = kf.reshape(1, bp, tp, n_heads, 2, hd)
    v_prompt = vf.reshape(1, bp, tp, n_heads, vd)
    conv_prompt = a_tail.transpose(0, 2, 1, 3).reshape(bp, SUBLANES, dff)[:, SUBLANES - (CONV_W - 1):][None]

    xs_pad = jnp.pad(x_sample, ((0, 0), (0, SAMPLE_PAD - ts), (0, 0))).reshape(bs * SAMPLE_PAD, d)
    qs, ksf, vsf, rs = _proj(xs_pad, g1, w_in_bf, qg, kg, bd, aw, rw, q_scale, False, F32, F32, 512)
    ck = jnp.transpose(cache_k[l], (0, 2, 3, 4, 1)).reshape(cache_k.shape[1], aw, page_size)
    cv = cache_v[l].reshape(cache_v.shape[1], page_size * n_heads, vd)
    o_att_s = _decode(page_table, lam, qs, ksf, vsf, biass, sg, ck, cv, n_heads, hd, out_scale)
    o_rec_s, s_s = _hgrn_sample(rs, lb, gn, state_hgrn[l], r_heads, kd, ts)
    prev = jnp.pad(state_conv[l], ((0, 0), (0, SAMPLE_PAD - (CONV_W - 1)), (0, 0)))
    prev3 = prev.reshape(bs * SAMPLE_PAD, n_f, FF_TILE).transpose(1, 0, 2)
    ys, a_s = _ffn(xs_pad, o_att_s, o_rec_s, wo_bf, g2, wg3, wu3, cw3, cb3, wd3, prev3, 256, 1)
    unpad = lambda z: z.reshape(bs, SAMPLE_PAD, -1)[:, :ts]
    y_sample = unpad(ys)
    k_sample = unpad(ksf).reshape(1, bs, ts, n_heads, 2, hd)
    v_sample = unpad(vsf).reshape(1, bs, ts, n_heads, vd)
    conv_sample = a_s.transpose(1, 0, 2).reshape(bs, SAMPLE_PAD, dff)[:, ts - (CONV_W - 1):ts][None]

    return (yp.reshape(bp, tp, d), y_sample, k_prompt, v_prompt, s_p[None], conv_prompt,
            k_sample, v_sample, s_s[None], conv_sample)
```

```python
import functools
import math

import numpy as np
import jax
import jax.numpy as jnp
from jax import lax
from jax.experimental import pallas as pl
from jax.experimental.pallas import tpu as pltpu

F32 = jnp.float32
BF16 = jnp.bfloat16

EPS = 1e-6
N_BUCKETS = 32
MAX_DISTANCE = 128
CONV_W = 3
NEG = -1e30
LOG2E = math.log2(math.e)
LANES = 128
SUBLANES = 8

TQ = 256
ATTN_GROUP = 2
HG_CHUNK = 128
HG_BLOCK = 512
SAMPLE_PAD = 8
VMEM_LIMIT = 56 * 1024 * 1024


def _const_spec(shape, index_map):
    return pl.BlockSpec(shape, index_map, pipeline_mode=pl.Buffered(1))


def _nt(a, b):
    return lax.dot_general(a, b, (((1,), (1,)), ((), ())), preferred_element_type=F32)


def _tn(a, b):
    return lax.dot_general(a, b, (((0,), (0,)), ((), ())), preferred_element_type=F32)


def _nn(a, b):
    return jnp.dot(a, b, preferred_element_type=F32)


def _rel_bucket_np(n):
    n = np.maximum(n, 0)
    max_exact = N_BUCKETS // 2
    nf = np.maximum(n, 1).astype(np.float32)
    large = max_exact + (np.log(nf / np.float32(max_exact)) / np.float32(math.log(MAX_DISTANCE / max_exact))
                         * np.float32(N_BUCKETS - max_exact)).astype(np.int32)
    large = np.minimum(large, N_BUCKETS - 1)
    return np.where(n < max_exact, n, large).astype(np.int32)


def _tables_kernel(relb_ref, lq_ref, hlb_ref, bkp_ref, bks_ref,
                   biasp_ref, biass_ref, lam_ref, lb_ref, *, n_heads, lam_init):
    for h in range(n_heads):
        far = relb_ref[N_BUCKETS - 1, h]
        for kind in range(2):
            bk = bkp_ref[kind]
            acc = jnp.where(bk < 0, NEG, 0.0).astype(F32)
            for b in range(N_BUCKETS - 1):
                acc = jnp.where(bk == b, (relb_ref[b, h] - far) * LOG2E, acc)
            biasp_ref[h, kind] = acc
            biasp_ref[h, 2] = jnp.zeros(acc.shape, F32)
            rows = bks_ref.shape[1] // n_heads
            bk = bks_ref[kind, h * rows:(h + 1) * rows, :]
            acc = jnp.where(bk < 0, NEG, 0.0).astype(F32)
            for b in range(N_BUCKETS - 1):
                acc = jnp.where(bk == b, (relb_ref[b, h] - far) * LOG2E, acc)
            biass_ref[kind, h * rows:(h + 1) * rows, :] = acc
    lq = lq_ref[...].astype(F32)
    s1 = jnp.sum(lq[0:1] * lq[1:2], axis=1, keepdims=True)
    s2 = jnp.sum(lq[2:3] * lq[3:4], axis=1, keepdims=True)
    lam = jnp.exp(s1) - jnp.exp(s2) + lam_init
    lam_ref[...] = jnp.broadcast_to(lam, lam_ref.shape)
    hl = hlb_ref[...].astype(F32)
    mx = jnp.max(hl, axis=0, keepdims=True)
    e = jnp.exp(hl - mx)
    lb_ref[...] = e[0:1] / jnp.sum(e, axis=0, keepdims=True)


def _tables(rel_bias, lambda_qk_l, hgrn_lb, n_heads, n_tok, page_size, lam_init):
    c = np.arange(TQ)[:, None]
    r = np.arange(TQ)[None, :]
    diag = np.where(c <= r, _rel_bucket_np(r - c), -1)
    prev = _rel_bucket_np(TQ + r - c)
    bkp = jnp.asarray(np.stack([diag, prev]).astype(np.int32))
    t = np.tile(np.arange(SAMPLE_PAD), 2 * n_heads)[:, None]
    cc = np.arange(page_size)[None, :]
    last_page = _rel_bucket_np(page_size + t - cc)
    new_page = np.where(cc <= np.minimum(t, n_tok - 1), _rel_bucket_np(t - cc), -1)
    bks = jnp.asarray(np.stack([last_page, new_page]).astype(np.int32))
    rows_s = 2 * n_heads * SAMPLE_PAD
    kern = functools.partial(_tables_kernel, n_heads=n_heads, lam_init=lam_init)
    return pl.pallas_call(
        kern,
        out_shape=(jax.ShapeDtypeStruct((n_heads, 3, TQ, TQ), F32),
                   jax.ShapeDtypeStruct((2, rows_s, page_size), F32),
                   jax.ShapeDtypeStruct((SUBLANES, LANES), F32),
                   jax.ShapeDtypeStruct((1, hgrn_lb.shape[1]), F32)),
        in_specs=[pl.BlockSpec(memory_space=pltpu.SMEM),
                  pl.BlockSpec(memory_space=pltpu.VMEM),
                  pl.BlockSpec(memory_space=pltpu.VMEM),
                  pl.BlockSpec(memory_space=pltpu.VMEM),
                  pl.BlockSpec(memory_space=pltpu.VMEM)],
        name="tables",
    )(rel_bias, lambda_qk_l, hgrn_lb, bkp, bks)


def _proj_kernel(x_ref, g1_ref, w_ref, qg_ref, kg_ref, bd_ref, *out_refs, aw, rw, vd, q_scale, emit_bf16):
    if emit_bf16:
        q_ref, kf_ref, vf_ref, kb_ref, vb_ref, r_ref = out_refs
    else:
        q_ref, kf_ref, vf_ref, r_ref = out_refs
    x = x_ref[...].astype(F32)
    ms = jnp.mean(x * x, axis=-1, keepdims=True)
    h = (x * lax.rsqrt(ms + EPS) * g1_ref[...]).astype(BF16)

    def seg_norm(z, g):
        msq = _nn((z * z).astype(BF16), bd_ref[...])
        return z * lax.rsqrt(msq + EPS) * g

    zq = _nn(h, w_ref[:, 0:aw])
    q_ref[...] = (seg_norm(zq, qg_ref[...]) * q_scale).astype(q_ref.dtype)
    zk = _nn(h, w_ref[:, aw:2 * aw])
    kn = seg_norm(zk, kg_ref[...])
    zv = _nn(h, w_ref[:, 2 * aw:3 * aw])
    if emit_bf16:
        n_heads = aw // vd
        kf_ref[0] = kn.T
        kb_ref[...] = kn.astype(BF16)
        vb_ref[0] = zv.T.astype(BF16)
        for hh in range(n_heads):
            vf_ref[pl.ds(hh, zv.shape[0], stride=n_heads), :] = zv[:, hh * vd:(hh + 1) * vd]
    else:
        kf_ref[...] = kn
        vf_ref[...] = zv
    for g in range(4):
        z = _nn(h, w_ref[:, 3 * aw + g * rw:3 * aw + (g + 1) * rw])
        r_ref[:, g * rw:(g + 1) * rw] = z.astype(r_ref.dtype)


def _proj(x2, g1, w_bf, qg, kg, bd, aw, rw, vd, q_scale, emit_bf16, q_dtype, r_dtype, tm, n_batch):
    n, d = x2.shape
    kern = functools.partial(_proj_kernel, aw=aw, rw=rw, vd=vd, q_scale=q_scale, emit_bf16=emit_bf16)
    row = lambda i: (i, 0)
    const = lambda i: (0, 0)
    if emit_bf16:
        t = n // n_batch
        per_b = t // tm
        n_heads = aw // vd
        xposed = lambda i: (i // per_b, 0, i % per_b)
        out_shape = [jax.ShapeDtypeStruct((n, aw), q_dtype),
                     jax.ShapeDtypeStruct((n_batch, aw, t), F32),
                     jax.ShapeDtypeStruct((n * n_heads, vd), F32),
                     jax.ShapeDtypeStruct((n, aw), BF16),
                     jax.ShapeDtypeStruct((n_batch, aw, t), BF16)]
        out_specs = [pl.BlockSpec((tm, aw), row),
                     pl.BlockSpec((1, aw, tm), xposed),
                     pl.BlockSpec((tm * n_heads, vd), row),
                     pl.BlockSpec((tm, aw), row),
                     pl.BlockSpec((1, aw, tm), xposed)]
    else:
        out_shape = [jax.ShapeDtypeStruct((n, aw), q_dtype),
                     jax.ShapeDtypeStruct((n, aw), F32),
                     jax.ShapeDtypeStruct((n, aw), F32)]
        out_specs = [pl.BlockSpec((tm, aw), row)] * 3
    out_shape.append(jax.ShapeDtypeStruct((n, 4 * rw), r_dtype))
    out_specs.append(pl.BlockSpec((tm, 4 * rw), row))
    return pl.pallas_call(
        kern,
        grid=(n // tm,),
        in_specs=[pl.BlockSpec((tm, d), row),
                  pl.BlockSpec((1, d), const),
                  _const_spec(w_bf.shape, const),
                  pl.BlockSpec((1, aw), const),
                  pl.BlockSpec((1, aw), const),
                  pl.BlockSpec((aw, aw), const)],
        out_specs=out_specs,
        out_shape=out_shape,
        compiler_params=pltpu.CompilerParams(dimension_semantics=("arbitrary",),
                                             vmem_limit_bytes=VMEM_LIMIT),
        name="proj_bf16" if emit_bf16 else "proj_f32",
    )(x2, g1, w_bf, qg, kg, bd)


def _attn_kernel(lam_ref, q_ref, k_ref, vt_ref, bias_ref, sg_ref, o_ref,
                 qst_sc, sa_sc, ma_sc, sb_sc, mb_sc, m_sc, l_sc, acc_sc, *, hd, out_scale):
    i = pl.program_id(2)
    tkl = ATTN_GROUP * TQ

    qt = q_ref[0].astype(F32).T
    row = lax.broadcasted_iota(jnp.int32, qt.shape, 0)
    qst_sc[...] = jnp.concatenate([jnp.where(row < hd, qt, 0.0), jnp.where(row >= hd, qt, 0.0)],
                                  axis=1).astype(BF16)
    m_sc[...] = jnp.full(m_sc.shape, NEG, F32)
    l_sc[...] = jnp.zeros(l_sc.shape, F32)
    acc_sc[...] = jnp.zeros(acc_sc.shape, F32)

    def scores(start, size):
        return _nn(k_ref[0, pl.ds(start, size), :], qst_sc[...])

    def softmax_pv(s, s_max, start):
        m_old = m_sc[...]
        m_new = jnp.maximum(m_old, s_max)
        alpha = jnp.exp2(m_old - m_new)
        p = jnp.exp2(s - m_new)
        l_sc[...] = alpha * l_sc[...] + jnp.sum(p, axis=0, keepdims=True)
        acc_sc[...] = alpha * acc_sc[...] + _nn(vt_ref[0, :, pl.ds(start, s.shape[0])], p.astype(BF16))
        m_sc[...] = m_new

    bufs = ((sa_sc, ma_sc), (sb_sc, mb_sc))

    def issue(buf, g):
        s = scores(pl.multiple_of(g * tkl, tkl), tkl)
        buf[0][...] = s
        buf[1][...] = jnp.max(s, axis=0, keepdims=True)

    def consume(buf, g):
        softmax_pv(buf[0][...], buf[1][...], pl.multiple_of(g * tkl, tkl))

    n_big = jnp.maximum(i - 1, 0) // ATTN_GROUP
    issue(bufs[0], 0)

    def pair(jj, carry):
        g = 2 * jj
        issue(bufs[1], g + 1)
        consume(bufs[0], g)
        issue(bufs[0], g + 2)
        consume(bufs[1], g + 1)
        return carry

    lax.fori_loop(0, n_big // 2, pair, 0)
    odd = (n_big & 1) == 1

    @pl.when(odd)
    def _():
        issue(bufs[1], n_big)
        consume(bufs[0], n_big - 1)

    base = n_big * ATTN_GROUP
    n_ahead = jnp.minimum(i + 1 - base, ATTN_GROUP)

    def small(c, s):
        kind = jnp.where(c == i, 0, jnp.where(c == i - 1, 1, 2))
        bias = bias_ref[0, kind]
        s = jnp.concatenate([s[:, 0:TQ] + bias, s[:, TQ:2 * TQ] + bias], axis=1)
        softmax_pv(s, jnp.max(s, axis=0, keepdims=True), pl.multiple_of(c * TQ, TQ))

    def tail(buf):
        def ahead(u, carry):
            small(base + u, buf[0][pl.ds(pl.multiple_of(u * TQ, TQ), TQ), :])
            return carry
        lax.fori_loop(0, n_ahead, ahead, 0)

    @pl.when(odd)
    def _():
        tail(bufs[1])

    @pl.when(jnp.logical_not(odd))
    def _():
        tail(bufs[0])

    def inline(c, carry):
        small(c, scores(pl.multiple_of(c * TQ, TQ), TQ))
        return carry

    lax.fori_loop(base + n_ahead, i + 1, inline, 0)

    a = acc_sc[...] * (1.0 / l_sc[...])
    ot = a[:, 0:TQ] - lam_ref[0, 0] * a[:, TQ:2 * TQ]
    ms = jnp.mean(ot * ot, axis=0, keepdims=True)
    o = (ot * lax.rsqrt(ms + EPS)).T
    o_ref[0] = (o * (sg_ref[...] * out_scale)).astype(o_ref.dtype)


def _attn(lam, q, k, vt, bias, sg, n_heads, hd, out_scale):
    b, t, aw = q.shape
    vd = aw // n_heads
    assert t % (ATTN_GROUP * TQ) == 0
    kern = functools.partial(_attn_kernel, hd=hd, out_scale=out_scale)
    return pl.pallas_call(
        kern,
        grid=(b, n_heads, t // TQ),
        in_specs=[pl.BlockSpec(memory_space=pltpu.SMEM),
                  pl.BlockSpec((1, TQ, vd), lambda bi, h, i: (bi, i, h)),
                  pl.BlockSpec((1, t, vd), lambda bi, h, i: (bi, 0, h)),
                  pl.BlockSpec((1, vd, t), lambda bi, h, i: (bi, h, 0)),
                  pl.BlockSpec((1, 3, TQ, TQ), lambda bi, h, i: (h, 0, 0, 0)),
                  pl.BlockSpec((1, vd), lambda bi, h, i: (0, 0))],
        out_specs=pl.BlockSpec((1, TQ, vd), lambda bi, h, i: (bi, i, h)),
        out_shape=jax.ShapeDtypeStruct((b, t, aw), BF16),
        scratch_shapes=[pltpu.VMEM((vd, 2 * TQ), BF16),
                        pltpu.VMEM((ATTN_GROUP * TQ, 2 * TQ), F32),
                        pltpu.VMEM((1, 2 * TQ), F32),
                        pltpu.VMEM((ATTN_GROUP * TQ, 2 * TQ), F32),
                        pltpu.VMEM((1, 2 * TQ), F32),
                        pltpu.VMEM((1, 2 * TQ), F32),
                        pltpu.VMEM((1, 2 * TQ), F32),
                        pltpu.VMEM((vd, 2 * TQ), F32)],
        compiler_params=pltpu.CompilerParams(dimension_semantics=("arbitrary",) * 3,
                                             vmem_limit_bytes=VMEM_LIMIT),
        name="attn_prompt",
    )(lam, q, k, vt, bias, sg)


def _decode_kernel(pt_ref, lam_ref, q_ref, kn_ref, vn_ref, bias_ref, sg_ref, *refs,
                   n_pages, n_heads, hd, out_scale):
    k_refs = refs[:n_pages]
    v_refs = refs[n_pages:2 * n_pages]
    o_ref = refs[2 * n_pages]
    del pt_ref
    q8 = q_ref[...].astype(F32)
    aw = q8.shape[1]
    vd = aw // n_heads
    ps = k_refs[0].shape[2]
    lane = lax.broadcasted_iota(jnp.int32, q8.shape, 1)
    pieces = [jnp.where((lane >= hm * hd) & (lane < (hm + 1) * hd), q8, 0.0) for hm in range(2 * n_heads)]
    qbd = jnp.concatenate(pieces, axis=0).astype(BF16)
    s_list = []
    for p in range(n_pages):
        s = _nn(qbd, k_refs[p][0].astype(BF16))
        if p == n_pages - 1:
            s = s + bias_ref[0]
        s_list.append(s)
    pad = jnp.zeros((ps - SAMPLE_PAD, aw), F32)
    k_new = jnp.concatenate([kn_ref[...].astype(F32), pad], axis=0).astype(BF16)
    v_new = jnp.concatenate([vn_ref[...].astype(F32), pad], axis=0).astype(BF16)
    s_list.append(_nt(qbd, k_new) + bias_ref[1])
    s_all = jnp.concatenate(s_list, axis=1)
    m = jnp.max(s_all, axis=1, keepdims=True)
    e = jnp.exp2(s_all - m)
    inv = 1.0 / jnp.sum(e, axis=1, keepdims=True)
    eb = e.astype(BF16)
    lam = lam_ref[0, 0]
    rows = 2 * SAMPLE_PAD
    outs = []
    for h in range(n_heads):
        eh = eb[h * rows:(h + 1) * rows]
        acc = _nn(eh[:, n_pages * ps:(n_pages + 1) * ps], v_new[:, h * vd:(h + 1) * vd])
        for p in range(n_pages):
            vh = v_refs[p][0, pl.ds(h, ps, stride=n_heads), :].astype(BF16)
            acc = acc + _nn(eh[:, p * ps:(p + 1) * ps], vh)
        acc = acc * inv[h * rows:(h + 1) * rows]
        o = acc[0:SAMPLE_PAD] - lam * acc[SAMPLE_PAD:rows]
        ms = jnp.mean(o * o, axis=-1, keepdims=True)
        outs.append(o * lax.rsqrt(ms + EPS) * (sg_ref[...] * out_scale))
    o_ref[...] = jnp.concatenate(outs, axis=1).astype(o_ref.dtype)


def _decode(page_table, lam, q, kn, vn, bias, sg, ck, cv, n_heads, hd, out_scale):
    n_seq, n_pages = page_table.shape
    _, aw, ps = ck.shape
    vd = aw // n_heads
    kern = functools.partial(_decode_kernel, n_pages=n_pages, n_heads=n_heads, hd=hd, out_scale=out_scale)
    row = lambda s, pt: (s, 0)
    page_idx = [functools.partial(lambda s, pt, p: (pt[s, p], 0, 0), p=p) for p in range(n_pages)]
    k_specs = [pl.BlockSpec((1, aw, ps), im) for im in page_idx]
    v_specs = [pl.BlockSpec((1,) + cv.shape[1:], im) for im in page_idx]
    grid_spec = pltpu.PrefetchScalarGridSpec(
        num_scalar_prefetch=1,
        grid=(n_seq,),
        in_specs=[pl.BlockSpec(memory_space=pltpu.SMEM),
                  pl.BlockSpec((SAMPLE_PAD, aw), row),
                  pl.BlockSpec((SAMPLE_PAD, aw), row),
                  pl.BlockSpec((SAMPLE_PAD, aw), row),
                  pl.BlockSpec(bias.shape, lambda s, pt: (0, 0, 0)),
                  pl.BlockSpec((1, vd), lambda s, pt: (0, 0))] + k_specs + v_specs,
        out_specs=pl.BlockSpec((SAMPLE_PAD, aw), row),
    )
    return pl.pallas_call(
        kern,
        grid_spec=grid_spec,
        out_shape=jax.ShapeDtypeStruct((n_seq * SAMPLE_PAD, aw), F32),
        compiler_params=pltpu.CompilerParams(dimension_semantics=("arbitrary",),
                                             vmem_limit_bytes=VMEM_LIMIT),
        name="attn_decode",
    )(page_table, lam, q, kn, vn, bias, sg, *([ck] * n_pages), *([cv] * n_pages))


def _mid_reference(b, level):
    c, w = b.shape
    p = 1 << level
    half = p // 2
    if p >= SUBLANES:
        bp = b.reshape(c // p, p, w)
        return jnp.broadcast_to(bp[:, half - 1:half, :], (c // p, p, w)).reshape(c, w)
    b8 = b.reshape(c // SUBLANES, SUBLANES, w)
    rig = lax.broadcasted_iota(jnp.int32, b8.shape, 1)
    out = None
    for start in range(SUBLANES - p, -1, -p):
        row = jnp.broadcast_to(b8[:, start + half - 1:start + half, :], b8.shape)
        out = row if out is None else jnp.where(rig < start + p, row, out)
    return out.reshape(c, w)


def _hgrn_kernel(r_ref, lb_ref, gn_ref, tri_ref, lev_ref, o_ref, sfin_ref, st_sc, *, n_heads, kd):
    t = pl.program_id(1)
    c = HG_CHUNK
    rw = n_heads * kd
    n_levels = c.bit_length() - 1

    @pl.when(t == 0)
    def _():
        st_sc[...] = jnp.zeros(st_sc.shape, F32)

    def chunk(ci, carry):
        r0 = pl.multiple_of(ci * c, c)
        lev = lev_ref[...]
        tri = tri_ref[...]
        for h in range(n_heads):
            cols = slice(h * kd, (h + 1) * kd)
            q = r_ref[0, pl.ds(r0, c), cols].astype(F32)
            rf = r_ref[0, pl.ds(r0, c), rw + h * kd:rw + (h + 1) * kd].astype(F32)
            v = r_ref[0, pl.ds(r0, c), 2 * rw + h * kd:2 * rw + (h + 1) * kd].astype(BF16)
            rg = r_ref[0, pl.ds(r0, c), 3 * rw + h * kd:3 * rw + (h + 1) * kd].astype(F32)
            lb = lb_ref[:, cols]
            f = lb + (1.0 - lb) * jax.nn.sigmoid(rf)
            logf = jnp.log(f)
            kk = 1.0 - f
            hi = logf.astype(BF16)
            lo = (logf - hi.astype(F32)).astype(BF16)
            b = _nn(tri, hi) + _nn(tri, lo)
            a = jnp.where(lev == 0, _nt(q.astype(BF16), kk.astype(BF16)), 0.0)
            for level in range(1, n_levels + 1):
                e = jnp.exp(-jnp.abs(b - _mid_reference(b, level)))
                pm = _nt((q * e).astype(BF16), (kk * e).astype(BF16))
                a = a + jnp.where(lev == level, pm, 0.0)
            st = st_sc[h]
            o = _nn(a.astype(BF16), v) + _nt((q * jnp.exp(b)).astype(BF16), st.astype(BF16))
            b_last = b[c - 1:c, :]
            k_hat = (kk * jnp.exp(b_last - b)).astype(BF16)
            st_sc[h] = jnp.exp(b_last) * st + _tn(v, k_hat)
            ms = jnp.mean(o * o, axis=-1, keepdims=True)
            on = o * lax.rsqrt(ms + EPS) * gn_ref[...] * (rg * jax.nn.sigmoid(rg))
            o_ref[0, pl.ds(r0, c), cols] = on.astype(o_ref.dtype)
        return carry

    lax.fori_loop(0, HG_BLOCK // c, chunk, 0)

    @pl.when(t == pl.num_programs(1) - 1)
    def _():
        for h in range(n_heads):
            sfin_ref[0, h] = st_sc[h].T


def _hgrn_prompt(r, lb, gn, n_heads, kd):
    b, t, _ = r.shape
    c = HG_CHUNK
    idx = np.arange(c)
    x = idx[:, None] ^ idx[None, :]
    lev = np.where(idx[:, None] > idx[None, :], np.floor(np.log2(np.maximum(x, 1))).astype(np.int32) + 1, -1)
    lev = np.where(idx[:, None] == idx[None, :], 0, lev).astype(np.int32)
    tri = (idx[:, None] >= idx[None, :]).astype(np.float32)
    kern = functools.partial(_hgrn_kernel, n_heads=n_heads, kd=kd)
    return pl.pallas_call(
        kern,
        grid=(b, t // HG_BLOCK),
        in_specs=[pl.BlockSpec((1, HG_BLOCK, r.shape[2]), lambda bi, ti: (bi, ti, 0)),
                  pl.BlockSpec((1, n_heads * kd), lambda bi, ti: (0, 0)),
                  pl.BlockSpec((1, kd), lambda bi, ti: (0, 0)),
                  pl.BlockSpec((c, c), lambda bi, ti: (0, 0)),
                  pl.BlockSpec((c, c), lambda bi, ti: (0, 0))],
        out_specs=[pl.BlockSpec((1, HG_BLOCK, n_heads * kd), lambda bi, ti: (bi, ti, 0)),
                   pl.BlockSpec((1, n_heads, kd, kd), lambda bi, ti: (bi, 0, 0, 0))],
        out_shape=[jax.ShapeDtypeStruct((b, t, n_heads * kd), BF16),
                   jax.ShapeDtypeStruct((b, n_heads, kd, kd), F32)],
        scratch_shapes=[pltpu.VMEM((n_heads, kd, kd), F32)],
        compiler_params=pltpu.CompilerParams(dimension_semantics=("arbitrary", "arbitrary"),
                                             vmem_limit_bytes=VMEM_LIMIT),
        name="hgrn_prompt",
    )(r, lb, gn, jnp.asarray(tri, BF16), jnp.asarray(lev))


def _hgrn_step_kernel(r_ref, lb_ref, gn_ref, s_ref, o_ref, so_ref, *, n_heads, kd, n_tok, n_seq):
    rw = n_heads * kd
    rowid = lax.broadcasted_iota(jnp.int32, (SAMPLE_PAD, kd), 0)
    for g in range(n_seq):
        rows = slice(g * SAMPLE_PAD, (g + 1) * SAMPLE_PAD)
        for h in range(n_heads):
            cols = slice(h * kd, (h + 1) * kd)
            q = r_ref[rows, h * kd:(h + 1) * kd]
            rf = r_ref[rows, rw + h * kd:rw + (h + 1) * kd]
            v = r_ref[rows, 2 * rw + h * kd:2 * rw + (h + 1) * kd]
            rg = r_ref[rows, 3 * rw + h * kd:3 * rw + (h + 1) * kd]
            lb = lb_ref[:, cols]
            f = lb + (1.0 - lb) * jax.nn.sigmoid(rf)
            ft = f.T
            kt = (1.0 - f).T
            qt = q.T
            s = s_ref[g, h]
            o = jnp.zeros((SAMPLE_PAD, kd), F32)
            for tok in range(n_tok):
                s = ft[:, tok:tok + 1] * s + kt[:, tok:tok + 1] * v[tok:tok + 1, :]
                ot = jnp.sum(qt[:, tok:tok + 1] * s, axis=0, keepdims=True)
                o = jnp.where(rowid == tok, ot, o)
            so_ref[g, h] = s
            ms = jnp.mean(o * o, axis=-1, keepdims=True)
            on = o * lax.rsqrt(ms + EPS) * gn_ref[...] * (rg * jax.nn.sigmoid(rg))
            o_ref[rows, cols] = on.astype(o_ref.dtype)


def _hgrn_sample(r, lb, gn, state, n_heads, kd, n_tok, seq_per_step=2):
    n_seq = state.shape[0]
    kern = functools.partial(_hgrn_step_kernel, n_heads=n_heads, kd=kd, n_tok=n_tok, n_seq=seq_per_step)
    rows = seq_per_step * SAMPLE_PAD
    return pl.pallas_call(
        kern,
        grid=(n_seq // seq_per_step,),
        in_specs=[pl.BlockSpec((rows, r.shape[1]), lambda i: (i, 0)),
                  pl.BlockSpec((1, n_heads * kd), lambda i: (0, 0)),
                  pl.BlockSpec((1, kd), lambda i: (0, 0)),
                  pl.BlockSpec((seq_per_step, n_heads, kd, kd), lambda i: (i, 0, 0, 0))],
        out_specs=[pl.BlockSpec((rows, n_heads * kd), lambda i: (i, 0)),
                   pl.BlockSpec((seq_per_step, n_heads, kd, kd), lambda i: (i, 0, 0, 0))],
        out_shape=[jax.ShapeDtypeStruct((n_seq * SAMPLE_PAD, n_heads * kd), F32),
                   jax.ShapeDtypeStruct(state.shape, F32)],
        compiler_params=pltpu.CompilerParams(dimension_semantics=("arbitrary",),
                                             vmem_limit_bytes=VMEM_LIMIT),
        name="hgrn_sample",
    )(r, lb, gn, state)


def _ffn_kernel(*refs, sample, aw):
    if sample:
        (x_ref, oa_ref, or_ref, wo_ref, g2_ref, wg_ref, wu_ref, cw_ref, cb_ref, wd_ref, prev_ref,
         y_ref, a_ref) = refs
    else:
        (x_ref, oa_ref, or_ref, wo_ref, g2_ref, wg_ref, wu_ref, cw_ref, cb_ref, wd_ref,
         y_ref, a_ref, carry_sc) = refs

        @pl.when(pl.program_id(1) == 0)
        def _():
            carry_sc[...] = jnp.zeros(carry_sc.shape, F32)

    x = x_ref[...].astype(F32)
    x1 = x + _nn(oa_ref[...].astype(BF16), wo_ref[0:aw, :]) + _nn(or_ref[...].astype(BF16), wo_ref[aw:, :])
    ms = jnp.mean(x1 * x1, axis=-1, keepdims=True)
    h2 = (x1 * lax.rsqrt(ms + EPS) * g2_ref[...]).astype(BF16)
    tm = x.shape[0]
    a = _nn(h2, wg_ref[...])
    u = _nn(h2, wu_ref[...])
    rowid = lax.broadcasted_iota(jnp.int32, a.shape, 0)
    am1 = pltpu.roll(a, 1, 0)
    am2 = pltpu.roll(a, 2, 0)
    if sample:
        rig = rowid & (SAMPLE_PAD - 1)
        prev = prev_ref[...]
        am1 = jnp.where(rig == 0, pltpu.roll(prev, tm - 1, 0), am1)
        am2 = jnp.where(rig < 2, prev, am2)
        a_ref[...] = a
    else:
        last = carry_sc[...]
        am1 = jnp.where(rowid == 0, last[SUBLANES - 1:SUBLANES, :], am1)
        am2 = jnp.where(rowid == 0, last[SUBLANES - 2:SUBLANES - 1, :],
                        jnp.where(rowid == 1, last[SUBLANES - 1:SUBLANES, :], am2))
        tail = a[tm - SUBLANES:tm, :]
        carry_sc[...] = tail
        a_ref[0] = tail
    cw = cw_ref[...]
    cv = cb_ref[...] + cw[0:1, :] * am2 + cw[1:2, :] * am1 + cw[2:3, :] * a
    gate = (cv * jax.nn.sigmoid(cv) * u).astype(BF16)
    y_ref[...] = x1 + _nn(gate, wd_ref[...])


def _ffn(x2, oa, orec, wo, g2, wg, wu, cw, cb, wd, prev, tm, n_batch):
    n, d = x2.shape
    aw = oa.shape[1]
    dff = wg.shape[1]
    sample = prev is not None
    kern = functools.partial(_ffn_kernel, sample=sample, aw=aw)
    scratch = []
    if sample:
        grid = (n // tm,)
        row = lambda i: (i, 0)
        c2 = lambda i: (0, 0)
        extra_in = [pl.BlockSpec((tm, dff), row)]
        extra_args = [prev]
        a_shape = jax.ShapeDtypeStruct((n, dff), F32)
        a_spec = pl.BlockSpec((tm, dff), row)
        sem = ("arbitrary",)
    else:
        per_b = n // n_batch // tm
        grid = (n_batch, per_b)
        row = lambda b, i: (b * per_b + i, 0)
        c2 = lambda b, i: (0, 0)
        extra_in = []
        extra_args = []
        a_shape = jax.ShapeDtypeStruct((n_batch, SUBLANES, dff), F32)
        a_spec = pl.BlockSpec((1, SUBLANES, dff), lambda b, i: (b, 0, 0))
        scratch.append(pltpu.VMEM((SUBLANES, dff), F32))
        sem = ("arbitrary", "arbitrary")
    return pl.pallas_call(
        kern,
        grid=grid,
        in_specs=[pl.BlockSpec((tm, d), row),
                  pl.BlockSpec((tm, aw), row),
                  pl.BlockSpec((tm, aw), row),
                  _const_spec(wo.shape, c2),
                  pl.BlockSpec((1, d), c2),
                  _const_spec(wg.shape, c2),
                  _const_spec(wu.shape, c2),
                  pl.BlockSpec(cw.shape, c2),
                  pl.BlockSpec(cb.shape, c2),
                  _const_spec(wd.shape, c2)] + extra_in,
        out_specs=[pl.BlockSpec((tm, d), row), a_spec],
        out_shape=[jax.ShapeDtypeStruct((n, d), F32), a_shape],
        scratch_shapes=scratch,
        compiler_params=pltpu.CompilerParams(dimension_semantics=sem, vmem_limit_bytes=VMEM_LIMIT),
        name="ffn_sample" if sample else "ffn_prompt",
    )(x2, oa, orec, wo, g2, wg, wu, cw, cb, wd, *extra_args)


def kernel(x_prompt, x_sample, cache_k, cache_v, page_table, state_hgrn, state_conv, rel_bias, norm1_g, w_in,
           qk_norm_g, lambda_qk, subln_g, hgrn_lb, hgrn_onorm_g, w_out, norm2_g, w_gate, w_up, conv_w, conv_b,
           w_down):
    bp, tp, d = x_prompt.shape
    bs, ts, _ = x_sample.shape
    depth = w_in.shape[0]
    assert depth == 1 and CONV_W - 1 <= ts <= SAMPLE_PAD and conv_w.shape[1] == CONV_W
    n_heads, _, hd = cache_k.shape[3:]
    vd = cache_v.shape[4]
    aw = n_heads * vd
    r_heads, kd = state_hgrn.shape[2], state_hgrn.shape[3]
    rw = r_heads * kd
    dff = w_gate.shape[2]
    page_size = cache_k.shape[2]
    assert page_size >= MAX_DISTANCE and tp % TQ == 0 and tp % HG_BLOCK == 0
    l = 0
    lam_init = 0.8 - 0.6 * math.exp(-0.3 * l)
    out_scale = 1.0 - lam_init

    biasp, biass, lam, lb = _tables(rel_bias, lambda_qk[l], hgrn_lb, n_heads, ts, page_size, lam_init)

    w_in_bf = w_in[l].astype(BF16)
    wo_bf = w_out[l].astype(BF16)
    wg_bf = w_gate[l].astype(BF16)
    wu_bf = w_up[l].astype(BF16)
    wd_bf = w_down[l].astype(BF16)
    cw = conv_w[l]
    cb = conv_b[l].reshape(1, dff)
    g1 = norm1_g[l].reshape(1, d)
    g2 = norm2_g[l].reshape(1, d)
    qg = jnp.tile(qk_norm_g[l, 0].reshape(1, 2 * hd), (1, n_heads))
    kg = jnp.tile(qk_norm_g[l, 1].reshape(1, 2 * hd), (1, n_heads))
    seg = np.arange(aw) // hd
    bd = jnp.asarray((seg[:, None] == seg[None, :]).astype(np.float32) / hd, BF16)
    sg = subln_g[l].reshape(1, vd)
    gn = hgrn_onorm_g[l].reshape(1, kd)
    q_scale = hd ** -0.5 * LOG2E

    xp2 = x_prompt.reshape(bp * tp, d)
    qb, kft, vf, kb, vbt, rp = _proj(xp2, g1, w_in_bf, qg, kg, bd, aw, rw, vd, q_scale, True, BF16, BF16, 512, bp)
    o_att = _attn(lam, qb.reshape(bp, tp, aw), kb.reshape(bp, tp, aw), vbt, biasp, sg, n_heads, hd, out_scale)
    o_rec, s_p = _hgrn_prompt(rp.reshape(bp, tp, 4 * rw), lb, gn, r_heads, kd)
    yp, a_tail = _ffn(xp2, o_att.reshape(bp * tp, aw), o_rec.reshape(bp * tp, rw), wo_bf, g2, wg_bf, wu_bf, cw, cb,
                      wd_bf, None, 256, bp)
    k_prompt = kft.reshape(bp, n_heads, 2, hd, tp).transpose(0, 4, 1, 2, 3)[None]
    v_prompt = vf.reshape(1, bp, tp, n_heads, vd)
    conv_prompt = a_tail[:, SUBLANES - (CONV_W - 1):][None]

    xs_pad = jnp.pad(x_sample, ((0, 0), (0, SAMPLE_PAD - ts), (0, 0))).reshape(bs * SAMPLE_PAD, d)
    qs, ksf, vsf, rs = _proj(xs_pad, g1, w_in_bf, qg, kg, bd, aw, rw, vd, q_scale, False, F32, F32, 512, bs)
    ck = jnp.transpose(cache_k[l], (0, 2, 3, 4, 1)).reshape(cache_k.shape[1], aw, page_size)
    cv = cache_v[l].reshape(cache_v.shape[1], page_size * n_heads, vd)
    o_att_s = _decode(page_table, lam, qs, ksf, vsf, biass, sg, ck, cv, n_heads, hd, out_scale)
    o_rec_s, s_s = _hgrn_sample(rs, lb, gn, state_hgrn[l], r_heads, kd, ts)
    prev = jnp.pad(state_conv[l], ((0, 0), (0, SAMPLE_PAD - (CONV_W - 1)), (0, 0))).reshape(bs * SAMPLE_PAD, dff)
    ys, a_s = _ffn(xs_pad, o_att_s, o_rec_s, wo_bf, g2, wg_bf, wu_bf, cw, cb, wd_bf, prev, 256, 1)
    unpad = lambda z: z.reshape(bs, SAMPLE_PAD, -1)[:, :ts]
    y_sample = unpad(ys)
    k_sample = unpad(ksf).reshape(1, bs, ts, n_heads, 2, hd)
    v_sample = unpad(vsf).reshape(1, bs, ts, n_heads, vd)
    conv_sample = a_s.reshape(bs, SAMPLE_PAD, dff)[:, ts - (CONV_W - 1):ts][None]

    return (yp.reshape(bp, tp, d), y_sample, k_prompt, v_prompt, s_p[None], conv_prompt,
            k_sample, v_sample, s_s[None], conv_sample)
```

```python
import functools
import math

import numpy as np
import jax
import jax.numpy as jnp
from jax import lax
from jax.experimental import pallas as pl
from jax.experimental.pallas import tpu as pltpu

F32 = jnp.float32
BF16 = jnp.bfloat16

EPS = 1e-6
N_BUCKETS = 32
MAX_DISTANCE = 128
CONV_W = 3
NEG = -1e30
LOG2E = math.log2(math.e)
LANES = 128
SUBLANES = 8

TQ = 256
ATTN_GROUP = 2
HG_CHUNK = 128
HG_BLOCK = 512
SAMPLE_PAD = 8
VMEM_LIMIT = 56 * 1024 * 1024


def _const_spec(shape, index_map):
    return pl.BlockSpec(shape, index_map, pipeline_mode=pl.Buffered(1))


def _nt(a, b):
    return lax.dot_general(a, b, (((1,), (1,)), ((), ())), preferred_element_type=F32)


def _tn(a, b):
    return lax.dot_general(a, b, (((0,), (0,)), ((), ())), preferred_element_type=F32)


def _nn(a, b):
    return jnp.dot(a, b, preferred_element_type=F32)


def _rel_bucket_np(n):
    n = np.maximum(n, 0)
    max_exact = N_BUCKETS // 2
    nf = np.maximum(n, 1).astype(np.float32)
    large = max_exact + (np.log(nf / np.float32(max_exact)) / np.float32(math.log(MAX_DISTANCE / max_exact))
                         * np.float32(N_BUCKETS - max_exact)).astype(np.int32)
    large = np.minimum(large, N_BUCKETS - 1)
    return np.where(n < max_exact, n, large).astype(np.int32)


def _tables_kernel(relb_ref, lq_ref, hlb_ref, bkp_ref, bks_ref,
                   biasp_ref, biass_ref, lam_ref, lb_ref, *, n_heads, lam_init):
    for h in range(n_heads):
        far = relb_ref[N_BUCKETS - 1, h]
        for kind in range(2):
            bk = bkp_ref[kind]
            acc = jnp.where(bk < 0, NEG, 0.0).astype(F32)
            for b in range(N_BUCKETS - 1):
                acc = jnp.where(bk == b, (relb_ref[b, h] - far) * LOG2E, acc)
            biasp_ref[h, kind] = acc
            biasp_ref[h, 2] = jnp.zeros(acc.shape, F32)
            rows = bks_ref.shape[1] // n_heads
            bk = bks_ref[kind, h * rows:(h + 1) * rows, :]
            acc = jnp.where(bk < 0, NEG, 0.0).astype(F32)
            for b in range(N_BUCKETS - 1):
                acc = jnp.where(bk == b, (relb_ref[b, h] - far) * LOG2E, acc)
            biass_ref[kind, h * rows:(h + 1) * rows, :] = acc
    lq = lq_ref[...].astype(F32)
    s1 = jnp.sum(lq[0:1] * lq[1:2], axis=1, keepdims=True)
    s2 = jnp.sum(lq[2:3] * lq[3:4], axis=1, keepdims=True)
    lam = jnp.exp(s1) - jnp.exp(s2) + lam_init
    lam_ref[...] = jnp.broadcast_to(lam, lam_ref.shape)
    hl = hlb_ref[...].astype(F32)
    mx = jnp.max(hl, axis=0, keepdims=True)
    e = jnp.exp(hl - mx)
    lb_ref[...] = e[0:1] / jnp.sum(e, axis=0, keepdims=True)


def _tables(rel_bias, lambda_qk_l, hgrn_lb, n_heads, n_tok, page_size, lam_init):
    c = np.arange(TQ)[:, None]
    r = np.arange(TQ)[None, :]
    diag = np.where(c <= r, _rel_bucket_np(r - c), -1)
    prev = _rel_bucket_np(TQ + r - c)
    bkp = jnp.asarray(np.stack([diag, prev]).astype(np.int32))
    t = np.tile(np.arange(SAMPLE_PAD), 2 * n_heads)[:, None]
    cc = np.arange(page_size)[None, :]
    last_page = _rel_bucket_np(page_size + t - cc)
    new_page = np.where(cc <= np.minimum(t, n_tok - 1), _rel_bucket_np(t - cc), -1)
    bks = jnp.asarray(np.stack([last_page, new_page]).astype(np.int32))
    rows_s = 2 * n_heads * SAMPLE_PAD
    kern = functools.partial(_tables_kernel, n_heads=n_heads, lam_init=lam_init)
    return pl.pallas_call(
        kern,
        out_shape=(jax.ShapeDtypeStruct((n_heads, 3, TQ, TQ), F32),
                   jax.ShapeDtypeStruct((2, rows_s, page_size), F32),
                   jax.ShapeDtypeStruct((SUBLANES, LANES), F32),
                   jax.ShapeDtypeStruct((1, hgrn_lb.shape[1]), F32)),
        in_specs=[pl.BlockSpec(memory_space=pltpu.SMEM),
                  pl.BlockSpec(memory_space=pltpu.VMEM),
                  pl.BlockSpec(memory_space=pltpu.VMEM),
                  pl.BlockSpec(memory_space=pltpu.VMEM),
                  pl.BlockSpec(memory_space=pltpu.VMEM)],
        name="tables",
    )(rel_bias, lambda_qk_l, hgrn_lb, bkp, bks)


def _proj_kernel(x_ref, g1_ref, w_ref, qg_ref, kg_ref, bd_ref, *out_refs, aw, rw, vd, q_scale, emit_bf16):
    if emit_bf16:
        q_ref, kf_ref, vf_ref, kb_ref, vb_ref, r_ref = out_refs
    else:
        q_ref, kf_ref, vf_ref, r_ref = out_refs
    x = x_ref[...].astype(F32)
    ms = jnp.mean(x * x, axis=-1, keepdims=True)
    h = (x * lax.rsqrt(ms + EPS) * g1_ref[...]).astype(BF16)

    def seg_norm(z, g):
        msq = _nn((z * z).astype(BF16), bd_ref[...])
        return z * lax.rsqrt(msq + EPS) * g

    zq = _nn(h, w_ref[:, 0:aw])
    q_ref[...] = (seg_norm(zq, qg_ref[...]) * q_scale).astype(q_ref.dtype)
    zk = _nn(h, w_ref[:, aw:2 * aw])
    kn = seg_norm(zk, kg_ref[...])
    zv = _nn(h, w_ref[:, 2 * aw:3 * aw])
    if emit_bf16:
        n_heads = aw // vd
        kf_ref[0] = kn.T
        kb_ref[...] = kn.astype(BF16)
        vb_ref[0] = zv.T.astype(BF16)
        for hh in range(n_heads):
            vf_ref[pl.ds(hh, zv.shape[0], stride=n_heads), :] = zv[:, hh * vd:(hh + 1) * vd]
    else:
        kf_ref[...] = kn
        vf_ref[...] = zv
    for g in range(4):
        z = _nn(h, w_ref[:, 3 * aw + g * rw:3 * aw + (g + 1) * rw])
        r_ref[:, g * rw:(g + 1) * rw] = z.astype(r_ref.dtype)


def _proj(x2, g1, w_bf, qg, kg, bd, aw, rw, vd, q_scale, emit_bf16, q_dtype, r_dtype, tm, n_batch):
    n, d = x2.shape
    kern = functools.partial(_proj_kernel, aw=aw, rw=rw, vd=vd, q_scale=q_scale, emit_bf16=emit_bf16)
    row = lambda i: (i, 0)
    const = lambda i: (0, 0)
    if emit_bf16:
        t = n // n_batch
        per_b = t // tm
        n_heads = aw // vd
        xposed = lambda i: (i // per_b, 0, i % per_b)
        out_shape = [jax.ShapeDtypeStruct((n, aw), q_dtype),
                     jax.ShapeDtypeStruct((n_batch, aw, t), F32),
                     jax.ShapeDtypeStruct((n * n_heads, vd), F32),
                     jax.ShapeDtypeStruct((n, aw), BF16),
                     jax.ShapeDtypeStruct((n_batch, aw, t), BF16)]
        out_specs = [pl.BlockSpec((tm, aw), row),
                     pl.BlockSpec((1, aw, tm), xposed),
                     pl.BlockSpec((tm * n_heads, vd), row),
                     pl.BlockSpec((tm, aw), row),
                     pl.BlockSpec((1, aw, tm), xposed)]
    else:
        out_shape = [jax.ShapeDtypeStruct((n, aw), q_dtype),
                     jax.ShapeDtypeStruct((n, aw), F32),
                     jax.ShapeDtypeStruct((n, aw), F32)]
        out_specs = [pl.BlockSpec((tm, aw), row)] * 3
    out_shape.append(jax.ShapeDtypeStruct((n, 4 * rw), r_dtype))
    out_specs.append(pl.BlockSpec((tm, 4 * rw), row))
    return pl.pallas_call(
        kern,
        grid=(n // tm,),
        in_specs=[pl.BlockSpec((tm, d), row),
                  pl.BlockSpec((1, d), const),
                  _const_spec(w_bf.shape, const),
                  pl.BlockSpec((1, aw), const),
                  pl.BlockSpec((1, aw), const),
                  pl.BlockSpec((aw, aw), const)],
        out_specs=out_specs,
        out_shape=out_shape,
        compiler_params=pltpu.CompilerParams(dimension_semantics=("arbitrary",),
                                             vmem_limit_bytes=VMEM_LIMIT),
        name="proj_bf16" if emit_bf16 else "proj_f32",
    )(x2, g1, w_bf, qg, kg, bd)


def _attn_kernel(lam_ref, q_ref, k_ref, vt_ref, bias_ref, sg_ref, o_ref,
                 qst_sc, sa_sc, ma_sc, sb_sc, mb_sc, m_sc, l_sc, acc_sc, *, hd, out_scale):
    i = pl.program_id(2)
    tkl = ATTN_GROUP * TQ

    qt = q_ref[0].astype(F32).T
    row = lax.broadcasted_iota(jnp.int32, qt.shape, 0)
    qst_sc[...] = jnp.concatenate([jnp.where(row < hd, qt, 0.0), jnp.where(row >= hd, qt, 0.0)],
                                  axis=1).astype(BF16)
    m_sc[...] = jnp.full(m_sc.shape, NEG, F32)
    l_sc[...] = jnp.zeros(l_sc.shape, F32)
    acc_sc[...] = jnp.zeros(acc_sc.shape, F32)

    def scores(start, size):
        return _nn(k_ref[0, pl.ds(start, size), :], qst_sc[...])

    def softmax_pv(s, s_max, start):
        m_old = m_sc[...]
        m_new = jnp.maximum(m_old, s_max)
        alpha = jnp.exp2(m_old - m_new)
        p = jnp.exp2(s - m_new)
        l_sc[...] = alpha * l_sc[...] + jnp.sum(p, axis=0, keepdims=True)
        acc_sc[...] = alpha * acc_sc[...] + _nn(vt_ref[0, :, pl.ds(start, s.shape[0])], p.astype(BF16))
        m_sc[...] = m_new

    bufs = ((sa_sc, ma_sc), (sb_sc, mb_sc))

    def issue(buf, g):
        s = scores(pl.multiple_of(g * tkl, tkl), tkl)
        buf[0][...] = s
        buf[1][...] = jnp.max(s, axis=0, keepdims=True)

    def consume(buf, g):
        softmax_pv(buf[0][...], buf[1][...], pl.multiple_of(g * tkl, tkl))

    n_big = jnp.maximum(i - 1, 0) // ATTN_GROUP
    issue(bufs[0], 0)

    def pair(jj, carry):
        g = 2 * jj
        issue(bufs[1], g + 1)
        consume(bufs[0], g)
        issue(bufs[0], g + 2)
        consume(bufs[1], g + 1)
        return carry

    lax.fori_loop(0, n_big // 2, pair, 0)
    odd = (n_big & 1) == 1

    @pl.when(odd)
    def _():
        issue(bufs[1], n_big)
        consume(bufs[0], n_big - 1)

    base = n_big * ATTN_GROUP
    n_ahead = jnp.minimum(i + 1 - base, ATTN_GROUP)

    def small(c, s):
        kind = jnp.where(c == i, 0, jnp.where(c == i - 1, 1, 2))
        bias = bias_ref[0, kind]
        s = jnp.concatenate([s[:, 0:TQ] + bias, s[:, TQ:2 * TQ] + bias], axis=1)
        softmax_pv(s, jnp.max(s, axis=0, keepdims=True), pl.multiple_of(c * TQ, TQ))

    def tail(buf):
        def ahead(u, carry):
            small(base + u, buf[0][pl.ds(pl.multiple_of(u * TQ, TQ), TQ), :])
            return carry
        lax.fori_loop(0, n_ahead, ahead, 0)

    @pl.when(odd)
    def _():
        tail(bufs[1])

    @pl.when(jnp.logical_not(odd))
    def _():
        tail(bufs[0])

    def inline(c, carry):
        small(c, scores(pl.multiple_of(c * TQ, TQ), TQ))
        return carry

    lax.fori_loop(base + n_ahead, i + 1, inline, 0)

    a = acc_sc[...] * (1.0 / l_sc[...])
    ot = a[:, 0:TQ] - lam_ref[0, 0] * a[:, TQ:2 * TQ]
    ms = jnp.mean(ot * ot, axis=0, keepdims=True)
    o = (ot * lax.rsqrt(ms + EPS)).T
    o_ref[0] = (o * (sg_ref[...] * out_scale)).astype(o_ref.dtype)


def _attn(lam, q, k, vt, bias, sg, n_heads, hd, out_scale):
    b, t, aw = q.shape
    vd = aw // n_heads
    assert t % (ATTN_GROUP * TQ) == 0
    kern = functools.partial(_attn_kernel, hd=hd, out_scale=out_scale)
    return pl.pallas_call(
        kern,
        grid=(b, n_heads, t // TQ),
        in_specs=[pl.BlockSpec(memory_space=pltpu.SMEM),
                  pl.BlockSpec((1, TQ, vd), lambda bi, h, i: (bi, i, h)),
                  pl.BlockSpec((1, t, vd), lambda bi, h, i: (bi, 0, h)),
                  pl.BlockSpec((1, vd, t), lambda bi, h, i: (bi, h, 0)),
                  pl.BlockSpec((1, 3, TQ, TQ), lambda bi, h, i: (h, 0, 0, 0)),
                  pl.BlockSpec((1, vd), lambda bi, h, i: (0, 0))],
        out_specs=pl.BlockSpec((1, TQ, vd), lambda bi, h, i: (bi, i, h)),
        out_shape=jax.ShapeDtypeStruct((b, t, aw), BF16),
        scratch_shapes=[pltpu.VMEM((vd, 2 * TQ), BF16),
                        pltpu.VMEM((ATTN_GROUP * TQ, 2 * TQ), F32),
                        pltpu.VMEM((1, 2 * TQ), F32),
                        pltpu.VMEM((ATTN_GROUP * TQ, 2 * TQ), F32),
                        pltpu.VMEM((1, 2 * TQ), F32),
                        pltpu.VMEM((1, 2 * TQ), F32),
                        pltpu.VMEM((1, 2 * TQ), F32),
                        pltpu.VMEM((vd, 2 * TQ), F32)],
        compiler_params=pltpu.CompilerParams(dimension_semantics=("arbitrary",) * 3,
                                             vmem_limit_bytes=VMEM_LIMIT),
        name="attn_prompt",
    )(lam, q, k, vt, bias, sg)


def _decode_kernel(pt_ref, lam_ref, q_ref, kn_ref, vn_ref, bias_ref, sg_ref, *refs,
                   n_pages, n_heads, hd, out_scale):
    k_refs = refs[:n_pages]
    v_refs = refs[n_pages:2 * n_pages]
    o_ref = refs[2 * n_pages]
    del pt_ref
    q8 = q_ref[...].astype(F32)
    aw = q8.shape[1]
    vd = aw // n_heads
    ps = k_refs[0].shape[2]
    lane = lax.broadcasted_iota(jnp.int32, q8.shape, 1)
    pieces = [jnp.where((lane >= hm * hd) & (lane < (hm + 1) * hd), q8, 0.0) for hm in range(2 * n_heads)]
    qbd = jnp.concatenate(pieces, axis=0).astype(BF16)
    s_list = []
    for p in range(n_pages):
        s = _nn(qbd, k_refs[p][0].astype(BF16))
        if p == n_pages - 1:
            s = s + bias_ref[0]
        s_list.append(s)
    pad = jnp.zeros((ps - SAMPLE_PAD, aw), F32)
    k_new = jnp.concatenate([kn_ref[...].astype(F32), pad], axis=0).astype(BF16)
    v_new = jnp.concatenate([vn_ref[...].astype(F32), pad], axis=0).astype(BF16)
    s_list.append(_nt(qbd, k_new) + bias_ref[1])
    s_all = jnp.concatenate(s_list, axis=1)
    m = jnp.max(s_all, axis=1, keepdims=True)
    e = jnp.exp2(s_all - m)
    inv = 1.0 / jnp.sum(e, axis=1, keepdims=True)
    eb = e.astype(BF16)
    lam = lam_ref[0, 0]
    rows = 2 * SAMPLE_PAD
    outs = []
    for h in range(n_heads):
        eh = eb[h * rows:(h + 1) * rows]
        acc = _nn(eh[:, n_pages * ps:(n_pages + 1) * ps], v_new[:, h * vd:(h + 1) * vd])
        for p in range(n_pages):
            vh = v_refs[p][0, pl.ds(h, ps, stride=n_heads), :].astype(BF16)
            acc = acc + _nn(eh[:, p * ps:(p + 1) * ps], vh)
        acc = acc * inv[h * rows:(h + 1) * rows]
        o = acc[0:SAMPLE_PAD] - lam * acc[SAMPLE_PAD:rows]
        ms = jnp.mean(o * o, axis=-1, keepdims=True)
        outs.append(o * lax.rsqrt(ms + EPS) * (sg_ref[...] * out_scale))
    o_ref[...] = jnp.concatenate(outs, axis=1).astype(o_ref.dtype)


def _decode(page_table, lam, q, kn, vn, bias, sg, ck, cv, n_heads, hd, out_scale):
    n_seq, n_pages = page_table.shape
    _, aw, ps = ck.shape
    vd = aw // n_heads
    kern = functools.partial(_decode_kernel, n_pages=n_pages, n_heads=n_heads, hd=hd, out_scale=out_scale)
    row = lambda s, pt: (s, 0)
    page_idx = [functools.partial(lambda s, pt, p: (pt[s, p], 0, 0), p=p) for p in range(n_pages)]
    k_specs = [pl.BlockSpec((1, aw, ps), im) for im in page_idx]
    v_specs = [pl.BlockSpec((1,) + cv.shape[1:], im) for im in page_idx]
    grid_spec = pltpu.PrefetchScalarGridSpec(
        num_scalar_prefetch=1,
        grid=(n_seq,),
        in_specs=[pl.BlockSpec(memory_space=pltpu.SMEM),
                  pl.BlockSpec((SAMPLE_PAD, aw), row),
                  pl.BlockSpec((SAMPLE_PAD, aw), row),
                  pl.BlockSpec((SAMPLE_PAD, aw), row),
                  pl.BlockSpec(bias.shape, lambda s, pt: (0, 0, 0)),
                  pl.BlockSpec((1, vd), lambda s, pt: (0, 0))] + k_specs + v_specs,
        out_specs=pl.BlockSpec((SAMPLE_PAD, aw), row),
    )
    return pl.pallas_call(
        kern,
        grid_spec=grid_spec,
        out_shape=jax.ShapeDtypeStruct((n_seq * SAMPLE_PAD, aw), F32),
        compiler_params=pltpu.CompilerParams(dimension_semantics=("arbitrary",),
                                             vmem_limit_bytes=VMEM_LIMIT),
        name="attn_decode",
    )(page_table, lam, q, kn, vn, bias, sg, *([ck] * n_pages), *([cv] * n_pages))


def _mid_reference(b, level):
    c, w = b.shape
    p = 1 << level
    half = p // 2
    if p >= SUBLANES:
        bp = b.reshape(c // p, p, w)
        return jnp.broadcast_to(bp[:, half - 1:half, :], (c // p, p, w)).reshape(c, w)
    b8 = b.reshape(c // SUBLANES, SUBLANES, w)
    rig = lax.broadcasted_iota(jnp.int32, b8.shape, 1)
    out = None
    for start in range(SUBLANES - p, -1, -p):
        row = jnp.broadcast_to(b8[:, start + half - 1:start + half, :], b8.shape)
        out = row if out is None else jnp.where(rig < start + p, row, out)
    return out.reshape(c, w)


def _neg_abs(x):
    bits = pltpu.bitcast(x, jnp.uint32) | jnp.uint32(0x80000000)
    return pltpu.bitcast(bits, F32)


def _upper_lower(up, low, level):
    c, w = up.shape
    half = (1 << level) // 2
    if half >= SUBLANES:
        shape = (c // (2 * half), 2, half, w)
        return jnp.concatenate([low.reshape(shape)[:, 0:1], up.reshape(shape)[:, 1:2]], axis=1).reshape(c, w)
    rowid = lax.broadcasted_iota(jnp.int32, up.shape, 0)
    return jnp.where((rowid & half) != 0, up, low)


def _hgrn_kernel(r_ref, lb_ref, gn_ref, tri_ref, lev_ref, o_ref, sfin_ref, st_sc, *, n_heads, kd):
    t = pl.program_id(1)
    c = HG_CHUNK
    rw = n_heads * kd
    n_levels = c.bit_length() - 1

    @pl.when(t == 0)
    def _():
        st_sc[...] = jnp.zeros(st_sc.shape, F32)

    def chunk(ci, carry):
        r0 = pl.multiple_of(ci * c, c)
        lev = lev_ref[...]
        tri = tri_ref[...]
        for h in range(n_heads):
            cols = slice(h * kd, (h + 1) * kd)
            q = r_ref[0, pl.ds(r0, c), cols].astype(F32)
            rf = r_ref[0, pl.ds(r0, c), rw + h * kd:rw + (h + 1) * kd].astype(F32)
            v = r_ref[0, pl.ds(r0, c), 2 * rw + h * kd:2 * rw + (h + 1) * kd].astype(BF16)
            rg = r_ref[0, pl.ds(r0, c), 3 * rw + h * kd:3 * rw + (h + 1) * kd].astype(F32)
            lb = lb_ref[:, cols]
            f = lb + (1.0 - lb) * jax.nn.sigmoid(rf)
            logf = jnp.log(f) * LOG2E
            kk = 1.0 - f
            hi = logf.astype(BF16)
            lo = (logf - hi.astype(F32)).astype(BF16)
            b = _nn(tri, hi) + _nn(tri, lo)
            a = jnp.where(lev == 0, _nt(q.astype(BF16), kk.astype(BF16)), 0.0)
            for level in range(1, n_levels + 1):
                e = jnp.exp2(_neg_abs(b - _mid_reference(b, level)))
                z = (_upper_lower(q, kk, level) * e).astype(BF16)
                a = jnp.where(lev == level, _nt(z, z), a)
            st = st_sc[h]
            o = _nn(a.astype(BF16), v) + _nt((q * jnp.exp2(b)).astype(BF16), st.astype(BF16))
            b_last = b[c - 1:c, :]
            k_hat = (kk * jnp.exp2(b_last - b)).astype(BF16)
            st_sc[h] = jnp.exp2(b_last) * st + _tn(v, k_hat)
            ms = jnp.mean(o * o, axis=-1, keepdims=True)
            on = o * lax.rsqrt(ms + EPS) * gn_ref[...] * (rg * jax.nn.sigmoid(rg))
            o_ref[0, pl.ds(r0, c), cols] = on.astype(o_ref.dtype)
        return carry

    lax.fori_loop(0, HG_BLOCK // c, chunk, 0)

    @pl.when(t == pl.num_programs(1) - 1)
    def _():
        for h in range(n_heads):
            sfin_ref[0, h] = st_sc[h].T


def _hgrn_prompt(r, lb, gn, n_heads, kd):
    b, t, _ = r.shape
    c = HG_CHUNK
    idx = np.arange(c)
    x = idx[:, None] ^ idx[None, :]
    lev = np.where(idx[:, None] > idx[None, :], np.floor(np.log2(np.maximum(x, 1))).astype(np.int32) + 1, -1)
    lev = np.where(idx[:, None] == idx[None, :], 0, lev).astype(np.int32)
    tri = (idx[:, None] >= idx[None, :]).astype(np.float32)
    kern = functools.partial(_hgrn_kernel, n_heads=n_heads, kd=kd)
    return pl.pallas_call(
        kern,
        grid=(b, t // HG_BLOCK),
        in_specs=[pl.BlockSpec((1, HG_BLOCK, r.shape[2]), lambda bi, ti: (bi, ti, 0)),
                  pl.BlockSpec((1, n_heads * kd), lambda bi, ti: (0, 0)),
                  pl.BlockSpec((1, kd), lambda bi, ti: (0, 0)),
                  pl.BlockSpec((c, c), lambda bi, ti: (0, 0)),
                  pl.BlockSpec((c, c), lambda bi, ti: (0, 0))],
        out_specs=[pl.BlockSpec((1, HG_BLOCK, n_heads * kd), lambda bi, ti: (bi, ti, 0)),
                   pl.BlockSpec((1, n_heads, kd, kd), lambda bi, ti: (bi, 0, 0, 0))],
        out_shape=[jax.ShapeDtypeStruct((b, t, n_heads * kd), BF16),
                   jax.ShapeDtypeStruct((b, n_heads, kd, kd), F32)],
        scratch_shapes=[pltpu.VMEM((n_heads, kd, kd), F32)],
        compiler_params=pltpu.CompilerParams(dimension_semantics=("arbitrary", "arbitrary"),
                                             vmem_limit_bytes=VMEM_LIMIT),
        name="hgrn_prompt",
    )(r, lb, gn, jnp.asarray(tri, BF16), jnp.asarray(lev))


def _hgrn_step_kernel(r_ref, lb_ref, gn_ref, s_ref, o_ref, so_ref, *, n_heads, kd, n_tok, n_seq):
    rw = n_heads * kd
    rowid = lax.broadcasted_iota(jnp.int32, (SAMPLE_PAD, kd), 0)
    live = rowid < n_tok
    zpad = jnp.zeros((SAMPLE_PAD, kd), F32)
    pad16 = lambda z: jnp.concatenate([z, zpad], axis=0).astype(BF16)
    ones16 = jnp.ones((2 * SAMPLE_PAD, kd), BF16)
    for g in range(n_seq):
        rows = slice(g * SAMPLE_PAD, (g + 1) * SAMPLE_PAD)
        for h in range(n_heads):
            cols = slice(h * kd, (h + 1) * kd)
            q = r_ref[rows, h * kd:(h + 1) * kd]
            rf = r_ref[rows, rw + h * kd:rw + (h + 1) * kd]
            v = r_ref[rows, 2 * rw + h * kd:2 * rw + (h + 1) * kd]
            rg = r_ref[rows, 3 * rw + h * kd:3 * rw + (h + 1) * kd]
            lb = lb_ref[:, cols]
            f = lb + (1.0 - lb) * jax.nn.sigmoid(rf)
            kk = jnp.where(live, 1.0 - f, 0.0)
            b = jnp.where(live, jnp.log(f) * LOG2E, 0.0)
            for sh in (1, 2, 4):
                b = b + jnp.where(rowid >= sh, pltpu.roll(b, sh, 0), 0.0)
            s0 = s_ref[g, h]
            o = jnp.sum(q * kk, axis=1, keepdims=True) * v
            for d in range(1, n_tok):
                x = q * pltpu.roll(kk, d, 0) * jnp.exp2(b - pltpu.roll(b, d, 0))
                w = jnp.sum(jnp.where(rowid >= d, x, 0.0), axis=1, keepdims=True)
                o = o + w * pltpu.roll(v, d, 0)
            o = o + _nn(pad16(q * jnp.exp2(b)), s0.astype(BF16))[0:SAMPLE_PAD]
            b_last = b[n_tok - 1:n_tok, :]
            decay = jnp.exp2(b_last)
            d_hi = decay.astype(BF16).astype(F32)
            d_split = jnp.where(rowid == 0, d_hi, jnp.where(rowid == 1, decay - d_hi, 0.0))
            decay_cols = _tn(pad16(d_split), ones16)
            so_ref[g, h] = decay_cols * s0 + _tn(pad16(kk * jnp.exp2(b_last - b)), pad16(v))
            ms = jnp.mean(o * o, axis=-1, keepdims=True)
            on = o * lax.rsqrt(ms + EPS) * gn_ref[...] * (rg * jax.nn.sigmoid(rg))
            o_ref[rows, cols] = on.astype(o_ref.dtype)


def _hgrn_sample(r, lb, gn, state, n_heads, kd, n_tok, seq_per_step=2):
    n_seq = state.shape[0]
    kern = functools.partial(_hgrn_step_kernel, n_heads=n_heads, kd=kd, n_tok=n_tok, n_seq=seq_per_step)
    rows = seq_per_step * SAMPLE_PAD
    return pl.pallas_call(
        kern,
        grid=(n_seq // seq_per_step,),
        in_specs=[pl.BlockSpec((rows, r.shape[1]), lambda i: (i, 0)),
                  pl.BlockSpec((1, n_heads * kd), lambda i: (0, 0)),
                  pl.BlockSpec((1, kd), lambda i: (0, 0)),
                  pl.BlockSpec((seq_per_step, n_heads, kd, kd), lambda i: (i, 0, 0, 0))],
        out_specs=[pl.BlockSpec((rows, n_heads * kd), lambda i: (i, 0)),
                   pl.BlockSpec((seq_per_step, n_heads, kd, kd), lambda i: (i, 0, 0, 0))],
        out_shape=[jax.ShapeDtypeStruct((n_seq * SAMPLE_PAD, n_heads * kd), F32),
                   jax.ShapeDtypeStruct(state.shape, F32)],
        compiler_params=pltpu.CompilerParams(dimension_semantics=("arbitrary",),
                                             vmem_limit_bytes=VMEM_LIMIT),
        name="hgrn_sample",
    )(r, lb, gn, state)


def _ffn_kernel(*refs, sample, aw):
    if sample:
        (x_ref, oa_ref, or_ref, wo_ref, g2_ref, wg_ref, wu_ref, cw_ref, cb_ref, wd_ref, prev_ref,
         y_ref, a_ref) = refs
    else:
        (x_ref, oa_ref, or_ref, wo_ref, g2_ref, wg_ref, wu_ref, cw_ref, cb_ref, wd_ref,
         y_ref, a_ref, carry_sc) = refs

        @pl.when(pl.program_id(1) == 0)
        def _():
            carry_sc[...] = jnp.zeros(carry_sc.shape, F32)

    x = x_ref[...].astype(F32)
    x1 = x + _nn(oa_ref[...].astype(BF16), wo_ref[0:aw, :]) + _nn(or_ref[...].astype(BF16), wo_ref[aw:, :])
    ms = jnp.mean(x1 * x1, axis=-1, keepdims=True)
    h2 = (x1 * lax.rsqrt(ms + EPS) * g2_ref[...]).astype(BF16)
    tm = x.shape[0]
    a = _nn(h2, wg_ref[...])
    u = _nn(h2, wu_ref[...])
    rowid = lax.broadcasted_iota(jnp.int32, a.shape, 0)
    am1 = pltpu.roll(a, 1, 0)
    am2 = pltpu.roll(a, 2, 0)
    if sample:
        rig = rowid & (SAMPLE_PAD - 1)
        prev = prev_ref[...]
        am1 = jnp.where(rig == 0, pltpu.roll(prev, tm - 1, 0), am1)
        am2 = jnp.where(rig < 2, prev, am2)
        a_ref[...] = a
    else:
        last = carry_sc[...]
        am1 = jnp.where(rowid == 0, last[SUBLANES - 1:SUBLANES, :], am1)
        am2 = jnp.where(rowid == 0, last[SUBLANES - 2:SUBLANES - 1, :],
                        jnp.where(rowid == 1, last[SUBLANES - 1:SUBLANES, :], am2))
        tail = a[tm - SUBLANES:tm, :]
        carry_sc[...] = tail
        a_ref[0] = tail
    cw = cw_ref[...]
    cv = cb_ref[...] + cw[0:1, :] * am2 + cw[1:2, :] * am1 + cw[2:3, :] * a
    gate = (cv * jax.nn.sigmoid(cv) * u).astype(BF16)
    y_ref[...] = x1 + _nn(gate, wd_ref[...])


def _ffn(x2, oa, orec, wo, g2, wg, wu, cw, cb, wd, prev, tm, n_batch):
    n, d = x2.shape
    aw = oa.shape[1]
    dff = wg.shape[1]
    sample = prev is not None
    kern = functools.partial(_ffn_kernel, sample=sample, aw=aw)
    scratch = []
    if sample:
        grid = (n // tm,)
        row = lambda i: (i, 0)
        c2 = lambda i: (0, 0)
        extra_in = [pl.BlockSpec((tm, dff), row)]
        extra_args = [prev]
        a_shape = jax.ShapeDtypeStruct((n, dff), F32)
        a_spec = pl.BlockSpec((tm, dff), row)
        sem = ("arbitrary",)
    else:
        per_b = n // n_batch // tm
        grid = (n_batch, per_b)
        row = lambda b, i: (b * per_b + i, 0)
        c2 = lambda b, i: (0, 0)
        extra_in = []
        extra_args = []
        a_shape = jax.ShapeDtypeStruct((n_batch, SUBLANES, dff), F32)
        a_spec = pl.BlockSpec((1, SUBLANES, dff), lambda b, i: (b, 0, 0))
        scratch.append(pltpu.VMEM((SUBLANES, dff), F32))
        sem = ("arbitrary", "arbitrary")
    return pl.pallas_call(
        kern,
        grid=grid,
        in_specs=[pl.BlockSpec((tm, d), row),
                  pl.BlockSpec((tm, aw), row),
                  pl.BlockSpec((tm, aw), row),
                  _const_spec(wo.shape, c2),
                  pl.BlockSpec((1, d), c2),
                  _const_spec(wg.shape, c2),
                  _const_spec(wu.shape, c2),
                  pl.BlockSpec(cw.shape, c2),
                  pl.BlockSpec(cb.shape, c2),
                  _const_spec(wd.shape, c2)] + extra_in,
        out_specs=[pl.BlockSpec((tm, d), row), a_spec],
        out_shape=[jax.ShapeDtypeStruct((n, d), F32), a_shape],
        scratch_shapes=scratch,
        compiler_params=pltpu.CompilerParams(dimension_semantics=sem, vmem_limit_bytes=VMEM_LIMIT),
        name="ffn_sample" if sample else "ffn_prompt",
    )(x2, oa, orec, wo, g2, wg, wu, cw, cb, wd, *extra_args)


def kernel(x_prompt, x_sample, cache_k, cache_v, page_table, state_hgrn, state_conv, rel_bias, norm1_g, w_in,
           qk_norm_g, lambda_qk, subln_g, hgrn_lb, hgrn_onorm_g, w_out, norm2_g, w_gate, w_up, conv_w, conv_b,
           w_down):
    bp, tp, d = x_prompt.shape
    bs, ts, _ = x_sample.shape
    depth = w_in.shape[0]
    assert depth == 1 and CONV_W - 1 <= ts <= SAMPLE_PAD and conv_w.shape[1] == CONV_W
    n_heads, _, hd = cache_k.shape[3:]
    vd = cache_v.shape[4]
    aw = n_heads * vd
    r_heads, kd = state_hgrn.shape[2], state_hgrn.shape[3]
    rw = r_heads * kd
    dff = w_gate.shape[2]
    page_size = cache_k.shape[2]
    assert page_size >= MAX_DISTANCE and tp % TQ == 0 and tp % HG_BLOCK == 0
    l = 0
    lam_init = 0.8 - 0.6 * math.exp(-0.3 * l)
    out_scale = 1.0 - lam_init

    biasp, biass, lam, lb = _tables(rel_bias, lambda_qk[l], hgrn_lb, n_heads, ts, page_size, lam_init)

    w_in_bf = w_in[l].astype(BF16)
    wo_bf = w_out[l].astype(BF16)
    wg_bf = w_gate[l].astype(BF16)
    wu_bf = w_up[l].astype(BF16)
    wd_bf = w_down[l].astype(BF16)
    cw = conv_w[l]
    cb = conv_b[l].reshape(1, dff)
    g1 = norm1_g[l].reshape(1, d)
    g2 = norm2_g[l].reshape(1, d)
    qg = jnp.tile(qk_norm_g[l, 0].reshape(1, 2 * hd), (1, n_heads))
    kg = jnp.tile(qk_norm_g[l, 1].reshape(1, 2 * hd), (1, n_heads))
    seg = np.arange(aw) // hd
    bd = jnp.asarray((seg[:, None] == seg[None, :]).astype(np.float32) / hd, BF16)
    sg = subln_g[l].reshape(1, vd)
    gn = hgrn_onorm_g[l].reshape(1, kd)
    q_scale = hd ** -0.5 * LOG2E

    xp2 = x_prompt.reshape(bp * tp, d)
    qb, kft, vf, kb, vbt, rp = _proj(xp2, g1, w_in_bf, qg, kg, bd, aw, rw, vd, q_scale, True, BF16, BF16, 512, bp)
    o_att = _attn(lam, qb.reshape(bp, tp, aw), kb.reshape(bp, tp, aw), vbt, biasp, sg, n_heads, hd, out_scale)
    o_rec, s_p = _hgrn_prompt(rp.reshape(bp, tp, 4 * rw), lb, gn, r_heads, kd)
    yp, a_tail = _ffn(xp2, o_att.reshape(bp * tp, aw), o_rec.reshape(bp * tp, rw), wo_bf, g2, wg_bf, wu_bf, cw, cb,
                      wd_bf, None, 256, bp)
    k_prompt = kft.reshape(bp, n_heads, 2, hd, tp).transpose(0, 4, 1, 2, 3)[None]
    v_prompt = vf.reshape(1, bp, tp, n_heads, vd)
    conv_prompt = a_tail[:, SUBLANES - (CONV_W - 1):][None]

    xs_pad = jnp.pad(x_sample, ((0, 0), (0, SAMPLE_PAD - ts), (0, 0))).reshape(bs * SAMPLE_PAD, d)
    qs, ksf, vsf, rs = _proj(xs_pad, g1, w_in_bf, qg, kg, bd, aw, rw, vd, q_scale, False, F32, F32, 512, bs)
    ck = jnp.transpose(cache_k[l], (0, 2, 3, 4, 1)).reshape(cache_k.shape[1], aw, page_size)
    cv = cache_v[l].reshape(cache_v.shape[1], page_size * n_heads, vd)
    o_att_s = _decode(page_table, lam, qs, ksf, vsf, biass, sg, ck, cv, n_heads, hd, out_scale)
    o_rec_s, s_s = _hgrn_sample(rs, lb, gn, state_hgrn[l], r_heads, kd, ts)
    prev = jnp.pad(state_conv[l], ((0, 0), (0, SAMPLE_PAD - (CONV_W - 1)), (0, 0))).reshape(bs * SAMPLE_PAD, dff)
    ys, a_s = _ffn(xs_pad, o_att_s, o_rec_s, wo_bf, g2, wg_bf, wu_bf, cw, cb, wd_bf, prev, 256, 1)
    unpad = lambda z: z.reshape(bs, SAMPLE_PAD, -1)[:, :ts]
    y_sample = unpad(ys)
    k_sample = unpad(ksf).reshape(1, bs, ts, n_heads, 2, hd)
    v_sample = unpad(vsf).reshape(1, bs, ts, n_heads, vd)
    conv_sample = a_s.reshape(bs, SAMPLE_PAD, dff)[:, ts - (CONV_W - 1):ts][None]

    return (yp.reshape(bp, tp, d), y_sample, k_prompt, v_prompt, s_p[None], conv_prompt,
            k_sample, v_sample, s_s[None], conv_sample)
```

```python
import functools
import math

import numpy as np
import jax
import jax.numpy as jnp
from jax import lax
from jax.experimental import pallas as pl
from jax.experimental.pallas import tpu as pltpu

F32 = jnp.float32
BF16 = jnp.bfloat16

EPS = 1e-6
N_BUCKETS = 32
MAX_DISTANCE = 128
CONV_W = 3
NEG = -1e30
LOG2E = math.log2(math.e)
LANES = 128
SUBLANES = 8

TQ = 512
HG_CHUNK = 128
HG_BLOCK = 512
SAMPLE_PAD = 8
VMEM_LIMIT = 56 * 1024 * 1024


def _const_spec(shape, index_map):
    return pl.BlockSpec(shape, index_map, pipeline_mode=pl.Buffered(1))


def _nt(a, b):
    return lax.dot_general(a, b, (((1,), (1,)), ((), ())), preferred_element_type=F32)


def _tn(a, b):
    return lax.dot_general(a, b, (((0,), (0,)), ((), ())), preferred_element_type=F32)


def _nn(a, b):
    return jnp.dot(a, b, preferred_element_type=F32)


def _rel_bucket_np(n):
    n = np.maximum(n, 0)
    max_exact = N_BUCKETS // 2
    nf = np.maximum(n, 1).astype(np.float32)
    large = max_exact + (np.log(nf / np.float32(max_exact)) / np.float32(math.log(MAX_DISTANCE / max_exact))
                         * np.float32(N_BUCKETS - max_exact)).astype(np.int32)
    large = np.minimum(large, N_BUCKETS - 1)
    return np.where(n < max_exact, n, large).astype(np.int32)


def _tables_kernel(relb_ref, lq_ref, hlb_ref, bkp_ref, bks_ref,
                   biasp_ref, biass_ref, lam_ref, lb_ref, *, n_heads, lam_init, near_blocks):
    for h in range(n_heads):
        far = relb_ref[N_BUCKETS - 1, h]
        for kind in range(2):
            biasp_ref[h, kind] = jnp.where(bkp_ref[kind] < 0, NEG, 0.0).astype(F32)
            for (r0, c0) in near_blocks[kind]:
                bk = bkp_ref[kind, r0:r0 + LANES, c0:c0 + LANES]
                acc = jnp.where(bk < 0, NEG, 0.0).astype(F32)
                for b in range(N_BUCKETS - 1):
                    acc = jnp.where(bk == b, (relb_ref[b, h] - far) * LOG2E, acc)
                biasp_ref[h, kind, r0:r0 + LANES, c0:c0 + LANES] = acc
            rows = bks_ref.shape[1] // n_heads
            bk = bks_ref[kind, h * rows:(h + 1) * rows, :]
            acc = jnp.where(bk < 0, NEG, 0.0).astype(F32)
            for b in range(N_BUCKETS - 1):
                acc = jnp.where(bk == b, (relb_ref[b, h] - far) * LOG2E, acc)
            biass_ref[kind, h * rows:(h + 1) * rows, :] = acc
    lq = lq_ref[...].astype(F32)
    s1 = jnp.sum(lq[0:1] * lq[1:2], axis=1, keepdims=True)
    s2 = jnp.sum(lq[2:3] * lq[3:4], axis=1, keepdims=True)
    lam = jnp.exp(s1) - jnp.exp(s2) + lam_init
    lam_ref[...] = jnp.broadcast_to(lam, lam_ref.shape)
    hl = hlb_ref[...].astype(F32)
    mx = jnp.max(hl, axis=0, keepdims=True)
    e = jnp.exp(hl - mx)
    lb_ref[...] = e[0:1] / jnp.sum(e, axis=0, keepdims=True)


def _tables(rel_bias, lambda_qk_l, hgrn_lb, n_heads, n_tok, page_size, lam_init):
    c = np.arange(TQ)[:, None]
    r = np.arange(TQ)[None, :]
    diag = np.where(c <= r, _rel_bucket_np(r - c), -1)
    prev = _rel_bucket_np(TQ + r - c)
    bkp_np = np.stack([diag, prev]).astype(np.int32)
    bkp = jnp.asarray(bkp_np)
    near_blocks = tuple(
        tuple((r0, c0) for r0 in range(0, TQ, LANES) for c0 in range(0, TQ, LANES)
              if np.any((bkp_np[kind, r0:r0 + LANES, c0:c0 + LANES] >= 0)
                        & (bkp_np[kind, r0:r0 + LANES, c0:c0 + LANES] < N_BUCKETS - 1)))
        for kind in range(2))
    t = np.tile(np.arange(SAMPLE_PAD), 2 * n_heads)[:, None]
    cc = np.arange(page_size)[None, :]
    last_page = _rel_bucket_np(page_size + t - cc)
    new_page = np.where(cc <= np.minimum(t, n_tok - 1), _rel_bucket_np(t - cc), -1)
    bks = jnp.asarray(np.stack([last_page, new_page]).astype(np.int32))
    rows_s = 2 * n_heads * SAMPLE_PAD
    kern = functools.partial(_tables_kernel, n_heads=n_heads, lam_init=lam_init, near_blocks=near_blocks)
    return pl.pallas_call(
        kern,
        out_shape=(jax.ShapeDtypeStruct((n_heads, 2, TQ, TQ), F32),
                   jax.ShapeDtypeStruct((2, rows_s, page_size), F32),
                   jax.ShapeDtypeStruct((SUBLANES, LANES), F32),
                   jax.ShapeDtypeStruct((1, hgrn_lb.shape[1]), F32)),
        in_specs=[pl.BlockSpec(memory_space=pltpu.SMEM),
                  pl.BlockSpec(memory_space=pltpu.VMEM),
                  pl.BlockSpec(memory_space=pltpu.VMEM),
                  pl.BlockSpec(memory_space=pltpu.VMEM),
                  pl.BlockSpec(memory_space=pltpu.VMEM)],
        name="tables",
    )(rel_bias, lambda_qk_l, hgrn_lb, bkp, bks)


def _proj_kernel(x_ref, g1_ref, w_ref, qg_ref, kg_ref, bd_ref, *out_refs, aw, rw, vd, q_scale, emit_bf16):
    if emit_bf16:
        q_ref, kf_ref, vf_ref, kb_ref, vb_ref, r_ref = out_refs
    else:
        q_ref, kf_ref, vf_ref, r_ref = out_refs
    x = x_ref[...].astype(F32)
    ms = jnp.mean(x * x, axis=-1, keepdims=True)
    h = (x * lax.rsqrt(ms + EPS) * g1_ref[...]).astype(BF16)

    def seg_norm(z, g):
        msq = _nn((z * z).astype(BF16), bd_ref[...])
        return z * lax.rsqrt(msq + EPS) * g

    zq = _nn(h, w_ref[:, 0:aw])
    q_ref[...] = (seg_norm(zq, qg_ref[...]) * q_scale).astype(q_ref.dtype)
    zk = _nn(h, w_ref[:, aw:2 * aw])
    kn = seg_norm(zk, kg_ref[...])
    zv = _nn(h, w_ref[:, 2 * aw:3 * aw])
    if emit_bf16:
        n_heads = aw // vd
        kf_ref[0] = kn.T
        kb_ref[...] = kn.astype(BF16)
        vb_ref[0] = zv.T.astype(BF16)
        for hh in range(n_heads):
            vf_ref[pl.ds(hh, zv.shape[0], stride=n_heads), :] = zv[:, hh * vd:(hh + 1) * vd]
    else:
        kf_ref[...] = kn
        vf_ref[...] = zv
    for g in range(4):
        z = _nn(h, w_ref[:, 3 * aw + g * rw:3 * aw + (g + 1) * rw])
        r_ref[:, g * rw:(g + 1) * rw] = z.astype(r_ref.dtype)


def _proj(x2, g1, w_bf, qg, kg, bd, aw, rw, vd, q_scale, emit_bf16, q_dtype, r_dtype, tm, n_batch):
    n, d = x2.shape
    kern = functools.partial(_proj_kernel, aw=aw, rw=rw, vd=vd, q_scale=q_scale, emit_bf16=emit_bf16)
    row = lambda i: (i, 0)
    const = lambda i: (0, 0)
    if emit_bf16:
        t = n // n_batch
        per_b = t // tm
        n_heads = aw // vd
        xposed = lambda i: (i // per_b, 0, i % per_b)
        out_shape = [jax.ShapeDtypeStruct((n, aw), q_dtype),
                     jax.ShapeDtypeStruct((n_batch, aw, t), F32),
                     jax.ShapeDtypeStruct((n * n_heads, vd), F32),
                     jax.ShapeDtypeStruct((n, aw), BF16),
                     jax.ShapeDtypeStruct((n_batch, aw, t), BF16)]
        out_specs = [pl.BlockSpec((tm, aw), row),
                     pl.BlockSpec((1, aw, tm), xposed),
                     pl.BlockSpec((tm * n_heads, vd), row),
                     pl.BlockSpec((tm, aw), row),
                     pl.BlockSpec((1, aw, tm), xposed)]
    else:
        out_shape = [jax.ShapeDtypeStruct((n, aw), q_dtype),
                     jax.ShapeDtypeStruct((n, aw), F32),
                     jax.ShapeDtypeStruct((n, aw), F32)]
        out_specs = [pl.BlockSpec((tm, aw), row)] * 3
    out_shape.append(jax.ShapeDtypeStruct((n, 4 * rw), r_dtype))
    out_specs.append(pl.BlockSpec((tm, 4 * rw), row))
    return pl.pallas_call(
        kern,
        grid=(n // tm,),
        in_specs=[pl.BlockSpec((tm, d), row),
                  pl.BlockSpec((1, d), const),
                  _const_spec(w_bf.shape, const),
                  pl.BlockSpec((1, aw), const),
                  pl.BlockSpec((1, aw), const),
                  pl.BlockSpec((aw, aw), const)],
        out_specs=out_specs,
        out_shape=out_shape,
        compiler_params=pltpu.CompilerParams(dimension_semantics=("arbitrary",),
                                             vmem_limit_bytes=VMEM_LIMIT),
        name="proj_bf16" if emit_bf16 else "proj_f32",
    )(x2, g1, w_bf, qg, kg, bd)


def _attn_kernel(lam_ref, q_ref, k_ref, vt_ref, bias_ref, sg_ref, o_ref,
                 qst_sc, sa_sc, ma_sc, sb_sc, mb_sc, m_sc, l_sc, acc_sc, *, hd, out_scale):
    i = pl.program_id(2)

    qt = q_ref[0].astype(F32).T
    row = lax.broadcasted_iota(jnp.int32, qt.shape, 0)
    qst_sc[...] = jnp.concatenate([jnp.where(row < hd, qt, 0.0), jnp.where(row >= hd, qt, 0.0)],
                                  axis=1).astype(BF16)
    m_sc[...] = jnp.full(m_sc.shape, NEG, F32)
    l_sc[...] = jnp.zeros(l_sc.shape, F32)
    acc_sc[...] = jnp.zeros(acc_sc.shape, F32)

    bufs = ((sa_sc, ma_sc), (sb_sc, mb_sc))

    def issue(buf, g, with_max):
        start = pl.multiple_of(g * TQ, TQ)
        s = _nn(k_ref[0, pl.ds(start, TQ), :], qst_sc[...])
        buf[0][...] = s
        if with_max:
            buf[1][...] = jnp.max(s, axis=0, keepdims=True)

    def consume(buf, g, kind):
        s = buf[0][...]
        if kind is None:
            s_max = buf[1][...]
        else:
            bias = bias_ref[0, kind]
            s = jnp.concatenate([s[:, 0:TQ] + bias, s[:, TQ:2 * TQ] + bias], axis=1)
            s_max = jnp.max(s, axis=0, keepdims=True)
        m_old = m_sc[...]
        m_new = jnp.maximum(m_old, s_max)
        alpha = jnp.exp2(m_old - m_new)
        p = jnp.exp2(s - m_new)
        l_sc[...] = alpha * l_sc[...] + jnp.sum(p, axis=0, keepdims=True)
        start = pl.multiple_of(g * TQ, TQ)
        acc_sc[...] = alpha * acc_sc[...] + _nn(vt_ref[0, :, pl.ds(start, TQ)], p.astype(BF16))
        m_sc[...] = m_new

    n_pure = jnp.maximum(i - 1, 0)
    issue(bufs[0], 0, True)

    def pair(jj, carry):
        g = 2 * jj
        issue(bufs[1], g + 1, True)
        consume(bufs[0], g, None)
        issue(bufs[0], g + 2, True)
        consume(bufs[1], g + 1, None)
        return carry

    lax.fori_loop(0, n_pure // 2, pair, 0)
    odd = (n_pure & 1) == 1

    def finish(cur, other):
        @pl.when(i > 0)
        def _():
            issue(other, i, False)
            consume(cur, i - 1, 1)
            consume(other, i, 0)

        @pl.when(i == 0)
        def _():
            consume(cur, 0, 0)

    @pl.when(odd)
    def _():
        issue(bufs[1], n_pure, False)
        consume(bufs[0], n_pure - 1, None)
        finish(bufs[1], bufs[0])

    @pl.when(jnp.logical_not(odd))
    def _():
        finish(bufs[0], bufs[1])

    a = acc_sc[...] * (1.0 / l_sc[...])
    ot = a[:, 0:TQ] - lam_ref[0, 0] * a[:, TQ:2 * TQ]
    ms = jnp.mean(ot * ot, axis=0, keepdims=True)
    o = (ot * lax.rsqrt(ms + EPS)).T
    o_ref[0] = (o * (sg_ref[...] * out_scale)).astype(o_ref.dtype)


def _attn(lam, q, k, vt, bias, sg, n_heads, hd, out_scale):
    b, t, aw = q.shape
    vd = aw // n_heads
    assert t % TQ == 0
    kern = functools.partial(_attn_kernel, hd=hd, out_scale=out_scale)
    return pl.pallas_call(
        kern,
        grid=(b, n_heads, t // TQ),
        in_specs=[pl.BlockSpec(memory_space=pltpu.SMEM),
                  pl.BlockSpec((1, TQ, vd), lambda bi, h, i: (bi, i, h)),
                  pl.BlockSpec((1, t, vd), lambda bi, h, i: (bi, 0, h)),
                  pl.BlockSpec((1, vd, t), lambda bi, h, i: (bi, h, 0)),
                  pl.BlockSpec((1, 2, TQ, TQ), lambda bi, h, i: (h, 0, 0, 0)),
                  pl.BlockSpec((1, vd), lambda bi, h, i: (0, 0))],
        out_specs=pl.BlockSpec((1, TQ, vd), lambda bi, h, i: (bi, i, h)),
        out_shape=jax.ShapeDtypeStruct((b, t, aw), BF16),
        scratch_shapes=[pltpu.VMEM((vd, 2 * TQ), BF16),
                        pltpu.VMEM((TQ, 2 * TQ), F32),
                        pltpu.VMEM((1, 2 * TQ), F32),
                        pltpu.VMEM((TQ, 2 * TQ), F32),
                        pltpu.VMEM((1, 2 * TQ), F32),
                        pltpu.VMEM((1, 2 * TQ), F32),
                        pltpu.VMEM((1, 2 * TQ), F32),
                        pltpu.VMEM((vd, 2 * TQ), F32)],
        compiler_params=pltpu.CompilerParams(dimension_semantics=("arbitrary",) * 3,
                                             vmem_limit_bytes=VMEM_LIMIT),
        name="attn_prompt",
    )(lam, q, k, vt, bias, sg)


def _decode_kernel(pt_ref, lam_ref, q_ref, kn_ref, vn_ref, bias_ref, sg_ref, *refs,
                   n_pages, n_heads, hd, out_scale):
    k_refs = refs[:n_pages]
    v_refs = refs[n_pages:2 * n_pages]
    o_ref = refs[2 * n_pages]
    del pt_ref
    q8 = q_ref[...].astype(F32)
    aw = q8.shape[1]
    vd = aw // n_heads
    ps = k_refs[0].shape[2]
    lane = lax.broadcasted_iota(jnp.int32, q8.shape, 1)
    pieces = [jnp.where((lane >= hm * hd) & (lane < (hm + 1) * hd), q8, 0.0) for hm in range(2 * n_heads)]
    qbd = jnp.concatenate(pieces, axis=0).astype(BF16)
    s_list = []
    for p in range(n_pages):
        s = _nn(qbd, k_refs[p][0].astype(BF16))
        if p == n_pages - 1:
            s = s + bias_ref[0]
        s_list.append(s)
    pad = jnp.zeros((ps - SAMPLE_PAD, aw), F32)
    k_new = jnp.concatenate([kn_ref[...].astype(F32), pad], axis=0).astype(BF16)
    v_new = jnp.concatenate([vn_ref[...].astype(F32), pad], axis=0).astype(BF16)
    s_list.append(_nt(qbd, k_new) + bias_ref[1])
    s_all = jnp.concatenate(s_list, axis=1)
    m = jnp.max(s_all, axis=1, keepdims=True)
    e = jnp.exp2(s_all - m)
    inv = 1.0 / jnp.sum(e, axis=1, keepdims=True)
    eb = e.astype(BF16)
    lam = lam_ref[0, 0]
    rows = 2 * SAMPLE_PAD
    outs = []
    for h in range(n_heads):
        eh = eb[h * rows:(h + 1) * rows]
        acc = _nn(eh[:, n_pages * ps:(n_pages + 1) * ps], v_new[:, h * vd:(h + 1) * vd])
        for p in range(n_pages):
            vh = v_refs[p][0, pl.ds(h, ps, stride=n_heads), :].astype(BF16)
            acc = acc + _nn(eh[:, p * ps:(p + 1) * ps], vh)
        acc = acc * inv[h * rows:(h + 1) * rows]
        o = acc[0:SAMPLE_PAD] - lam * acc[SAMPLE_PAD:rows]
        ms = jnp.mean(o * o, axis=-1, keepdims=True)
        outs.append(o * lax.rsqrt(ms + EPS) * (sg_ref[...] * out_scale))
    o_ref[...] = jnp.concatenate(outs, axis=1).astype(o_ref.dtype)


def _decode(page_table, lam, q, kn, vn, bias, sg, ck, cv, n_heads, hd, out_scale):
    n_seq, n_pages = page_table.shape
    _, aw, ps = ck.shape
    vd = aw // n_heads
    kern = functools.partial(_decode_kernel, n_pages=n_pages, n_heads=n_heads, hd=hd, out_scale=out_scale)
    row = lambda s, pt: (s, 0)
    page_idx = [functools.partial(lambda s, pt, p: (pt[s, p], 0, 0), p=p) for p in range(n_pages)]
    k_specs = [pl.BlockSpec((1, aw, ps), im) for im in page_idx]
    v_specs = [pl.BlockSpec((1,) + cv.shape[1:], im) for im in page_idx]
    grid_spec = pltpu.PrefetchScalarGridSpec(
        num_scalar_prefetch=1,
        grid=(n_seq,),
        in_specs=[pl.BlockSpec(memory_space=pltpu.SMEM),
                  pl.BlockSpec((SAMPLE_PAD, aw), row),
                  pl.BlockSpec((SAMPLE_PAD, aw), row),
                  pl.BlockSpec((SAMPLE_PAD, aw), row),
                  pl.BlockSpec(bias.shape, lambda s, pt: (0, 0, 0)),
                  pl.BlockSpec((1, vd), lambda s, pt: (0, 0))] + k_specs + v_specs,
        out_specs=pl.BlockSpec((SAMPLE_PAD, aw), row),
    )
    return pl.pallas_call(
        kern,
        grid_spec=grid_spec,
        out_shape=jax.ShapeDtypeStruct((n_seq * SAMPLE_PAD, aw), F32),
        compiler_params=pltpu.CompilerParams(dimension_semantics=("arbitrary",),
                                             vmem_limit_bytes=VMEM_LIMIT),
        name="attn_decode",
    )(page_table, lam, q, kn, vn, bias, sg, *([ck] * n_pages), *([cv] * n_pages))


def _mid_reference(b, level):
    c, w = b.shape
    p = 1 << level
    half = p // 2
    if p >= SUBLANES:
        bp = b.reshape(c // p, p, w)
        return jnp.broadcast_to(bp[:, half - 1:half, :], (c // p, p, w)).reshape(c, w)
    b8 = b.reshape(c // SUBLANES, SUBLANES, w)
    rig = lax.broadcasted_iota(jnp.int32, b8.shape, 1)
    out = None
    for start in range(SUBLANES - p, -1, -p):
        row = jnp.broadcast_to(b8[:, start + half - 1:start + half, :], b8.shape)
        out = row if out is None else jnp.where(rig < start + p, row, out)
    return out.reshape(c, w)


def _upper_lower(up, low, level):
    c, w = up.shape
    half = (1 << level) // 2
    if half >= SUBLANES:
        shape = (c // (2 * half), 2, half, w)
        return jnp.concatenate([low.reshape(shape)[:, 0:1], up.reshape(shape)[:, 1:2]], axis=1).reshape(c, w)
    rowid = lax.broadcasted_iota(jnp.int32, up.shape, 0)
    return jnp.where((rowid & half) != 0, up, low)


def _hgrn_kernel(r_ref, lb_ref, gn_ref, tri_ref, lev_ref, o_ref, sfin_ref, st_sc, *, n_heads, kd):
    t = pl.program_id(1)
    c = HG_CHUNK
    rw = n_heads * kd
    n_levels = c.bit_length() - 1

    @pl.when(t == 0)
    def _():
        st_sc[...] = jnp.zeros(st_sc.shape, F32)

    def chunk(ci, carry):
        r0 = pl.multiple_of(ci * c, c)
        lev = lev_ref[...]
        tri = tri_ref[...]
        for h in range(n_heads):
            cols = slice(h * kd, (h + 1) * kd)
            q = r_ref[0, pl.ds(r0, c), cols].astype(F32)
            rf = r_ref[0, pl.ds(r0, c), rw + h * kd:rw + (h + 1) * kd].astype(F32)
            v = r_ref[0, pl.ds(r0, c), 2 * rw + h * kd:2 * rw + (h + 1) * kd].astype(BF16)
            rg = r_ref[0, pl.ds(r0, c), 3 * rw + h * kd:3 * rw + (h + 1) * kd].astype(F32)
            lb = lb_ref[:, cols]
            f = lb + (1.0 - lb) * jax.nn.sigmoid(rf)
            logf = jnp.log(f) * LOG2E
            kk = 1.0 - f
            hi = logf.astype(BF16)
            lo = (logf - hi.astype(F32)).astype(BF16)
            b = _nn(tri, hi) + _nn(tri, lo)
            a = jnp.where(lev == 0, _nt(q.astype(BF16), kk.astype(BF16)), 0.0)
            for level in range(1, n_levels + 1):
                e = jnp.exp2(-jnp.abs(b - _mid_reference(b, level)))
                z = (_upper_lower(q, kk, level) * e).astype(BF16)
                a = jnp.where(lev == level, _nt(z, z), a)
            st = st_sc[h]
            o = _nn(a.astype(BF16), v) + _nt((q * jnp.exp2(b)).astype(BF16), st.astype(BF16))
            b_last = b[c - 1:c, :]
            k_hat = (kk * jnp.exp2(b_last - b)).astype(BF16)
            st_sc[h] = jnp.exp2(b_last) * st + _tn(v, k_hat)
            ms = jnp.mean(o * o, axis=-1, keepdims=True)
            on = o * lax.rsqrt(ms + EPS) * gn_ref[...] * (rg * jax.nn.sigmoid(rg))
            o_ref[0, pl.ds(r0, c), cols] = on.astype(o_ref.dtype)
        return carry

    lax.fori_loop(0, HG_BLOCK // c, chunk, 0)

    @pl.when(t == pl.num_programs(1) - 1)
    def _():
        for h in range(n_heads):
            sfin_ref[0, h] = st_sc[h].T


def _hgrn_prompt(r, lb, gn, n_heads, kd):
    b, t, _ = r.shape
    c = HG_CHUNK
    idx = np.arange(c)
    x = idx[:, None] ^ idx[None, :]
    lev = np.where(idx[:, None] > idx[None, :], np.floor(np.log2(np.maximum(x, 1))).astype(np.int32) + 1, -1)
    lev = np.where(idx[:, None] == idx[None, :], 0, lev).astype(np.int32)
    tri = (idx[:, None] >= idx[None, :]).astype(np.float32)
    kern = functools.partial(_hgrn_kernel, n_heads=n_heads, kd=kd)
    return pl.pallas_call(
        kern,
        grid=(b, t // HG_BLOCK),
        in_specs=[pl.BlockSpec((1, HG_BLOCK, r.shape[2]), lambda bi, ti: (bi, ti, 0)),
                  pl.BlockSpec((1, n_heads * kd), lambda bi, ti: (0, 0)),
                  pl.BlockSpec((1, kd), lambda bi, ti: (0, 0)),
                  pl.BlockSpec((c, c), lambda bi, ti: (0, 0)),
                  pl.BlockSpec((c, c), lambda bi, ti: (0, 0))],
        out_specs=[pl.BlockSpec((1, HG_BLOCK, n_heads * kd), lambda bi, ti: (bi, ti, 0)),
                   pl.BlockSpec((1, n_heads, kd, kd), lambda bi, ti: (bi, 0, 0, 0))],
        out_shape=[jax.ShapeDtypeStruct((b, t, n_heads * kd), BF16),
                   jax.ShapeDtypeStruct((b, n_heads, kd, kd), F32)],
        scratch_shapes=[pltpu.VMEM((n_heads, kd, kd), F32)],
        compiler_params=pltpu.CompilerParams(dimension_semantics=("arbitrary", "arbitrary"),
                                             vmem_limit_bytes=VMEM_LIMIT),
        name="hgrn_prompt",
    )(r, lb, gn, jnp.asarray(tri, BF16), jnp.asarray(lev))


def _hgrn_step_kernel(r_ref, lb_ref, gn_ref, s_ref, o_ref, so_ref, *, n_heads, kd, n_tok, n_seq):
    rw = n_heads * kd
    rowid = lax.broadcasted_iota(jnp.int32, (SAMPLE_PAD, kd), 0)
    live = rowid < n_tok
    zpad = jnp.zeros((SAMPLE_PAD, kd), F32)
    pad16 = lambda z: jnp.concatenate([z, zpad], axis=0).astype(BF16)
    ones16 = jnp.ones((2 * SAMPLE_PAD, kd), BF16)
    for g in range(n_seq):
        rows = slice(g * SAMPLE_PAD, (g + 1) * SAMPLE_PAD)
        for h in range(n_heads):
            cols = slice(h * kd, (h + 1) * kd)
            q = r_ref[rows, h * kd:(h + 1) * kd]
            rf = r_ref[rows, rw + h * kd:rw + (h + 1) * kd]
            v = r_ref[rows, 2 * rw + h * kd:2 * rw + (h + 1) * kd]
            rg = r_ref[rows, 3 * rw + h * kd:3 * rw + (h + 1) * kd]
            lb = lb_ref[:, cols]
            f = lb + (1.0 - lb) * jax.nn.sigmoid(rf)
            kk = jnp.where(live, 1.0 - f, 0.0)
            b = jnp.where(live, jnp.log(f) * LOG2E, 0.0)
            for sh in (1, 2, 4):
                b = b + jnp.where(rowid >= sh, pltpu.roll(b, sh, 0), 0.0)
            s0 = s_ref[g, h]
            o = jnp.sum(q * kk, axis=1, keepdims=True) * v
            for d in range(1, n_tok):
                x = q * pltpu.roll(kk, d, 0) * jnp.exp2(b - pltpu.roll(b, d, 0))
                w = jnp.sum(jnp.where(rowid >= d, x, 0.0), axis=1, keepdims=True)
                o = o + w * pltpu.roll(v, d, 0)
            o = o + _nn(pad16(q * jnp.exp2(b)), s0.astype(BF16))[0:SAMPLE_PAD]
            b_last = b[n_tok - 1:n_tok, :]
            decay = jnp.exp2(b_last)
            d_hi = decay.astype(BF16).astype(F32)
            d_split = jnp.where(rowid == 0, d_hi, jnp.where(rowid == 1, decay - d_hi, 0.0))
            decay_cols = _tn(pad16(d_split), ones16)
            so_ref[g, h] = decay_cols * s0 + _tn(pad16(kk * jnp.exp2(b_last - b)), pad16(v))
            ms = jnp.mean(o * o, axis=-1, keepdims=True)
            on = o * lax.rsqrt(ms + EPS) * gn_ref[...] * (rg * jax.nn.sigmoid(rg))
            o_ref[rows, cols] = on.astype(o_ref.dtype)


def _hgrn_sample(r, lb, gn, state, n_heads, kd, n_tok, seq_per_step=2):
    n_seq = state.shape[0]
    kern = functools.partial(_hgrn_step_kernel, n_heads=n_heads, kd=kd, n_tok=n_tok, n_seq=seq_per_step)
    rows = seq_per_step * SAMPLE_PAD
    return pl.pallas_call(
        kern,
        grid=(n_seq // seq_per_step,),
        in_specs=[pl.BlockSpec((rows, r.shape[1]), lambda i: (i, 0)),
                  pl.BlockSpec((1, n_heads * kd), lambda i: (0, 0)),
                  pl.BlockSpec((1, kd), lambda i: (0, 0)),
                  pl.BlockSpec((seq_per_step, n_heads, kd, kd), lambda i: (i, 0, 0, 0))],
        out_specs=[pl.BlockSpec((rows, n_heads * kd), lambda i: (i, 0)),
                   pl.BlockSpec((seq_per_step, n_heads, kd, kd), lambda i: (i, 0, 0, 0))],
        out_shape=[jax.ShapeDtypeStruct((n_seq * SAMPLE_PAD, n_heads * kd), F32),
                   jax.ShapeDtypeStruct(state.shape, F32)],
        compiler_params=pltpu.CompilerParams(dimension_semantics=("arbitrary",),
                                             vmem_limit_bytes=VMEM_LIMIT),
        name="hgrn_sample",
    )(r, lb, gn, state)


def _ffn_kernel(*refs, sample, aw):
    if sample:
        (x_ref, oa_ref, or_ref, wo_ref, g2_ref, wg_ref, wu_ref, cw_ref, cb_ref, wd_ref, prev_ref,
         y_ref, a_ref) = refs
    else:
        (x_ref, oa_ref, or_ref, wo_ref, g2_ref, wg_ref, wu_ref, cw_ref, cb_ref, wd_ref,
         y_ref, a_ref, carry_sc) = refs

        @pl.when(pl.program_id(1) == 0)
        def _():
            carry_sc[...] = jnp.zeros(carry_sc.shape, F32)

    x = x_ref[...].astype(F32)
    x1 = x + _nn(oa_ref[...].astype(BF16), wo_ref[0:aw, :]) + _nn(or_ref[...].astype(BF16), wo_ref[aw:, :])
    ms = jnp.mean(x1 * x1, axis=-1, keepdims=True)
    h2 = (x1 * lax.rsqrt(ms + EPS) * g2_ref[...]).astype(BF16)
    tm = x.shape[0]
    a = _nn(h2, wg_ref[...])
    u = _nn(h2, wu_ref[...])
    rowid = lax.broadcasted_iota(jnp.int32, a.shape, 0)
    am1 = pltpu.roll(a, 1, 0)
    am2 = pltpu.roll(a, 2, 0)
    if sample:
        rig = rowid & (SAMPLE_PAD - 1)
        prev = prev_ref[...]
        am1 = jnp.where(rig == 0, pltpu.roll(prev, tm - 1, 0), am1)
        am2 = jnp.where(rig < 2, prev, am2)
        a_ref[...] = a
    else:
        last = carry_sc[...]
        am1 = jnp.where(rowid == 0, last[SUBLANES - 1:SUBLANES, :], am1)
        am2 = jnp.where(rowid == 0, last[SUBLANES - 2:SUBLANES - 1, :],
                        jnp.where(rowid == 1, last[SUBLANES - 1:SUBLANES, :], am2))
        tail = a[tm - SUBLANES:tm, :]
        carry_sc[...] = tail
        a_ref[0] = tail
    cw = cw_ref[...]
    cv = cb_ref[...] + cw[0:1, :] * am2 + cw[1:2, :] * am1 + cw[2:3, :] * a
    gate = (cv * jax.nn.sigmoid(cv) * u).astype(BF16)
    y_ref[...] = x1 + _nn(gate, wd_ref[...])


def _ffn(x2, oa, orec, wo, g2, wg, wu, cw, cb, wd, prev, tm, n_batch):
    n, d = x2.shape
    aw = oa.shape[1]
    dff = wg.shape[1]
    sample = prev is not None
    kern = functools.partial(_ffn_kernel, sample=sample, aw=aw)
    scratch = []
    if sample:
        grid = (n // tm,)
        row = lambda i: (i, 0)
        c2 = lambda i: (0, 0)
        extra_in = [pl.BlockSpec((tm, dff), row)]
        extra_args = [prev]
        a_shape = jax.ShapeDtypeStruct((n, dff), F32)
        a_spec = pl.BlockSpec((tm, dff), row)
        sem = ("arbitrary",)
    else:
        per_b = n // n_batch // tm
        grid = (n_batch, per_b)
        row = lambda b, i: (b * per_b + i, 0)
        c2 = lambda b, i: (0, 0)
        extra_in = []
        extra_args = []
        a_shape = jax.ShapeDtypeStruct((n_batch, SUBLANES, dff), F32)
        a_spec = pl.BlockSpec((1, SUBLANES, dff), lambda b, i: (b, 0, 0))
        scratch.append(pltpu.VMEM((SUBLANES, dff), F32))
        sem = ("arbitrary", "arbitrary")
    return pl.pallas_call(
        kern,
        grid=grid,
        in_specs=[pl.BlockSpec((tm, d), row),
                  pl.BlockSpec((tm, aw), row),
                  pl.BlockSpec((tm, aw), row),
                  _const_spec(wo.shape, c2),
                  pl.BlockSpec((1, d), c2),
                  _const_spec(wg.shape, c2),
                  _const_spec(wu.shape, c2),
                  pl.BlockSpec(cw.shape, c2),
                  pl.BlockSpec(cb.shape, c2),
                  _const_spec(wd.shape, c2)] + extra_in,
        out_specs=[pl.BlockSpec((tm, d), row), a_spec],
        out_shape=[jax.ShapeDtypeStruct((n, d), F32), a_shape],
        scratch_shapes=scratch,
        compiler_params=pltpu.CompilerParams(dimension_semantics=sem, vmem_limit_bytes=VMEM_LIMIT),
        name="ffn_sample" if sample else "ffn_prompt",
    )(x2, oa, orec, wo, g2, wg, wu, cw, cb, wd, *extra_args)


def kernel(x_prompt, x_sample, cache_k, cache_v, page_table, state_hgrn, state_conv, rel_bias, norm1_g, w_in,
           qk_norm_g, lambda_qk, subln_g, hgrn_lb, hgrn_onorm_g, w_out, norm2_g, w_gate, w_up, conv_w, conv_b,
           w_down):
    bp, tp, d = x_prompt.shape
    bs, ts, _ = x_sample.shape
    depth = w_in.shape[0]
    assert depth == 1 and CONV_W - 1 <= ts <= SAMPLE_PAD and conv_w.shape[1] == CONV_W
    n_heads, _, hd = cache_k.shape[3:]
    vd = cache_v.shape[4]
    aw = n_heads * vd
    r_heads, kd = state_hgrn.shape[2], state_hgrn.shape[3]
    rw = r_heads * kd
    dff = w_gate.shape[2]
    page_size = cache_k.shape[2]
    assert page_size >= MAX_DISTANCE and tp % TQ == 0 and tp % HG_BLOCK == 0
    l = 0
    lam_init = 0.8 - 0.6 * math.exp(-0.3 * l)
    out_scale = 1.0 - lam_init

    biasp, biass, lam, lb = _tables(rel_bias, lambda_qk[l], hgrn_lb, n_heads, ts, page_size, lam_init)

    w_in_bf = w_in[l].astype(BF16)
    wo_bf = w_out[l].astype(BF16)
    wg_bf = w_gate[l].astype(BF16)
    wu_bf = w_up[l].astype(BF16)
    wd_bf = w_down[l].astype(BF16)
    cw = conv_w[l]
    cb = conv_b[l].reshape(1, dff)
    g1 = norm1_g[l].reshape(1, d)
    g2 = norm2_g[l].reshape(1, d)
    qg = jnp.tile(qk_norm_g[l, 0].reshape(1, 2 * hd), (1, n_heads))
    kg = jnp.tile(qk_norm_g[l, 1].reshape(1, 2 * hd), (1, n_heads))
    seg = np.arange(aw) // hd
    bd = jnp.asarray((seg[:, None] == seg[None, :]).astype(np.float32) / hd, BF16)
    sg = subln_g[l].reshape(1, vd)
    gn = hgrn_onorm_g[l].reshape(1, kd)
    q_scale = hd ** -0.5 * LOG2E

    xp2 = x_prompt.reshape(bp * tp, d)
    qb, kft, vf, kb, vbt, rp = _proj(xp2, g1, w_in_bf, qg, kg, bd, aw, rw, vd, q_scale, True, BF16, BF16, 512, bp)
    o_att = _attn(lam, qb.reshape(bp, tp, aw), kb.reshape(bp, tp, aw), vbt, biasp, sg, n_heads, hd, out_scale)
    o_rec, s_p = _hgrn_prompt(rp.reshape(bp, tp, 4 * rw), lb, gn, r_heads, kd)
    yp, a_tail = _ffn(xp2, o_att.reshape(bp * tp, aw), o_rec.reshape(bp * tp, rw), wo_bf, g2, wg_bf, wu_bf, cw, cb,
                      wd_bf, None, 512, bp)
    k_prompt = kft.reshape(bp, n_heads, 2, hd, tp).transpose(0, 4, 1, 2, 3)[None]
    v_prompt = vf.reshape(1, bp, tp, n_heads, vd)
    conv_prompt = a_tail[:, SUBLANES - (CONV_W - 1):][None]

    xs_pad = jnp.pad(x_sample, ((0, 0), (0, SAMPLE_PAD - ts), (0, 0))).reshape(bs * SAMPLE_PAD, d)
    qs, ksf, vsf, rs = _proj(xs_pad, g1, w_in_bf, qg, kg, bd, aw, rw, vd, q_scale, False, F32, F32, 512, bs)
    ck = jnp.transpose(cache_k[l], (0, 2, 3, 4, 1)).reshape(cache_k.shape[1], aw, page_size)
    cv = cache_v[l].reshape(cache_v.shape[1], page_size * n_heads, vd)
    o_att_s = _decode(page_table, lam, qs, ksf, vsf, biass, sg, ck, cv, n_heads, hd, out_scale)
    o_rec_s, s_s = _hgrn_sample(rs, lb, gn, state_hgrn[l], r_heads, kd, ts)
    prev = jnp.pad(state_conv[l], ((0, 0), (0, SAMPLE_PAD - (CONV_W - 1)), (0, 0))).reshape(bs * SAMPLE_PAD, dff)
    ys, a_s = _ffn(xs_pad, o_att_s, o_rec_s, wo_bf, g2, wg_bf, wu_bf, cw, cb, wd_bf, prev, 256, 1)
    unpad = lambda z: z.reshape(bs, SAMPLE_PAD, -1)[:, :ts]
    y_sample = unpad(ys)
    k_sample = unpad(ksf).reshape(1, bs, ts, n_heads, 2, hd)
    v_sample = unpad(vsf).reshape(1, bs, ts, n_heads, vd)
    conv_sample = a_s.reshape(bs, SAMPLE_PAD, dff)[:, ts - (CONV_W - 1):ts][None]

    return (yp.reshape(bp, tp, d), y_sample, k_prompt, v_prompt, s_p[None], conv_prompt,
            k_sample, v_sample, s_s[None], conv_sample)
```

```python
import functools
import math

import numpy as np
import jax
import jax.numpy as jnp
from jax import lax
from jax.experimental import pallas as pl
from jax.experimental.pallas import tpu as pltpu

F32 = jnp.float32
BF16 = jnp.bfloat16

EPS = 1e-6
N_BUCKETS = 32
MAX_DISTANCE = 128
CONV_W = 3
NEG = -1e30
LOG2E = math.log2(math.e)
LANES = 128
SUBLANES = 8

TQ = 512
HG_CHUNK = 128
MIX_SEQS = 2
SAMPLE_PAD = 8
VMEM_LIMIT = 56 * 1024 * 1024


def _const_spec(shape, index_map):
    return pl.BlockSpec(shape, index_map, pipeline_mode=pl.Buffered(1))


def _nt(a, b):
    return lax.dot_general(a, b, (((1,), (1,)), ((), ())), preferred_element_type=F32)


def _tn(a, b):
    return lax.dot_general(a, b, (((0,), (0,)), ((), ())), preferred_element_type=F32)


def _nn(a, b):
    return jnp.dot(a, b, preferred_element_type=F32)


def _rel_bucket_np(n):
    n = np.maximum(n, 0)
    max_exact = N_BUCKETS // 2
    nf = np.maximum(n, 1).astype(np.float32)
    large = max_exact + (np.log(nf / np.float32(max_exact)) / np.float32(math.log(MAX_DISTANCE / max_exact))
                         * np.float32(N_BUCKETS - max_exact)).astype(np.int32)
    large = np.minimum(large, N_BUCKETS - 1)
    return np.where(n < max_exact, n, large).astype(np.int32)


def _tables_kernel(relb_ref, lq_ref, hlb_ref, bkp_ref, bks_ref,
                   biasp_ref, biass_ref, lam_ref, lb_ref, *, n_heads, lam_init, near_blocks):
    for h in range(n_heads):
        far = relb_ref[N_BUCKETS - 1, h]
        for kind in range(2):
            biasp_ref[h, kind] = jnp.where(bkp_ref[kind] < 0, NEG, 0.0).astype(F32)
            for (r0, c0) in near_blocks[kind]:
                bk = bkp_ref[kind, r0:r0 + LANES, c0:c0 + LANES]
                acc = jnp.where(bk < 0, NEG, 0.0).astype(F32)
                for b in range(N_BUCKETS - 1):
                    acc = jnp.where(bk == b, (relb_ref[b, h] - far) * LOG2E, acc)
                biasp_ref[h, kind, r0:r0 + LANES, c0:c0 + LANES] = acc
            rows = bks_ref.shape[1] // n_heads
            bk = bks_ref[kind, h * rows:(h + 1) * rows, :]
            acc = jnp.where(bk < 0, NEG, 0.0).astype(F32)
            for b in range(N_BUCKETS - 1):
                acc = jnp.where(bk == b, (relb_ref[b, h] - far) * LOG2E, acc)
            biass_ref[kind, h * rows:(h + 1) * rows, :] = acc
    lq = lq_ref[...].astype(F32)
    s1 = jnp.sum(lq[0:1] * lq[1:2], axis=1, keepdims=True)
    s2 = jnp.sum(lq[2:3] * lq[3:4], axis=1, keepdims=True)
    lam = jnp.exp(s1) - jnp.exp(s2) + lam_init
    lam_ref[...] = jnp.broadcast_to(lam, lam_ref.shape)
    hl = hlb_ref[...].astype(F32)
    mx = jnp.max(hl, axis=0, keepdims=True)
    e = jnp.exp(hl - mx)
    lb_ref[...] = e[0:1] / jnp.sum(e, axis=0, keepdims=True)


def _tables(rel_bias, lambda_qk_l, hgrn_lb, n_heads, n_tok, page_size, lam_init):
    c = np.arange(TQ)[:, None]
    r = np.arange(TQ)[None, :]
    diag = np.where(c <= r, _rel_bucket_np(r - c), -1)
    prev = _rel_bucket_np(TQ + r - c)
    bkp_np = np.stack([diag, prev]).astype(np.int32)
    bkp = jnp.asarray(bkp_np)
    near_blocks = tuple(
        tuple((r0, c0) for r0 in range(0, TQ, LANES) for c0 in range(0, TQ, LANES)
              if np.any((bkp_np[kind, r0:r0 + LANES, c0:c0 + LANES] >= 0)
                        & (bkp_np[kind, r0:r0 + LANES, c0:c0 + LANES] < N_BUCKETS - 1)))
        for kind in range(2))
    t = np.tile(np.arange(SAMPLE_PAD), 2 * n_heads)[:, None]
    cc = np.arange(page_size)[None, :]
    last_page = _rel_bucket_np(page_size + t - cc)
    new_page = np.where(cc <= np.minimum(t, n_tok - 1), _rel_bucket_np(t - cc), -1)
    bks = jnp.asarray(np.stack([last_page, new_page]).astype(np.int32))
    rows_s = 2 * n_heads * SAMPLE_PAD
    kern = functools.partial(_tables_kernel, n_heads=n_heads, lam_init=lam_init, near_blocks=near_blocks)
    return pl.pallas_call(
        kern,
        out_shape=(jax.ShapeDtypeStruct((n_heads, 2, TQ, TQ), F32),
                   jax.ShapeDtypeStruct((2, rows_s, page_size), F32),
                   jax.ShapeDtypeStruct((SUBLANES, LANES), F32),
                   jax.ShapeDtypeStruct((1, hgrn_lb.shape[1]), F32)),
        in_specs=[pl.BlockSpec(memory_space=pltpu.SMEM),
                  pl.BlockSpec(memory_space=pltpu.VMEM),
                  pl.BlockSpec(memory_space=pltpu.VMEM),
                  pl.BlockSpec(memory_space=pltpu.VMEM),
                  pl.BlockSpec(memory_space=pltpu.VMEM)],
        name="tables",
    )(rel_bias, lambda_qk_l, hgrn_lb, bkp, bks)


def _proj_kernel(x_ref, g1_ref, w_ref, qg_ref, kg_ref, bd_ref, *out_refs, aw, rw, vd, q_scale, emit_bf16):
    if emit_bf16:
        q_ref, kf_ref, vf_ref, kb_ref, vb_ref, r_ref = out_refs
    else:
        q_ref, kf_ref, vf_ref, r_ref = out_refs
    x = x_ref[...].astype(F32)
    ms = jnp.mean(x * x, axis=-1, keepdims=True)
    h = (x * lax.rsqrt(ms + EPS) * g1_ref[...]).astype(BF16)

    def seg_norm(z, g):
        msq = _nn((z * z).astype(BF16), bd_ref[...])
        return z * lax.rsqrt(msq + EPS) * g

    zq = _nn(h, w_ref[:, 0:aw])
    q_ref[...] = (seg_norm(zq, qg_ref[...]) * q_scale).astype(q_ref.dtype)
    zk = _nn(h, w_ref[:, aw:2 * aw])
    kn = seg_norm(zk, kg_ref[...])
    zv = _nn(h, w_ref[:, 2 * aw:3 * aw])
    if emit_bf16:
        n_heads = aw // vd
        kf_ref[0] = kn.T
        kb_ref[...] = kn.astype(BF16)
        vb_ref[0] = zv.T.astype(BF16)
        for hh in range(n_heads):
            vf_ref[pl.ds(hh, zv.shape[0], stride=n_heads), :] = zv[:, hh * vd:(hh + 1) * vd]
    else:
        kf_ref[...] = kn
        vf_ref[...] = zv
    for g in range(4):
        z = _nn(h, w_ref[:, 3 * aw + g * rw:3 * aw + (g + 1) * rw])
        r_ref[:, g * rw:(g + 1) * rw] = z.astype(r_ref.dtype)


def _proj(x2, g1, w_bf, qg, kg, bd, aw, rw, vd, q_scale, emit_bf16, q_dtype, r_dtype, tm, n_batch):
    n, d = x2.shape
    kern = functools.partial(_proj_kernel, aw=aw, rw=rw, vd=vd, q_scale=q_scale, emit_bf16=emit_bf16)
    row = lambda i: (i, 0)
    const = lambda i: (0, 0)
    if emit_bf16:
        t = n // n_batch
        per_b = t // tm
        n_heads = aw // vd
        xposed = lambda i: (i // per_b, 0, i % per_b)
        out_shape = [jax.ShapeDtypeStruct((n, aw), q_dtype),
                     jax.ShapeDtypeStruct((n_batch, aw, t), F32),
                     jax.ShapeDtypeStruct((n * n_heads, vd), F32),
                     jax.ShapeDtypeStruct((n, aw), BF16),
                     jax.ShapeDtypeStruct((n_batch, aw, t), BF16)]
        out_specs = [pl.BlockSpec((tm, aw), row),
                     pl.BlockSpec((1, aw, tm), xposed),
                     pl.BlockSpec((tm * n_heads, vd), row),
                     pl.BlockSpec((tm, aw), row),
                     pl.BlockSpec((1, aw, tm), xposed)]
    else:
        out_shape = [jax.ShapeDtypeStruct((n, aw), q_dtype),
                     jax.ShapeDtypeStruct((n, aw), F32),
                     jax.ShapeDtypeStruct((n, aw), F32)]
        out_specs = [pl.BlockSpec((tm, aw), row)] * 3
    out_shape.append(jax.ShapeDtypeStruct((n, 4 * rw), r_dtype))
    out_specs.append(pl.BlockSpec((tm, 4 * rw), row))
    return pl.pallas_call(
        kern,
        grid=(n // tm,),
        in_specs=[pl.BlockSpec((tm, d), row),
                  pl.BlockSpec((1, d), const),
                  _const_spec(w_bf.shape, const),
                  pl.BlockSpec((1, aw), const),
                  pl.BlockSpec((1, aw), const),
                  pl.BlockSpec((aw, aw), const)],
        out_specs=out_specs,
        out_shape=out_shape,
        compiler_params=pltpu.CompilerParams(dimension_semantics=("arbitrary",),
                                             vmem_limit_bytes=VMEM_LIMIT),
        name="proj_bf16" if emit_bf16 else "proj_f32",
    )(x2, g1, w_bf, qg, kg, bd)


def _attn_kernel(lam_ref, q_ref, k_ref, vt_ref, bias_ref, sg_ref, o_ref,
                 qst_sc, sa_sc, ma_sc, sb_sc, mb_sc, m_sc, l_sc, acc_sc, *, hd, out_scale):
    i = pl.program_id(2)

    qt = q_ref[0].astype(F32).T
    row = lax.broadcasted_iota(jnp.int32, qt.shape, 0)
    qst_sc[...] = jnp.concatenate([jnp.where(row < hd, qt, 0.0), jnp.where(row >= hd, qt, 0.0)],
                                  axis=1).astype(BF16)
    m_sc[...] = jnp.full(m_sc.shape, NEG, F32)
    l_sc[...] = jnp.zeros(l_sc.shape, F32)
    acc_sc[...] = jnp.zeros(acc_sc.shape, F32)

    bufs = ((sa_sc, ma_sc), (sb_sc, mb_sc))

    def issue(buf, g, with_max):
        start = pl.multiple_of(g * TQ, TQ)
        s = _nn(k_ref[0, pl.ds(start, TQ), :], qst_sc[...])
        buf[0][...] = s
        if with_max:
            buf[1][...] = jnp.max(s, axis=0, keepdims=True)

    def consume(buf, g, kind):
        s = buf[0][...]
        if kind is None:
            s_max = buf[1][...]
        else:
            bias = bias_ref[0, kind]
            s = jnp.concatenate([s[:, 0:TQ] + bias, s[:, TQ:2 * TQ] + bias], axis=1)
            s_max = jnp.max(s, axis=0, keepdims=True)
        m_old = m_sc[...]
        m_new = jnp.maximum(m_old, s_max)
        alpha = jnp.exp2(m_old - m_new)
        p = jnp.exp2(s - m_new)
        l_sc[...] = alpha * l_sc[...] + jnp.sum(p, axis=0, keepdims=True)
        start = pl.multiple_of(g * TQ, TQ)
        acc_sc[...] = alpha * acc_sc[...] + _nn(vt_ref[0, :, pl.ds(start, TQ)], p.astype(BF16))
        m_sc[...] = m_new

    n_pure = jnp.maximum(i - 1, 0)
    issue(bufs[0], 0, True)

    def pair(jj, carry):
        g = 2 * jj
        issue(bufs[1], g + 1, True)
        consume(bufs[0], g, None)
        issue(bufs[0], g + 2, True)
        consume(bufs[1], g + 1, None)
        return carry

    lax.fori_loop(0, n_pure // 2, pair, 0)
    odd = (n_pure & 1) == 1

    def finish(cur, other):
        @pl.when(i > 0)
        def _():
            issue(other, i, False)
            consume(cur, i - 1, 1)
            consume(other, i, 0)

        @pl.when(i == 0)
        def _():
            consume(cur, 0, 0)

    @pl.when(odd)
    def _():
        issue(bufs[1], n_pure, False)
        consume(bufs[0], n_pure - 1, None)
        finish(bufs[1], bufs[0])

    @pl.when(jnp.logical_not(odd))
    def _():
        finish(bufs[0], bufs[1])

    a = acc_sc[...] * (1.0 / l_sc[...])
    ot = a[:, 0:TQ] - lam_ref[0, 0] * a[:, TQ:2 * TQ]
    ms = jnp.mean(ot * ot, axis=0, keepdims=True)
    o = (ot * lax.rsqrt(ms + EPS)).T
    o_ref[0] = (o * (sg_ref[...] * out_scale)).astype(o_ref.dtype)


def _attn(lam, q, k, vt, bias, sg, n_heads, hd, out_scale):
    b, t, aw = q.shape
    vd = aw // n_heads
    assert t % TQ == 0
    kern = functools.partial(_attn_kernel, hd=hd, out_scale=out_scale)
    return pl.pallas_call(
        kern,
        grid=(b, n_heads, t // TQ),
        in_specs=[pl.BlockSpec(memory_space=pltpu.SMEM),
                  pl.BlockSpec((1, TQ, vd), lambda bi, h, i: (bi, i, h)),
                  pl.BlockSpec((1, t, vd), lambda bi, h, i: (bi, 0, h)),
                  pl.BlockSpec((1, vd, t), lambda bi, h, i: (bi, h, 0)),
                  pl.BlockSpec((1, 2, TQ, TQ), lambda bi, h, i: (h, 0, 0, 0)),
                  pl.BlockSpec((1, vd), lambda bi, h, i: (0, 0))],
        out_specs=pl.BlockSpec((1, TQ, vd), lambda bi, h, i: (bi, i, h)),
        out_shape=jax.ShapeDtypeStruct((b, t, aw), BF16),
        scratch_shapes=[pltpu.VMEM((vd, 2 * TQ), BF16),
                        pltpu.VMEM((TQ, 2 * TQ), F32),
                        pltpu.VMEM((1, 2 * TQ), F32),
                        pltpu.VMEM((TQ, 2 * TQ), F32),
                        pltpu.VMEM((1, 2 * TQ), F32),
                        pltpu.VMEM((1, 2 * TQ), F32),
                        pltpu.VMEM((1, 2 * TQ), F32),
                        pltpu.VMEM((vd, 2 * TQ), F32)],
        compiler_params=pltpu.CompilerParams(dimension_semantics=("arbitrary",) * 3,
                                             vmem_limit_bytes=VMEM_LIMIT),
        name="attn_prompt",
    )(lam, q, k, vt, bias, sg)


def _decode_seq(lam, q8, kn8, vn8, bias_ref, sg, k_refs, v_refs, n_heads, hd, out_scale):
    n_pages = len(k_refs)
    aw = q8.shape[1]
    vd = aw // n_heads
    ps = k_refs[0].shape[2]
    lane = lax.broadcasted_iota(jnp.int32, q8.shape, 1)
    pieces = [jnp.where((lane >= hm * hd) & (lane < (hm + 1) * hd), q8, 0.0) for hm in range(2 * n_heads)]
    qbd = jnp.concatenate(pieces, axis=0).astype(BF16)
    s_list = []
    for p in range(n_pages):
        s = _nn(qbd, k_refs[p][0].astype(BF16))
        if p == n_pages - 1:
            s = s + bias_ref[0]
        s_list.append(s)
    pad = jnp.zeros((ps - SAMPLE_PAD, aw), F32)
    k_new = jnp.concatenate([kn8, pad], axis=0).astype(BF16)
    v_new = jnp.concatenate([vn8, pad], axis=0).astype(BF16)
    s_list.append(_nt(qbd, k_new) + bias_ref[1])
    s_all = jnp.concatenate(s_list, axis=1)
    m = jnp.max(s_all, axis=1, keepdims=True)
    e = jnp.exp2(s_all - m)
    inv = 1.0 / jnp.sum(e, axis=1, keepdims=True)
    eb = e.astype(BF16)
    rows = 2 * SAMPLE_PAD
    outs = []
    for h in range(n_heads):
        eh = eb[h * rows:(h + 1) * rows]
        acc = _nn(eh[:, n_pages * ps:(n_pages + 1) * ps], v_new[:, h * vd:(h + 1) * vd])
        for p in range(n_pages):
            vh = v_refs[p][0, pl.ds(h, ps, stride=n_heads), :].astype(BF16)
            acc = acc + _nn(eh[:, p * ps:(p + 1) * ps], vh)
        acc = acc * inv[h * rows:(h + 1) * rows]
        o = acc[0:SAMPLE_PAD] - lam * acc[SAMPLE_PAD:rows]
        ms = jnp.mean(o * o, axis=-1, keepdims=True)
        outs.append(o * lax.rsqrt(ms + EPS) * (sg * out_scale))
    return jnp.concatenate(outs, axis=1)


def _mid_reference(b, level):
    c, w = b.shape
    p = 1 << level
    half = p // 2
    if p >= SUBLANES:
        bp = b.reshape(c // p, p, w)
        return jnp.broadcast_to(bp[:, half - 1:half, :], (c // p, p, w)).reshape(c, w)
    b8 = b.reshape(c // SUBLANES, SUBLANES, w)
    rig = lax.broadcasted_iota(jnp.int32, b8.shape, 1)
    out = None
    for start in range(SUBLANES - p, -1, -p):
        row = jnp.broadcast_to(b8[:, start + half - 1:start + half, :], b8.shape)
        out = row if out is None else jnp.where(rig < start + p, row, out)
    return out.reshape(c, w)


def _upper_lower(up, low, level):
    c, w = up.shape
    half = (1 << level) // 2
    if half >= SUBLANES:
        shape = (c // (2 * half), 2, half, w)
        return jnp.concatenate([low.reshape(shape)[:, 0:1], up.reshape(shape)[:, 1:2]], axis=1).reshape(c, w)
    rowid = lax.broadcasted_iota(jnp.int32, up.shape, 0)
    return jnp.where((rowid & half) != 0, up, low)


def _hgrn_chunk(r_ref, r0, lb_ref, gn_ref, tri, lev, o_ref, st_sc, n_heads, kd):
    c = HG_CHUNK
    rw = n_heads * kd
    n_levels = c.bit_length() - 1
    for h in range(n_heads):
        cols = slice(h * kd, (h + 1) * kd)
        q = r_ref[0, r0:r0 + c, cols].astype(F32)
        rf = r_ref[0, r0:r0 + c, rw + h * kd:rw + (h + 1) * kd].astype(F32)
        v = r_ref[0, r0:r0 + c, 2 * rw + h * kd:2 * rw + (h + 1) * kd].astype(BF16)
        rg = r_ref[0, r0:r0 + c, 3 * rw + h * kd:3 * rw + (h + 1) * kd].astype(F32)
        lb = lb_ref[:, cols]
        f = lb + (1.0 - lb) * jax.nn.sigmoid(rf)
        logf = jnp.log(f) * LOG2E
        kk = 1.0 - f
        hi = logf.astype(BF16)
        lo = (logf - hi.astype(F32)).astype(BF16)
        b = _nn(tri, hi) + _nn(tri, lo)
        a = jnp.where(lev == 0, _nt(q.astype(BF16), kk.astype(BF16)), 0.0)
        for level in range(1, n_levels + 1):
            e = jnp.exp2(-jnp.abs(b - _mid_reference(b, level)))
            z = (_upper_lower(q, kk, level) * e).astype(BF16)
            a = jnp.where(lev == level, _nt(z, z), a)
        st = st_sc[h]
        o = _nn(a.astype(BF16), v) + _nt((q * jnp.exp2(b)).astype(BF16), st.astype(BF16))
        b_last = b[c - 1:c, :]
        k_hat = (kk * jnp.exp2(b_last - b)).astype(BF16)
        st_sc[h] = jnp.exp2(b_last) * st + _tn(v, k_hat)
        ms = jnp.mean(o * o, axis=-1, keepdims=True)
        on = o * lax.rsqrt(ms + EPS) * gn_ref[...] * (rg * jax.nn.sigmoid(rg))
        o_ref[0, r0:r0 + c, cols] = on.astype(o_ref.dtype)


def _mix_kernel(pt_ref, lam_ref, r_ref, lb_ref, gn_ref, tri_ref, lev_ref, q_ref, kn_ref, vn_ref, bias_ref, sg_ref,
                *refs, n_pages, n_heads, hd, r_heads, kd, out_scale, steps_per_batch):
    del pt_ref
    n_k = MIX_SEQS * n_pages
    k_refs, v_refs = refs[:n_k], refs[n_k:2 * n_k]
    o_rec_ref, sfin_ref, o_dec_ref, st_sc = refs[2 * n_k:]
    t = lax.rem(pl.program_id(0), steps_per_batch)

    @pl.when(t == 0)
    def _():
        st_sc[...] = jnp.zeros(st_sc.shape, F32)

    lev = lev_ref[...]
    tri = tri_ref[...]
    lam = lam_ref[0, 0]
    for c in range(MIX_SEQS):
        _hgrn_chunk(r_ref, c * HG_CHUNK, lb_ref, gn_ref, tri, lev, o_rec_ref, st_sc, r_heads, kd)
        rows = slice(c * SAMPLE_PAD, (c + 1) * SAMPLE_PAD)
        o_dec_ref[rows, :] = _decode_seq(
            lam, q_ref[rows, :].astype(F32), kn_ref[rows, :].astype(F32), vn_ref[rows, :].astype(F32), bias_ref,
            sg_ref[...], k_refs[c * n_pages:(c + 1) * n_pages], v_refs[c * n_pages:(c + 1) * n_pages],
            n_heads, hd, out_scale).astype(o_dec_ref.dtype)

    @pl.when(t == steps_per_batch - 1)
    def _():
        for h in range(r_heads):
            sfin_ref[0, h] = st_sc[h].T


def _mix(page_table, lam, r, lb, gn, q, kn, vn, bias, sg, ck, cv, n_heads, hd, r_heads, kd, out_scale):
    n_seq, n_pages = page_table.shape
    b, t, _ = r.shape
    _, aw, ps = ck.shape
    vd = aw // n_heads
    c = HG_CHUNK
    tokens = MIX_SEQS * c
    steps_per_batch = t // tokens
    n_steps = b * steps_per_batch
    assert t % tokens == 0 and n_seq == n_steps * MIX_SEQS
    idx = np.arange(c)
    x = idx[:, None] ^ idx[None, :]
    lev = np.where(idx[:, None] > idx[None, :], np.floor(np.log2(np.maximum(x, 1))).astype(np.int32) + 1, -1)
    lev = np.where(idx[:, None] == idx[None, :], 0, lev).astype(np.int32)
    tri = (idx[:, None] >= idx[None, :]).astype(np.float32)
    kern = functools.partial(_mix_kernel, n_pages=n_pages, n_heads=n_heads, hd=hd, r_heads=r_heads, kd=kd,
                             out_scale=out_scale, steps_per_batch=steps_per_batch)
    c2 = lambda s, pt: (0, 0)
    tok = lambda s, pt: (s // steps_per_batch, s % steps_per_batch, 0)
    seq_rows = lambda s, pt: (s, 0)
    page_idx = [functools.partial(lambda s, pt, j, p: (pt[s * MIX_SEQS + j, p], 0, 0), j=j, p=p)
                for j in range(MIX_SEQS) for p in range(n_pages)]
    k_specs = [pl.BlockSpec((1, aw, ps), im) for im in page_idx]
    v_specs = [pl.BlockSpec((1,) + cv.shape[1:], im) for im in page_idx]
    rows = MIX_SEQS * SAMPLE_PAD
    grid_spec = pltpu.PrefetchScalarGridSpec(
        num_scalar_prefetch=1,
        grid=(n_steps,),
        in_specs=[pl.BlockSpec(memory_space=pltpu.SMEM),
                  pl.BlockSpec((1, tokens, r.shape[2]), tok),
                  pl.BlockSpec((1, r_heads * kd), c2),
                  pl.BlockSpec((1, kd), c2),
                  pl.BlockSpec((c, c), c2),
                  pl.BlockSpec((c, c), c2),
                  pl.BlockSpec((rows, aw), seq_rows),
                  pl.BlockSpec((rows, aw), seq_rows),
                  pl.BlockSpec((rows, aw), seq_rows),
                  pl.BlockSpec(bias.shape, lambda s, pt: (0, 0, 0)),
                  pl.BlockSpec((1, vd), c2)] + k_specs + v_specs,
        out_specs=[pl.BlockSpec((1, tokens, r_heads * kd), tok),
                   pl.BlockSpec((1, r_heads, kd, kd), lambda s, pt: (s // steps_per_batch, 0, 0, 0)),
                   pl.BlockSpec((rows, aw), seq_rows)],
        scratch_shapes=[pltpu.VMEM((r_heads, kd, kd), F32)],
    )
    n_k = MIX_SEQS * n_pages
    return pl.pallas_call(
        kern,
        grid_spec=grid_spec,
        out_shape=[jax.ShapeDtypeStruct((b, t, r_heads * kd), BF16),
                   jax.ShapeDtypeStruct((b, r_heads, kd, kd), F32),
                   jax.ShapeDtypeStruct((n_seq * SAMPLE_PAD, aw), F32)],
        compiler_params=pltpu.CompilerParams(dimension_semantics=("arbitrary",),
                                             vmem_limit_bytes=VMEM_LIMIT),
        name="hgrn_prompt_attn_decode",
    )(page_table, lam, r, lb, gn, jnp.asarray(tri, BF16), jnp.asarray(lev), q, kn, vn, bias, sg,
      *([ck] * n_k), *([cv] * n_k))


def _hgrn_step_kernel(r_ref, lb_ref, gn_ref, s_ref, o_ref, so_ref, *, n_heads, kd, n_tok, n_seq):
    rw = n_heads * kd
    rowid = lax.broadcasted_iota(jnp.int32, (SAMPLE_PAD, kd), 0)
    live = rowid < n_tok
    zpad = jnp.zeros((SAMPLE_PAD, kd), F32)
    pad16 = lambda z: jnp.concatenate([z, zpad], axis=0).astype(BF16)
    ones16 = jnp.ones((2 * SAMPLE_PAD, kd), BF16)
    for g in range(n_seq):
        rows = slice(g * SAMPLE_PAD, (g + 1) * SAMPLE_PAD)
        for h in range(n_heads):
            cols = slice(h * kd, (h + 1) * kd)
            q = r_ref[rows, h * kd:(h + 1) * kd]
            rf = r_ref[rows, rw + h * kd:rw + (h + 1) * kd]
            v = r_ref[rows, 2 * rw + h * kd:2 * rw + (h + 1) * kd]
            rg = r_ref[rows, 3 * rw + h * kd:3 * rw + (h + 1) * kd]
            lb = lb_ref[:, cols]
            f = lb + (1.0 - lb) * jax.nn.sigmoid(rf)
            kk = jnp.where(live, 1.0 - f, 0.0)
            b = jnp.where(live, jnp.log(f) * LOG2E, 0.0)
            for sh in (1, 2, 4):
                b = b + jnp.where(rowid >= sh, pltpu.roll(b, sh, 0), 0.0)
            s0 = s_ref[g, h]
            o = jnp.sum(q * kk, axis=1, keepdims=True) * v
            for d in range(1, n_tok):
                x = q * pltpu.roll(kk, d, 0) * jnp.exp2(b - pltpu.roll(b, d, 0))
                w = jnp.sum(jnp.where(rowid >= d, x, 0.0), axis=1, keepdims=True)
                o = o + w * pltpu.roll(v, d, 0)
            o = o + _nn(pad16(q * jnp.exp2(b)), s0.astype(BF16))[0:SAMPLE_PAD]
            b_last = b[n_tok - 1:n_tok, :]
            decay = jnp.exp2(b_last)
            d_hi = decay.astype(BF16).astype(F32)
            d_split = jnp.where(rowid == 0, d_hi, jnp.where(rowid == 1, decay - d_hi, 0.0))
            decay_cols = _tn(pad16(d_split), ones16)
            so_ref[g, h] = decay_cols * s0 + _tn(pad16(kk * jnp.exp2(b_last - b)), pad16(v))
            ms = jnp.mean(o * o, axis=-1, keepdims=True)
            on = o * lax.rsqrt(ms + EPS) * gn_ref[...] * (rg * jax.nn.sigmoid(rg))
            o_ref[rows, cols] = on.astype(o_ref.dtype)


def _hgrn_sample(r, lb, gn, state, n_heads, kd, n_tok, seq_per_step=2):
    n_seq = state.shape[0]
    kern = functools.partial(_hgrn_step_kernel, n_heads=n_heads, kd=kd, n_tok=n_tok, n_seq=seq_per_step)
    rows = seq_per_step * SAMPLE_PAD
    return pl.pallas_call(
        kern,
        grid=(n_seq // seq_per_step,),
        in_specs=[pl.BlockSpec((rows, r.shape[1]), lambda i: (i, 0)),
                  pl.BlockSpec((1, n_heads * kd), lambda i: (0, 0)),
                  pl.BlockSpec((1, kd), lambda i: (0, 0)),
                  pl.BlockSpec((seq_per_step, n_heads, kd, kd), lambda i: (i, 0, 0, 0))],
        out_specs=[pl.BlockSpec((rows, n_heads * kd), lambda i: (i, 0)),
                   pl.BlockSpec((seq_per_step, n_heads, kd, kd), lambda i: (i, 0, 0, 0))],
        out_shape=[jax.ShapeDtypeStruct((n_seq * SAMPLE_PAD, n_heads * kd), F32),
                   jax.ShapeDtypeStruct(state.shape, F32)],
        compiler_params=pltpu.CompilerParams(dimension_semantics=("arbitrary",),
                                             vmem_limit_bytes=VMEM_LIMIT),
        name="hgrn_sample",
    )(r, lb, gn, state)


def _ffn_kernel(*refs, sample, aw):
    if sample:
        (x_ref, oa_ref, or_ref, wo_ref, g2_ref, wg_ref, wu_ref, cw_ref, cb_ref, wd_ref, prev_ref,
         y_ref, a_ref) = refs
    else:
        (x_ref, oa_ref, or_ref, wo_ref, g2_ref, wg_ref, wu_ref, cw_ref, cb_ref, wd_ref,
         y_ref, a_ref, carry_sc) = refs

        @pl.when(pl.program_id(1) == 0)
        def _():
            carry_sc[...] = jnp.zeros(carry_sc.shape, F32)

    x = x_ref[...].astype(F32)
    x1 = x + _nn(oa_ref[...].astype(BF16), wo_ref[0:aw, :]) + _nn(or_ref[...].astype(BF16), wo_ref[aw:, :])
    ms = jnp.mean(x1 * x1, axis=-1, keepdims=True)
    h2 = (x1 * lax.rsqrt(ms + EPS) * g2_ref[...]).astype(BF16)
    tm = x.shape[0]
    a = _nn(h2, wg_ref[...])
    u = _nn(h2, wu_ref[...])
    rowid = lax.broadcasted_iota(jnp.int32, a.shape, 0)
    am1 = pltpu.roll(a, 1, 0)
    am2 = pltpu.roll(a, 2, 0)
    if sample:
        rig = rowid & (SAMPLE_PAD - 1)
        prev = prev_ref[...]
        am1 = jnp.where(rig == 0, pltpu.roll(prev, tm - 1, 0), am1)
        am2 = jnp.where(rig < 2, prev, am2)
        a_ref[...] = a
    else:
        last = carry_sc[...]
        am1 = jnp.where(rowid == 0, last[SUBLANES - 1:SUBLANES, :], am1)
        am2 = jnp.where(rowid == 0, last[SUBLANES - 2:SUBLANES - 1, :],
                        jnp.where(rowid == 1, last[SUBLANES - 1:SUBLANES, :], am2))
        tail = a[tm - SUBLANES:tm, :]
        carry_sc[...] = tail
        a_ref[0] = tail
    cw = cw_ref[...]
    cv = cb_ref[...] + cw[0:1, :] * am2 + cw[1:2, :] * am1 + cw[2:3, :] * a
    gate = (cv * jax.nn.sigmoid(cv) * u).astype(BF16)
    y_ref[...] = x1 + _nn(gate, wd_ref[...])


def _ffn(x2, oa, orec, wo, g2, wg, wu, cw, cb, wd, prev, tm, n_batch):
    n, d = x2.shape
    aw = oa.shape[1]
    dff = wg.shape[1]
    sample = prev is not None
    kern = functools.partial(_ffn_kernel, sample=sample, aw=aw)
    scratch = []
    if sample:
        grid = (n // tm,)
        row = lambda i: (i, 0)
        c2 = lambda i: (0, 0)
        extra_in = [pl.BlockSpec((tm, dff), row)]
        extra_args = [prev]
        a_shape = jax.ShapeDtypeStruct((n, dff), F32)
        a_spec = pl.BlockSpec((tm, dff), row)
        sem = ("arbitrary",)
    else:
        per_b = n // n_batch // tm
        grid = (n_batch, per_b)
        row = lambda b, i: (b * per_b + i, 0)
        c2 = lambda b, i: (0, 0)
        extra_in = []
        extra_args = []
        a_shape = jax.ShapeDtypeStruct((n_batch, SUBLANES, dff), F32)
        a_spec = pl.BlockSpec((1, SUBLANES, dff), lambda b, i: (b, 0, 0))
        scratch.append(pltpu.VMEM((SUBLANES, dff), F32))
        sem = ("arbitrary", "arbitrary")
    return pl.pallas_call(
        kern,
        grid=grid,
        in_specs=[pl.BlockSpec((tm, d), row),
                  pl.BlockSpec((tm, aw), row),
                  pl.BlockSpec((tm, aw), row),
                  _const_spec(wo.shape, c2),
                  pl.BlockSpec((1, d), c2),
                  _const_spec(wg.shape, c2),
                  _const_spec(wu.shape, c2),
                  pl.BlockSpec(cw.shape, c2),
                  pl.BlockSpec(cb.shape, c2),
                  _const_spec(wd.shape, c2)] + extra_in,
        out_specs=[pl.BlockSpec((tm, d), row), a_spec],
        out_shape=[jax.ShapeDtypeStruct((n, d), F32), a_shape],
        scratch_shapes=scratch,
        compiler_params=pltpu.CompilerParams(dimension_semantics=sem, vmem_limit_bytes=VMEM_LIMIT),
        name="ffn_sample" if sample else "ffn_prompt",
    )(x2, oa, orec, wo, g2, wg, wu, cw, cb, wd, *extra_args)


def kernel(x_prompt, x_sample, cache_k, cache_v, page_table, state_hgrn, state_conv, rel_bias, norm1_g, w_in,
           qk_norm_g, lambda_qk, subln_g, hgrn_lb, hgrn_onorm_g, w_out, norm2_g, w_gate, w_up, conv_w, conv_b,
           w_down):
    bp, tp, d = x_prompt.shape
    bs, ts, _ = x_sample.shape
    depth = w_in.shape[0]
    assert depth == 1 and CONV_W - 1 <= ts <= SAMPLE_PAD and conv_w.shape[1] == CONV_W
    n_heads, _, hd = cache_k.shape[3:]
    vd = cache_v.shape[4]
    aw = n_heads * vd
    r_heads, kd = state_hgrn.shape[2], state_hgrn.shape[3]
    rw = r_heads * kd
    dff = w_gate.shape[2]
    page_size = cache_k.shape[2]
    assert page_size >= MAX_DISTANCE and tp % TQ == 0
    l = 0
    lam_init = 0.8 - 0.6 * math.exp(-0.3 * l)
    out_scale = 1.0 - lam_init

    biasp, biass, lam, lb = _tables(rel_bias, lambda_qk[l], hgrn_lb, n_heads, ts, page_size, lam_init)

    w_in_bf = w_in[l].astype(BF16)
    wo_bf = w_out[l].astype(BF16)
    wg_bf = w_gate[l].astype(BF16)
    wu_bf = w_up[l].astype(BF16)
    wd_bf = w_down[l].astype(BF16)
    cw = conv_w[l]
    cb = conv_b[l].reshape(1, dff)
    g1 = norm1_g[l].reshape(1, d)
    g2 = norm2_g[l].reshape(1, d)
    qg = jnp.tile(qk_norm_g[l, 0].reshape(1, 2 * hd), (1, n_heads))
    kg = jnp.tile(qk_norm_g[l, 1].reshape(1, 2 * hd), (1, n_heads))
    seg = np.arange(aw) // hd
    bd = jnp.asarray((seg[:, None] == seg[None, :]).astype(np.float32) / hd, BF16)
    sg = subln_g[l].reshape(1, vd)
    gn = hgrn_onorm_g[l].reshape(1, kd)
    q_scale = hd ** -0.5 * LOG2E

    xp2 = x_prompt.reshape(bp * tp, d)
    qb, kft, vf, kb, vbt, rp = _proj(xp2, g1, w_in_bf, qg, kg, bd, aw, rw, vd, q_scale, True, BF16, BF16, 512, bp)
    xs_pad = jnp.pad(x_sample, ((0, 0), (0, SAMPLE_PAD - ts), (0, 0))).reshape(bs * SAMPLE_PAD, d)
    qs, ksf, vsf, rs = _proj(xs_pad, g1, w_in_bf, qg, kg, bd, aw, rw, vd, q_scale, False, F32, F32, 512, bs)

    o_att = _attn(lam, qb.reshape(bp, tp, aw), kb.reshape(bp, tp, aw), vbt, biasp, sg, n_heads, hd, out_scale)
    ck = jnp.transpose(cache_k[l], (0, 2, 3, 4, 1)).reshape(cache_k.shape[1], aw, page_size)
    cv = cache_v[l].reshape(cache_v.shape[1], page_size * n_heads, vd)
    o_rec, s_p, o_att_s = _mix(page_table, lam, rp.reshape(bp, tp, 4 * rw), lb, gn, qs, ksf, vsf, biass, sg, ck, cv,
                               n_heads, hd, r_heads, kd, out_scale)
    o_rec_s, s_s = _hgrn_sample(rs, lb, gn, state_hgrn[l], r_heads, kd, ts)

    yp, a_tail = _ffn(xp2, o_att.reshape(bp * tp, aw), o_rec.reshape(bp * tp, rw), wo_bf, g2, wg_bf, wu_bf, cw, cb,
                      wd_bf, None, 512, bp)
    k_prompt = kft.reshape(bp, n_heads, 2, hd, tp).transpose(0, 4, 1, 2, 3)[None]
    v_prompt = vf.reshape(1, bp, tp, n_heads, vd)
    conv_prompt = a_tail[:, SUBLANES - (CONV_W - 1):][None]
    prev = jnp.pad(state_conv[l], ((0, 0), (0, SAMPLE_PAD - (CONV_W - 1)), (0, 0))).reshape(bs * SAMPLE_PAD, dff)
    ys, a_s = _ffn(xs_pad, o_att_s, o_rec_s, wo_bf, g2, wg_bf, wu_bf, cw, cb, wd_bf, prev, 256, 1)
    unpad = lambda z: z.reshape(bs, SAMPLE_PAD, -1)[:, :ts]
    y_sample = unpad(ys)
    k_sample = unpad(ksf).reshape(1, bs, ts, n_heads, 2, hd)
    v_sample = unpad(vsf).reshape(1, bs, ts, n_heads, vd)
    conv_sample = a_s.reshape(bs, SAMPLE_PAD, dff)[:, ts - (CONV_W - 1):ts][None]

    return (yp.reshape(bp, tp, d), y_sample, k_prompt, v_prompt, s_p[None], conv_prompt,
            k_sample, v_sample, s_s[None], conv_sample)
```

```python
import functools
import math

import numpy as np
import jax
import jax.numpy as jnp
from jax import lax
from jax.experimental import pallas as pl
from jax.experimental.pallas import tpu as pltpu

F32 = jnp.float32
BF16 = jnp.bfloat16

EPS = 1e-6
N_BUCKETS = 32
MAX_DISTANCE = 128
CONV_W = 3
NEG = -1e30
LOG2E = math.log2(math.e)
LANES = 128
SUBLANES = 8

TQ = 512
HG_CHUNK = 128
MIX_SEQS = 2
SAMPLE_PAD = 8
VMEM_LIMIT = 56 * 1024 * 1024


def _const_spec(shape, index_map):
    return pl.BlockSpec(shape, index_map, pipeline_mode=pl.Buffered(1))


def _nt(a, b):
    return lax.dot_general(a, b, (((1,), (1,)), ((), ())), preferred_element_type=F32)


def _tn(a, b):
    return lax.dot_general(a, b, (((0,), (0,)), ((), ())), preferred_element_type=F32)


def _nn(a, b):
    return jnp.dot(a, b, preferred_element_type=F32)


def _rel_bucket_np(n):
    n = np.maximum(n, 0)
    max_exact = N_BUCKETS // 2
    nf = np.maximum(n, 1).astype(np.float32)
    large = max_exact + (np.log(nf / np.float32(max_exact)) / np.float32(math.log(MAX_DISTANCE / max_exact))
                         * np.float32(N_BUCKETS - max_exact)).astype(np.int32)
    large = np.minimum(large, N_BUCKETS - 1)
    return np.where(n < max_exact, n, large).astype(np.int32)


def _tables_kernel(relb_ref, lq_ref, hlb_ref, bkp_ref, bks_ref,
                   biasp_ref, biass_ref, lam_ref, lb_ref, *, n_heads, lam_init, near_blocks):
    for h in range(n_heads):
        far = relb_ref[N_BUCKETS - 1, h]
        for kind in range(2):
            biasp_ref[h, kind] = jnp.where(bkp_ref[kind] < 0, NEG, 0.0).astype(F32)
            for (r0, c0) in near_blocks[kind]:
                bk = bkp_ref[kind, r0:r0 + LANES, c0:c0 + LANES]
                acc = jnp.where(bk < 0, NEG, 0.0).astype(F32)
                for b in range(N_BUCKETS - 1):
                    acc = jnp.where(bk == b, (relb_ref[b, h] - far) * LOG2E, acc)
                biasp_ref[h, kind, r0:r0 + LANES, c0:c0 + LANES] = acc
            rows = bks_ref.shape[1] // n_heads
            bk = bks_ref[kind, h * rows:(h + 1) * rows, :]
            acc = jnp.where(bk < 0, NEG, 0.0).astype(F32)
            for b in range(N_BUCKETS - 1):
                acc = jnp.where(bk == b, (relb_ref[b, h] - far) * LOG2E, acc)
            biass_ref[kind, h * rows:(h + 1) * rows, :] = acc
    lq = lq_ref[...].astype(F32)
    s1 = jnp.sum(lq[0:1] * lq[1:2], axis=1, keepdims=True)
    s2 = jnp.sum(lq[2:3] * lq[3:4], axis=1, keepdims=True)
    lam = jnp.exp(s1) - jnp.exp(s2) + lam_init
    lam_ref[...] = jnp.broadcast_to(lam, lam_ref.shape)
    hl = hlb_ref[...].astype(F32)
    mx = jnp.max(hl, axis=0, keepdims=True)
    e = jnp.exp(hl - mx)
    lb_ref[...] = e[0:1] / jnp.sum(e, axis=0, keepdims=True)


def _tables(rel_bias, lambda_qk_l, hgrn_lb, n_heads, n_tok, page_size, lam_init):
    c = np.arange(TQ)[:, None]
    r = np.arange(TQ)[None, :]
    diag = np.where(c <= r, _rel_bucket_np(r - c), -1)
    prev = _rel_bucket_np(TQ + r - c)
    bkp_np = np.stack([diag, prev]).astype(np.int32)
    bkp = jnp.asarray(bkp_np)
    near_blocks = tuple(
        tuple((r0, c0) for r0 in range(0, TQ, LANES) for c0 in range(0, TQ, LANES)
              if np.any((bkp_np[kind, r0:r0 + LANES, c0:c0 + LANES] >= 0)
                        & (bkp_np[kind, r0:r0 + LANES, c0:c0 + LANES] < N_BUCKETS - 1)))
        for kind in range(2))
    t = np.tile(np.arange(SAMPLE_PAD), 2 * n_heads)[:, None]
    cc = np.arange(page_size)[None, :]
    last_page = _rel_bucket_np(page_size + t - cc)
    new_page = np.where(cc <= np.minimum(t, n_tok - 1), _rel_bucket_np(t - cc), -1)
    bks = jnp.asarray(np.stack([last_page, new_page]).astype(np.int32))
    rows_s = 2 * n_heads * SAMPLE_PAD
    kern = functools.partial(_tables_kernel, n_heads=n_heads, lam_init=lam_init, near_blocks=near_blocks)
    return pl.pallas_call(
        kern,
        out_shape=(jax.ShapeDtypeStruct((n_heads, 2, TQ, TQ), F32),
                   jax.ShapeDtypeStruct((2, rows_s, page_size), F32),
                   jax.ShapeDtypeStruct((SUBLANES, LANES), F32),
                   jax.ShapeDtypeStruct((1, hgrn_lb.shape[1]), F32)),
        in_specs=[pl.BlockSpec(memory_space=pltpu.SMEM),
                  pl.BlockSpec(memory_space=pltpu.VMEM),
                  pl.BlockSpec(memory_space=pltpu.VMEM),
                  pl.BlockSpec(memory_space=pltpu.VMEM),
                  pl.BlockSpec(memory_space=pltpu.VMEM)],
        name="tables",
    )(rel_bias, lambda_qk_l, hgrn_lb, bkp, bks)


def _proj_kernel(x_ref, g1_ref, w_ref, qg_ref, kg_ref, bd_ref, *out_refs, aw, rw, vd, q_scale, emit_bf16):
    if emit_bf16:
        q_ref, kf_ref, vf_ref, kb_ref, vb_ref, r_ref = out_refs
    else:
        q_ref, kf_ref, vf_ref, r_ref = out_refs
    x = x_ref[...].astype(F32)
    ms = jnp.mean(x * x, axis=-1, keepdims=True)
    h = (x * lax.rsqrt(ms + EPS) * g1_ref[...]).astype(BF16)

    def seg_norm(z, g):
        msq = _nn((z * z).astype(BF16), bd_ref[...])
        return z * lax.rsqrt(msq + EPS) * g

    zq = _nn(h, w_ref[:, 0:aw])
    q_ref[...] = (seg_norm(zq, qg_ref[...]) * q_scale).astype(q_ref.dtype)
    zk = _nn(h, w_ref[:, aw:2 * aw])
    kn = seg_norm(zk, kg_ref[...])
    zv = _nn(h, w_ref[:, 2 * aw:3 * aw])
    if emit_bf16:
        n_heads = aw // vd
        kf_ref[0] = kn.T
        kb_ref[...] = kn.astype(BF16)
        vb_ref[0] = zv.T.astype(BF16)
        for hh in range(n_heads):
            vf_ref[pl.ds(hh, zv.shape[0], stride=n_heads), :] = zv[:, hh * vd:(hh + 1) * vd]
    else:
        kf_ref[...] = kn
        vf_ref[...] = zv
    for g in range(4):
        z = _nn(h, w_ref[:, 3 * aw + g * rw:3 * aw + (g + 1) * rw])
        r_ref[:, g * rw:(g + 1) * rw] = z.astype(r_ref.dtype)


def _proj(x2, g1, w_bf, qg, kg, bd, aw, rw, vd, q_scale, emit_bf16, q_dtype, r_dtype, tm, n_batch):
    n, d = x2.shape
    kern = functools.partial(_proj_kernel, aw=aw, rw=rw, vd=vd, q_scale=q_scale, emit_bf16=emit_bf16)
    row = lambda i: (i, 0)
    const = lambda i: (0, 0)
    if emit_bf16:
        t = n // n_batch
        per_b = t // tm
        n_heads = aw // vd
        xposed = lambda i: (i // per_b, 0, i % per_b)
        out_shape = [jax.ShapeDtypeStruct((n, aw), q_dtype),
                     jax.ShapeDtypeStruct((n_batch, aw, t), F32),
                     jax.ShapeDtypeStruct((n * n_heads, vd), F32),
                     jax.ShapeDtypeStruct((n, aw), BF16),
                     jax.ShapeDtypeStruct((n_batch, aw, t), BF16)]
        out_specs = [pl.BlockSpec((tm, aw), row),
                     pl.BlockSpec((1, aw, tm), xposed),
                     pl.BlockSpec((tm * n_heads, vd), row),
                     pl.BlockSpec((tm, aw), row),
                     pl.BlockSpec((1, aw, tm), xposed)]
    else:
        out_shape = [jax.ShapeDtypeStruct((n, aw), q_dtype),
                     jax.ShapeDtypeStruct((n, aw), F32),
                     jax.ShapeDtypeStruct((n, aw), F32)]
        out_specs = [pl.BlockSpec((tm, aw), row)] * 3
    out_shape.append(jax.ShapeDtypeStruct((n, 4 * rw), r_dtype))
    out_specs.append(pl.BlockSpec((tm, 4 * rw), row))
    return pl.pallas_call(
        kern,
        grid=(n // tm,),
        in_specs=[pl.BlockSpec((tm, d), row),
                  pl.BlockSpec((1, d), const),
                  _const_spec(w_bf.shape, const),
                  pl.BlockSpec((1, aw), const),
                  pl.BlockSpec((1, aw), const),
                  pl.BlockSpec((aw, aw), const)],
        out_specs=out_specs,
        out_shape=out_shape,
        compiler_params=pltpu.CompilerParams(dimension_semantics=("arbitrary",),
                                             vmem_limit_bytes=VMEM_LIMIT),
        name="proj_bf16" if emit_bf16 else "proj_f32",
    )(x2, g1, w_bf, qg, kg, bd)


def _attn_kernel(lam_ref, q_ref, k_ref, vt_ref, bias_ref, sg_ref, o_ref,
                 qst_sc, sa_sc, ma_sc, sb_sc, mb_sc, m_sc, l_sc, acc_sc, *, hd, out_scale):
    i = pl.program_id(2)

    qt = q_ref[0].astype(F32).T
    row = lax.broadcasted_iota(jnp.int32, qt.shape, 0)
    qst_sc[...] = jnp.concatenate([jnp.where(row < hd, qt, 0.0), jnp.where(row >= hd, qt, 0.0)],
                                  axis=1).astype(BF16)
    m_sc[...] = jnp.full(m_sc.shape, NEG, F32)
    l_sc[...] = jnp.zeros(l_sc.shape, F32)
    acc_sc[...] = jnp.zeros(acc_sc.shape, F32)

    bufs = ((sa_sc, ma_sc), (sb_sc, mb_sc))

    def issue(buf, g, with_max):
        start = pl.multiple_of(g * TQ, TQ)
        s = _nn(k_ref[0, pl.ds(start, TQ), :], qst_sc[...])
        buf[0][...] = s
        if with_max:
            buf[1][...] = jnp.max(s, axis=0, keepdims=True)

    def consume(buf, g, kind):
        s = buf[0][...]
        if kind is None:
            s_max = buf[1][...]
        else:
            bias = bias_ref[0, kind]
            s = jnp.concatenate([s[:, 0:TQ] + bias, s[:, TQ:2 * TQ] + bias], axis=1)
            s_max = jnp.max(s, axis=0, keepdims=True)
        m_old = m_sc[...]
        m_new = jnp.maximum(m_old, s_max)
        alpha = jnp.exp2(m_old - m_new)
        p = jnp.exp2(s - m_new)
        l_sc[...] = alpha * l_sc[...] + jnp.sum(p, axis=0, keepdims=True)
        start = pl.multiple_of(g * TQ, TQ)
        acc_sc[...] = alpha * acc_sc[...] + _nn(vt_ref[0, :, pl.ds(start, TQ)], p.astype(BF16))
        m_sc[...] = m_new

    n_pure = jnp.maximum(i - 1, 0)
    issue(bufs[0], 0, True)

    def pair(jj, carry):
        g = 2 * jj
        issue(bufs[1], g + 1, True)
        consume(bufs[0], g, None)
        issue(bufs[0], g + 2, True)
        consume(bufs[1], g + 1, None)
        return carry

    lax.fori_loop(0, n_pure // 2, pair, 0)
    odd = (n_pure & 1) == 1

    def finish(cur, other):
        @pl.when(i > 0)
        def _():
            issue(other, i, False)
            consume(cur, i - 1, 1)
            consume(other, i, 0)

        @pl.when(i == 0)
        def _():
            consume(cur, 0, 0)

    @pl.when(odd)
    def _():
        issue(bufs[1], n_pure, False)
        consume(bufs[0], n_pure - 1, None)
        finish(bufs[1], bufs[0])

    @pl.when(jnp.logical_not(odd))
    def _():
        finish(bufs[0], bufs[1])

    a = acc_sc[...] * (1.0 / l_sc[...])
    ot = a[:, 0:TQ] - lam_ref[0, 0] * a[:, TQ:2 * TQ]
    ms = jnp.mean(ot * ot, axis=0, keepdims=True)
    o = (ot * lax.rsqrt(ms + EPS)).T
    o_ref[0] = (o * (sg_ref[...] * out_scale)).astype(o_ref.dtype)


def _attn(lam, q, k, vt, bias, sg, n_heads, hd, out_scale):
    b, t, aw = q.shape
    vd = aw // n_heads
    assert t % TQ == 0
    kern = functools.partial(_attn_kernel, hd=hd, out_scale=out_scale)
    return pl.pallas_call(
        kern,
        grid=(b, n_heads, t // TQ),
        in_specs=[pl.BlockSpec(memory_space=pltpu.SMEM),
                  pl.BlockSpec((1, TQ, vd), lambda bi, h, i: (bi, i, h)),
                  pl.BlockSpec((1, t, vd), lambda bi, h, i: (bi, 0, h)),
                  pl.BlockSpec((1, vd, t), lambda bi, h, i: (bi, h, 0)),
                  pl.BlockSpec((1, 2, TQ, TQ), lambda bi, h, i: (h, 0, 0, 0)),
                  pl.BlockSpec((1, vd), lambda bi, h, i: (0, 0))],
        out_specs=pl.BlockSpec((1, TQ, vd), lambda bi, h, i: (bi, i, h)),
        out_shape=jax.ShapeDtypeStruct((b, t, aw), BF16),
        scratch_shapes=[pltpu.VMEM((vd, 2 * TQ), BF16),
                        pltpu.VMEM((TQ, 2 * TQ), F32),
                        pltpu.VMEM((1, 2 * TQ), F32),
                        pltpu.VMEM((TQ, 2 * TQ), F32),
                        pltpu.VMEM((1, 2 * TQ), F32),
                        pltpu.VMEM((1, 2 * TQ), F32),
                        pltpu.VMEM((1, 2 * TQ), F32),
                        pltpu.VMEM((vd, 2 * TQ), F32)],
        compiler_params=pltpu.CompilerParams(dimension_semantics=("arbitrary",) * 3,
                                             vmem_limit_bytes=VMEM_LIMIT),
        name="attn_prompt",
    )(lam, q, k, vt, bias, sg)


def _decode_seq(lam, q8, kn8, vn8, bias_ref, sg, k_refs, v_refs, o_ref, rows_out, n_heads, hd, out_scale):
    n_pages = len(k_refs)
    aw = q8.shape[1]
    vd = aw // n_heads
    ps = k_refs[0].shape[2]
    lane = lax.broadcasted_iota(jnp.int32, q8.shape, 1)
    pieces = [jnp.where((lane >= hm * hd) & (lane < (hm + 1) * hd), q8, 0.0) for hm in range(2 * n_heads)]
    qbd = jnp.concatenate(pieces, axis=0).astype(BF16)
    s_list = []
    for p in range(n_pages):
        s = _nn(qbd, k_refs[p][0].astype(BF16))
        if p == n_pages - 1:
            s = s + bias_ref[0]
        s_list.append(s)
    pad = jnp.zeros((ps - SAMPLE_PAD, aw), F32)
    k_new = jnp.concatenate([kn8, pad], axis=0).astype(BF16)
    v_new = jnp.concatenate([vn8, pad], axis=0).astype(BF16)
    s_list.append(_nt(qbd, k_new) + bias_ref[1])
    s_all = jnp.concatenate(s_list, axis=1)
    m = jnp.max(s_all, axis=1, keepdims=True)
    e = jnp.exp2(s_all - m)
    inv = 1.0 / jnp.sum(e, axis=1, keepdims=True)
    eb = e.astype(BF16)
    rows = 2 * SAMPLE_PAD
    for h in range(n_heads):
        eh = eb[h * rows:(h + 1) * rows]
        acc = _nn(eh[:, n_pages * ps:(n_pages + 1) * ps], v_new[:, h * vd:(h + 1) * vd])
        for p in range(n_pages):
            vh = v_refs[p][0, pl.ds(h, ps, stride=n_heads), :].astype(BF16)
            acc = acc + _nn(eh[:, p * ps:(p + 1) * ps], vh)
        acc = acc * inv[h * rows:(h + 1) * rows]
        o = acc[0:SAMPLE_PAD] - lam * acc[SAMPLE_PAD:rows]
        ms = jnp.mean(o * o, axis=-1, keepdims=True)
        o_ref[rows_out, h * vd:(h + 1) * vd] = (o * lax.rsqrt(ms + EPS) * (sg * out_scale)).astype(o_ref.dtype)


def _mid_reference(b, level):
    c, w = b.shape
    p = 1 << level
    half = p // 2
    if p >= SUBLANES:
        bp = b.reshape(c // p, p, w)
        return jnp.broadcast_to(bp[:, half - 1:half, :], (c // p, p, w)).reshape(c, w)
    b8 = b.reshape(c // SUBLANES, SUBLANES, w)
    rig = lax.broadcasted_iota(jnp.int32, b8.shape, 1)
    out = None
    for start in range(SUBLANES - p, -1, -p):
        row = jnp.broadcast_to(b8[:, start + half - 1:start + half, :], b8.shape)
        out = row if out is None else jnp.where(rig < start + p, row, out)
    return out.reshape(c, w)


def _upper_lower(up, low, level):
    c, w = up.shape
    half = (1 << level) // 2
    if half >= SUBLANES:
        shape = (c // (2 * half), 2, half, w)
        return jnp.concatenate([low.reshape(shape)[:, 0:1], up.reshape(shape)[:, 1:2]], axis=1).reshape(c, w)
    rowid = lax.broadcasted_iota(jnp.int32, up.shape, 0)
    return jnp.where((rowid & half) != 0, up, low)


def _hgrn_chunk(r_ref, r0, lb_ref, gn_ref, tri, lev, o_ref, st_sc, n_heads, kd):
    c = HG_CHUNK
    rw = n_heads * kd
    n_levels = c.bit_length() - 1
    head = lambda z, h: z[:, h * kd:(h + 1) * kd]
    q = r_ref[0, r0:r0 + c, 0:rw].astype(F32)
    rf = r_ref[0, r0:r0 + c, rw:2 * rw].astype(F32)
    v = r_ref[0, r0:r0 + c, 2 * rw:3 * rw].astype(BF16)
    rg = r_ref[0, r0:r0 + c, 3 * rw:4 * rw].astype(F32)
    lb = lb_ref[...]
    f = lb + (1.0 - lb) * jax.nn.sigmoid(rf)
    logf = jnp.log(f) * LOG2E
    kk = 1.0 - f
    hi = logf.astype(BF16)
    lo = (logf - hi.astype(F32)).astype(BF16)
    b = _nn(tri, hi) + _nn(tri, lo)
    qb = q.astype(BF16)
    kb = kk.astype(BF16)
    diag = lev == 0
    a = [jnp.where(diag, _nt(head(qb, h), head(kb, h)), 0.0) for h in range(n_heads)]
    for level in range(1, n_levels + 1):
        e = jnp.exp2(-jnp.abs(b - _mid_reference(b, level)))
        z = (_upper_lower(q, kk, level) * e).astype(BF16)
        here = lev == level
        a = [jnp.where(here, _nt(head(z, h), head(z, h)), a[h]) for h in range(n_heads)]
    b_last = b[c - 1:c, :]
    q_hat = (q * jnp.exp2(b)).astype(BF16)
    k_hat = (kk * jnp.exp2(b_last - b)).astype(BF16)
    decay = jnp.exp2(b_last)
    outs = []
    for h in range(n_heads):
        st = st_sc[h]
        o = _nn(a[h].astype(BF16), head(v, h)) + _nt(head(q_hat, h), st.astype(BF16))
        st_sc[h] = head(decay, h) * st + _tn(head(v, h), head(k_hat, h))
        ms = jnp.mean(o * o, axis=-1, keepdims=True)
        outs.append(o * lax.rsqrt(ms + EPS) * gn_ref[...])
    on = jnp.concatenate(outs, axis=1) * (rg * jax.nn.sigmoid(rg))
    o_ref[0, r0:r0 + c, :] = on.astype(o_ref.dtype)


def _mix_kernel(pt_ref, lam_ref, r_ref, lb_ref, gn_ref, tri_ref, lev_ref, q_ref, kn_ref, vn_ref, bias_ref, sg_ref,
                *refs, n_pages, n_heads, hd, r_heads, kd, out_scale, steps_per_batch):
    del pt_ref
    n_k = MIX_SEQS * n_pages
    k_refs, v_refs = refs[:n_k], refs[n_k:2 * n_k]
    o_rec_ref, sfin_ref, o_dec_ref, st_sc = refs[2 * n_k:]
    t = lax.rem(pl.program_id(0), steps_per_batch)

    @pl.when(t == 0)
    def _():
        st_sc[...] = jnp.zeros(st_sc.shape, F32)

    lev = lev_ref[...]
    tri = tri_ref[...]
    lam = lam_ref[0, 0]
    for c in range(MIX_SEQS):
        _hgrn_chunk(r_ref, c * HG_CHUNK, lb_ref, gn_ref, tri, lev, o_rec_ref, st_sc, r_heads, kd)
        rows = slice(c * SAMPLE_PAD, (c + 1) * SAMPLE_PAD)
        _decode_seq(lam, q_ref[rows, :].astype(F32), kn_ref[rows, :].astype(F32), vn_ref[rows, :].astype(F32),
                    bias_ref, sg_ref[...], k_refs[c * n_pages:(c + 1) * n_pages],
                    v_refs[c * n_pages:(c + 1) * n_pages], o_dec_ref, rows, n_heads, hd, out_scale)

    @pl.when(t == steps_per_batch - 1)
    def _():
        for h in range(r_heads):
            sfin_ref[0, h] = st_sc[h].T


def _mix(page_table, lam, r, lb, gn, q, kn, vn, bias, sg, ck, cv, n_heads, hd, r_heads, kd, out_scale):
    n_seq, n_pages = page_table.shape
    b, t, _ = r.shape
    _, aw, ps = ck.shape
    vd = aw // n_heads
    c = HG_CHUNK
    tokens = MIX_SEQS * c
    steps_per_batch = t // tokens
    n_steps = b * steps_per_batch
    assert t % tokens == 0 and n_seq == n_steps * MIX_SEQS
    idx = np.arange(c)
    x = idx[:, None] ^ idx[None, :]
    lev = np.where(idx[:, None] > idx[None, :], np.floor(np.log2(np.maximum(x, 1))).astype(np.int32) + 1, -1)
    lev = np.where(idx[:, None] == idx[None, :], 0, lev).astype(np.int32)
    tri = (idx[:, None] >= idx[None, :]).astype(np.float32)
    kern = functools.partial(_mix_kernel, n_pages=n_pages, n_heads=n_heads, hd=hd, r_heads=r_heads, kd=kd,
                             out_scale=out_scale, steps_per_batch=steps_per_batch)
    c2 = lambda s, pt: (0, 0)
    tok = lambda s, pt: (s // steps_per_batch, s % steps_per_batch, 0)
    seq_rows = lambda s, pt: (s, 0)
    page_idx = [functools.partial(lambda s, pt, j, p: (pt[s * MIX_SEQS + j, p], 0, 0), j=j, p=p)
                for j in range(MIX_SEQS) for p in range(n_pages)]
    k_specs = [pl.BlockSpec((1, aw, ps), im) for im in page_idx]
    v_specs = [pl.BlockSpec((1,) + cv.shape[1:], im) for im in page_idx]
    rows = MIX_SEQS * SAMPLE_PAD
    grid_spec = pltpu.PrefetchScalarGridSpec(
        num_scalar_prefetch=1,
        grid=(n_steps,),
        in_specs=[pl.BlockSpec(memory_space=pltpu.SMEM),
                  pl.BlockSpec((1, tokens, r.shape[2]), tok),
                  pl.BlockSpec((1, r_heads * kd), c2),
                  pl.BlockSpec((1, kd), c2),
                  pl.BlockSpec((c, c), c2),
                  pl.BlockSpec((c, c), c2),
                  pl.BlockSpec((rows, aw), seq_rows),
                  pl.BlockSpec((rows, aw), seq_rows),
                  pl.BlockSpec((rows, aw), seq_rows),
                  pl.BlockSpec(bias.shape, lambda s, pt: (0, 0, 0)),
                  pl.BlockSpec((1, vd), c2)] + k_specs + v_specs,
        out_specs=[pl.BlockSpec((1, tokens, r_heads * kd), tok),
                   pl.BlockSpec((1, r_heads, kd, kd), lambda s, pt: (s // steps_per_batch, 0, 0, 0)),
                   pl.BlockSpec((rows, aw), seq_rows)],
        scratch_shapes=[pltpu.VMEM((r_heads, kd, kd), F32)],
    )
    n_k = MIX_SEQS * n_pages
    return pl.pallas_call(
        kern,
        grid_spec=grid_spec,
        out_shape=[jax.ShapeDtypeStruct((b, t, r_heads * kd), BF16),
                   jax.ShapeDtypeStruct((b, r_heads, kd, kd), F32),
                   jax.ShapeDtypeStruct((n_seq * SAMPLE_PAD, aw), F32)],
        compiler_params=pltpu.CompilerParams(dimension_semantics=("arbitrary",),
                                             vmem_limit_bytes=VMEM_LIMIT),
        name="hgrn_prompt_attn_decode",
    )(page_table, lam, r, lb, gn, jnp.asarray(tri, BF16), jnp.asarray(lev), q, kn, vn, bias, sg,
      *([ck] * n_k), *([cv] * n_k))


def _hgrn_step_kernel(r_ref, lb_ref, gn_ref, s_ref, o_ref, so_ref, *, n_heads, kd, n_tok, n_seq):
    rw = n_heads * kd
    rowid = lax.broadcasted_iota(jnp.int32, (SAMPLE_PAD, kd), 0)
    live = rowid < n_tok
    zpad = jnp.zeros((SAMPLE_PAD, kd), F32)
    pad16 = lambda z: jnp.concatenate([z, zpad], axis=0).astype(BF16)
    ones16 = jnp.ones((2 * SAMPLE_PAD, kd), BF16)
    for g in range(n_seq):
        rows = slice(g * SAMPLE_PAD, (g + 1) * SAMPLE_PAD)
        for h in range(n_heads):
            cols = slice(h * kd, (h + 1) * kd)
            q = r_ref[rows, h * kd:(h + 1) * kd]
            rf = r_ref[rows, rw + h * kd:rw + (h + 1) * kd]
            v = r_ref[rows, 2 * rw + h * kd:2 * rw + (h + 1) * kd]
            rg = r_ref[rows, 3 * rw + h * kd:3 * rw + (h + 1) * kd]
            lb = lb_ref[:, cols]
            f = lb + (1.0 - lb) * jax.nn.sigmoid(rf)
            kk = jnp.where(live, 1.0 - f, 0.0)
            b = jnp.where(live, jnp.log(f) * LOG2E, 0.0)
            for sh in (1, 2, 4):
                b = b + jnp.where(rowid >= sh, pltpu.roll(b, sh, 0), 0.0)
            s0 = s_ref[g, h]
            o = jnp.sum(q * kk, axis=1, keepdims=True) * v
            for d in range(1, n_tok):
                x = q * pltpu.roll(kk, d, 0) * jnp.exp2(b - pltpu.roll(b, d, 0))
                w = jnp.sum(jnp.where(rowid >= d, x, 0.0), axis=1, keepdims=True)
                o = o + w * pltpu.roll(v, d, 0)
            o = o + _nn(pad16(q * jnp.exp2(b)), s0.astype(BF16))[0:SAMPLE_PAD]
            b_last = b[n_tok - 1:n_tok, :]
            decay = jnp.exp2(b_last)
            d_hi = decay.astype(BF16).astype(F32)
            d_split = jnp.where(rowid == 0, d_hi, jnp.where(rowid == 1, decay - d_hi, 0.0))
            decay_cols = _tn(pad16(d_split), ones16)
            so_ref[g, h] = decay_cols * s0 + _tn(pad16(kk * jnp.exp2(b_last - b)), pad16(v))
            ms = jnp.mean(o * o, axis=-1, keepdims=True)
            on = o * lax.rsqrt(ms + EPS) * gn_ref[...] * (rg * jax.nn.sigmoid(rg))
            o_ref[rows, cols] = on.astype(o_ref.dtype)


def _hgrn_sample(r, lb, gn, state, n_heads, kd, n_tok, seq_per_step=2):
    n_seq = state.shape[0]
    kern = functools.partial(_hgrn_step_kernel, n_heads=n_heads, kd=kd, n_tok=n_tok, n_seq=seq_per_step)
    rows = seq_per_step * SAMPLE_PAD
    return pl.pallas_call(
        kern,
        grid=(n_seq // seq_per_step,),
        in_specs=[pl.BlockSpec((rows, r.shape[1]), lambda i: (i, 0)),
                  pl.BlockSpec((1, n_heads * kd), lambda i: (0, 0)),
                  pl.BlockSpec((1, kd), lambda i: (0, 0)),
                  pl.BlockSpec((seq_per_step, n_heads, kd, kd), lambda i: (i, 0, 0, 0))],
        out_specs=[pl.BlockSpec((rows, n_heads * kd), lambda i: (i, 0)),
                   pl.BlockSpec((seq_per_step, n_heads, kd, kd), lambda i: (i, 0, 0, 0))],
        out_shape=[jax.ShapeDtypeStruct((n_seq * SAMPLE_PAD, n_heads * kd), F32),
                   jax.ShapeDtypeStruct(state.shape, F32)],
        compiler_params=pltpu.CompilerParams(dimension_semantics=("arbitrary",),
                                             vmem_limit_bytes=VMEM_LIMIT),
        name="hgrn_sample",
    )(r, lb, gn, state)


def _ffn_kernel(*refs, sample, aw):
    if sample:
        (x_ref, oa_ref, or_ref, wo_ref, g2_ref, wg_ref, wu_ref, cw_ref, cb_ref, wd_ref, prev_ref,
         y_ref, a_ref) = refs
    else:
        (x_ref, oa_ref, or_ref, wo_ref, g2_ref, wg_ref, wu_ref, cw_ref, cb_ref, wd_ref,
         y_ref, a_ref, carry_sc) = refs

        @pl.when(pl.program_id(1) == 0)
        def _():
            carry_sc[...] = jnp.zeros(carry_sc.shape, F32)

    x = x_ref[...].astype(F32)
    x1 = x + _nn(oa_ref[...].astype(BF16), wo_ref[0:aw, :]) + _nn(or_ref[...].astype(BF16), wo_ref[aw:, :])
    ms = jnp.mean(x1 * x1, axis=-1, keepdims=True)
    h2 = (x1 * lax.rsqrt(ms + EPS) * g2_ref[...]).astype(BF16)
    tm = x.shape[0]
    a = _nn(h2, wg_ref[...])
    u = _nn(h2, wu_ref[...])
    rowid = lax.broadcasted_iota(jnp.int32, a.shape, 0)
    am1 = pltpu.roll(a, 1, 0)
    am2 = pltpu.roll(a, 2, 0)
    if sample:
        rig = rowid & (SAMPLE_PAD - 1)
        prev = prev_ref[...]
        am1 = jnp.where(rig == 0, pltpu.roll(prev, tm - 1, 0), am1)
        am2 = jnp.where(rig < 2, prev, am2)
        a_ref[...] = a
    else:
        last = carry_sc[...]
        am1 = jnp.where(rowid == 0, last[SUBLANES - 1:SUBLANES, :], am1)
        am2 = jnp.where(rowid == 0, last[SUBLANES - 2:SUBLANES - 1, :],
                        jnp.where(rowid == 1, last[SUBLANES - 1:SUBLANES, :], am2))
        tail = a[tm - SUBLANES:tm, :]
        carry_sc[...] = tail
        a_ref[0] = tail
    cw = cw_ref[...]
    cv = cb_ref[...] + cw[0:1, :] * am2 + cw[1:2, :] * am1 + cw[2:3, :] * a
    gate = (cv * jax.nn.sigmoid(cv) * u).astype(BF16)
    y_ref[...] = x1 + _nn(gate, wd_ref[...])


def _ffn(x2, oa, orec, wo, g2, wg, wu, cw, cb, wd, prev, tm, n_batch):
    n, d = x2.shape
    aw = oa.shape[1]
    dff = wg.shape[1]
    sample = prev is not None
    kern = functools.partial(_ffn_kernel, sample=sample, aw=aw)
    scratch = []
    if sample:
        grid = (n // tm,)
        row = lambda i: (i, 0)
        c2 = lambda i: (0, 0)
        extra_in = [pl.BlockSpec((tm, dff), row)]
        extra_args = [prev]
        a_shape = jax.ShapeDtypeStruct((n, dff), F32)
        a_spec = pl.BlockSpec((tm, dff), row)
        sem = ("arbitrary",)
    else:
        per_b = n // n_batch // tm
        grid = (n_batch, per_b)
        row = lambda b, i: (b * per_b + i, 0)
        c2 = lambda b, i: (0, 0)
        extra_in = []
        extra_args = []
        a_shape = jax.ShapeDtypeStruct((n_batch, SUBLANES, dff), F32)
        a_spec = pl.BlockSpec((1, SUBLANES, dff), lambda b, i: (b, 0, 0))
        scratch.append(pltpu.VMEM((SUBLANES, dff), F32))
        sem = ("arbitrary", "arbitrary")
    return pl.pallas_call(
        kern,
        grid=grid,
        in_specs=[pl.BlockSpec((tm, d), row),
                  pl.BlockSpec((tm, aw), row),
                  pl.BlockSpec((tm, aw), row),
                  _const_spec(wo.shape, c2),
                  pl.BlockSpec((1, d), c2),
                  _const_spec(wg.shape, c2),
                  _const_spec(wu.shape, c2),
                  pl.BlockSpec(cw.shape, c2),
                  pl.BlockSpec(cb.shape, c2),
                  _const_spec(wd.shape, c2)] + extra_in,
        out_specs=[pl.BlockSpec((tm, d), row), a_spec],
        out_shape=[jax.ShapeDtypeStruct((n, d), F32), a_shape],
        scratch_shapes=scratch,
        compiler_params=pltpu.CompilerParams(dimension_semantics=sem, vmem_limit_bytes=VMEM_LIMIT),
        name="ffn_sample" if sample else "ffn_prompt",
    )(x2, oa, orec, wo, g2, wg, wu, cw, cb, wd, *extra_args)


def kernel(x_prompt, x_sample, cache_k, cache_v, page_table, state_hgrn, state_conv, rel_bias, norm1_g, w_in,
           qk_norm_g, lambda_qk, subln_g, hgrn_lb, hgrn_onorm_g, w_out, norm2_g, w_gate, w_up, conv_w, conv_b,
           w_down):
    bp, tp, d = x_prompt.shape
    bs, ts, _ = x_sample.shape
    depth = w_in.shape[0]
    assert depth == 1 and CONV_W - 1 <= ts <= SAMPLE_PAD and conv_w.shape[1] == CONV_W
    n_heads, _, hd = cache_k.shape[3:]
    vd = cache_v.shape[4]
    aw = n_heads * vd
    r_heads, kd = state_hgrn.shape[2], state_hgrn.shape[3]
    rw = r_heads * kd
    dff = w_gate.shape[2]
    page_size = cache_k.shape[2]
    assert page_size >= MAX_DISTANCE and tp % TQ == 0
    l = 0
    lam_init = 0.8 - 0.6 * math.exp(-0.3 * l)
    out_scale = 1.0 - lam_init

    biasp, biass, lam, lb = _tables(rel_bias, lambda_qk[l], hgrn_lb, n_heads, ts, page_size, lam_init)

    w_in_bf = w_in[l].astype(BF16)
    wo_bf = w_out[l].astype(BF16)
    wg_bf = w_gate[l].astype(BF16)
    wu_bf = w_up[l].astype(BF16)
    wd_bf = w_down[l].astype(BF16)
    cw = conv_w[l]
    cb = conv_b[l].reshape(1, dff)
    g1 = norm1_g[l].reshape(1, d)
    g2 = norm2_g[l].reshape(1, d)
    qg = jnp.tile(qk_norm_g[l, 0].reshape(1, 2 * hd), (1, n_heads))
    kg = jnp.tile(qk_norm_g[l, 1].reshape(1, 2 * hd), (1, n_heads))
    seg = np.arange(aw) // hd
    bd = jnp.asarray((seg[:, None] == seg[None, :]).astype(np.float32) / hd, BF16)
    sg = subln_g[l].reshape(1, vd)
    gn = hgrn_onorm_g[l].reshape(1, kd)
    q_scale = hd ** -0.5 * LOG2E

    xp2 = x_prompt.reshape(bp * tp, d)
    qb, kft, vf, kb, vbt, rp = _proj(xp2, g1, w_in_bf, qg, kg, bd, aw, rw, vd, q_scale, True, BF16, BF16, 512, bp)
    xs_pad = jnp.pad(x_sample, ((0, 0), (0, SAMPLE_PAD - ts), (0, 0))).reshape(bs * SAMPLE_PAD, d)
    qs, ksf, vsf, rs = _proj(xs_pad, g1, w_in_bf, qg, kg, bd, aw, rw, vd, q_scale, False, F32, F32, 512, bs)

    o_att = _attn(lam, qb.reshape(bp, tp, aw), kb.reshape(bp, tp, aw), vbt, biasp, sg, n_heads, hd, out_scale)
    ck = jnp.transpose(cache_k[l], (0, 2, 3, 4, 1)).reshape(cache_k.shape[1], aw, page_size)
    cv = cache_v[l].reshape(cache_v.shape[1], page_size * n_heads, vd)
    o_rec, s_p, o_att_s = _mix(page_table, lam, rp.reshape(bp, tp, 4 * rw), lb, gn, qs, ksf, vsf, biass, sg, ck, cv,
                               n_heads, hd, r_heads, kd, out_scale)
    o_rec_s, s_s = _hgrn_sample(rs, lb, gn, state_hgrn[l], r_heads, kd, ts)

    yp, a_tail = _ffn(xp2, o_att.reshape(bp * tp, aw), o_rec.reshape(bp * tp, rw), wo_bf, g2, wg_bf, wu_bf, cw, cb,
                      wd_bf, None, 512, bp)
    k_prompt = kft.reshape(bp, n_heads, 2, hd, tp).transpose(0, 4, 1, 2, 3)[None]
    v_prompt = vf.reshape(1, bp, tp, n_heads, vd)
    conv_prompt = a_tail[:, SUBLANES - (CONV_W - 1):][None]
    prev = jnp.pad(state_conv[l], ((0, 0), (0, SAMPLE_PAD - (CONV_W - 1)), (0, 0))).reshape(bs * SAMPLE_PAD, dff)
    ys, a_s = _ffn(xs_pad, o_att_s, o_rec_s, wo_bf, g2, wg_bf, wu_bf, cw, cb, wd_bf, prev, 256, 1)
    unpad = lambda z: z.reshape(bs, SAMPLE_PAD, -1)[:, :ts]
    y_sample = unpad(ys)
    k_sample = unpad(ksf).reshape(1, bs, ts, n_heads, 2, hd)
    v_sample = unpad(vsf).reshape(1, bs, ts, n_heads, vd)
    conv_sample = a_s.reshape(bs, SAMPLE_PAD, dff)[:, ts - (CONV_W - 1):ts][None]

    return (yp.reshape(bp, tp, d), y_sample, k_prompt, v_prompt, s_p[None], conv_prompt,
            k_sample, v_sample, s_s[None], conv_sample)
```

```python
import functools
import math

import numpy as np
import jax
import jax.numpy as jnp
from jax import lax
from jax.experimental import pallas as pl
from jax.experimental.pallas import tpu as pltpu

F32 = jnp.float32
BF16 = jnp.bfloat16

EPS = 1e-6
N_BUCKETS = 32
MAX_DISTANCE = 128
CONV_W = 3
NEG = -1e30
LOG2E = math.log2(math.e)
LANES = 128
SUBLANES = 8

TQ = 512
HG_CHUNK = 128
MIX_SEQS = 2
SAMPLE_PAD = 8
VMEM_LIMIT = 56 * 1024 * 1024


def _const_spec(shape, index_map):
    return pl.BlockSpec(shape, index_map, pipeline_mode=pl.Buffered(1))


def _nt(a, b):
    return lax.dot_general(a, b, (((1,), (1,)), ((), ())), preferred_element_type=F32)


def _tn(a, b):
    return lax.dot_general(a, b, (((0,), (0,)), ((), ())), preferred_element_type=F32)


def _nn(a, b):
    return jnp.dot(a, b, preferred_element_type=F32)


def _rel_bucket_np(n):
    n = np.maximum(n, 0)
    max_exact = N_BUCKETS // 2
    nf = np.maximum(n, 1).astype(np.float32)
    large = max_exact + (np.log(nf / np.float32(max_exact)) / np.float32(math.log(MAX_DISTANCE / max_exact))
                         * np.float32(N_BUCKETS - max_exact)).astype(np.int32)
    large = np.minimum(large, N_BUCKETS - 1)
    return np.where(n < max_exact, n, large).astype(np.int32)


def _tables_kernel(relb_ref, lq_ref, hlb_ref, bkp_ref, bks_ref,
                   biasp_ref, biass_ref, lam_ref, lb_ref, *, n_heads, lam_init, near_blocks):
    for h in range(n_heads):
        far = relb_ref[N_BUCKETS - 1, h]
        for kind in range(2):
            biasp_ref[h, kind] = jnp.where(bkp_ref[kind] < 0, NEG, 0.0).astype(F32)
            for (r0, c0) in near_blocks[kind]:
                bk = bkp_ref[kind, r0:r0 + LANES, c0:c0 + LANES]
                acc = jnp.where(bk < 0, NEG, 0.0).astype(F32)
                for b in range(N_BUCKETS - 1):
                    acc = jnp.where(bk == b, (relb_ref[b, h] - far) * LOG2E, acc)
                biasp_ref[h, kind, r0:r0 + LANES, c0:c0 + LANES] = acc
            rows = bks_ref.shape[1] // n_heads
            bk = bks_ref[kind, h * rows:(h + 1) * rows, :]
            acc = jnp.where(bk < 0, NEG, 0.0).astype(F32)
            for b in range(N_BUCKETS - 1):
                acc = jnp.where(bk == b, (relb_ref[b, h] - far) * LOG2E, acc)
            biass_ref[kind, h * rows:(h + 1) * rows, :] = acc
    lq = lq_ref[...].astype(F32)
    s1 = jnp.sum(lq[0:1] * lq[1:2], axis=1, keepdims=True)
    s2 = jnp.sum(lq[2:3] * lq[3:4], axis=1, keepdims=True)
    lam = jnp.exp(s1) - jnp.exp(s2) + lam_init
    lam_ref[...] = jnp.broadcast_to(lam, lam_ref.shape)
    hl = hlb_ref[...].astype(F32)
    mx = jnp.max(hl, axis=0, keepdims=True)
    e = jnp.exp(hl - mx)
    lb_ref[...] = e[0:1] / jnp.sum(e, axis=0, keepdims=True)


def _tables(rel_bias, lambda_qk_l, hgrn_lb, n_heads, n_tok, page_size, lam_init):
    c = np.arange(TQ)[:, None]
    r = np.arange(TQ)[None, :]
    diag = np.where(c <= r, _rel_bucket_np(r - c), -1)
    prev = _rel_bucket_np(TQ + r - c)
    bkp_np = np.stack([diag, prev]).astype(np.int32)
    bkp = jnp.asarray(bkp_np)
    near_blocks = tuple(
        tuple((r0, c0) for r0 in range(0, TQ, LANES) for c0 in range(0, TQ, LANES)
              if np.any((bkp_np[kind, r0:r0 + LANES, c0:c0 + LANES] >= 0)
                        & (bkp_np[kind, r0:r0 + LANES, c0:c0 + LANES] < N_BUCKETS - 1)))
        for kind in range(2))
    t = np.tile(np.arange(SAMPLE_PAD), 2 * n_heads)[:, None]
    cc = np.arange(page_size)[None, :]
    last_page = _rel_bucket_np(page_size + t - cc)
    new_page = np.where(cc <= np.minimum(t, n_tok - 1), _rel_bucket_np(t - cc), -1)
    bks = jnp.asarray(np.stack([last_page, new_page]).astype(np.int32))
    rows_s = 2 * n_heads * SAMPLE_PAD
    kern = functools.partial(_tables_kernel, n_heads=n_heads, lam_init=lam_init, near_blocks=near_blocks)
    return pl.pallas_call(
        kern,
        out_shape=(jax.ShapeDtypeStruct((n_heads, 2, TQ, TQ), F32),
                   jax.ShapeDtypeStruct((2, rows_s, page_size), F32),
                   jax.ShapeDtypeStruct((SUBLANES, LANES), F32),
                   jax.ShapeDtypeStruct((1, hgrn_lb.shape[1]), F32)),
        in_specs=[pl.BlockSpec(memory_space=pltpu.SMEM),
                  pl.BlockSpec(memory_space=pltpu.VMEM),
                  pl.BlockSpec(memory_space=pltpu.VMEM),
                  pl.BlockSpec(memory_space=pltpu.VMEM),
                  pl.BlockSpec(memory_space=pltpu.VMEM)],
        name="tables",
    )(rel_bias, lambda_qk_l, hgrn_lb, bkp, bks)


def _proj_kernel(x_ref, g1_ref, w_ref, qg_ref, kg_ref, bd_ref, *out_refs, aw, rw, vd, q_scale, emit_bf16):
    if emit_bf16:
        q_ref, kf_ref, vf_ref, kb_ref, vb_ref, r_ref = out_refs
    else:
        q_ref, kf_ref, vf_ref, r_ref = out_refs
    x = x_ref[...].astype(F32)
    ms = jnp.mean(x * x, axis=-1, keepdims=True)
    h = (x * lax.rsqrt(ms + EPS) * g1_ref[...]).astype(BF16)

    def seg_norm(z, g):
        msq = _nn((z * z).astype(BF16), bd_ref[...])
        return z * lax.rsqrt(msq + EPS) * g

    zq = _nn(h, w_ref[:, 0:aw])
    q_ref[...] = (seg_norm(zq, qg_ref[...]) * q_scale).astype(q_ref.dtype)
    zk = _nn(h, w_ref[:, aw:2 * aw])
    kn = seg_norm(zk, kg_ref[...])
    zv = _nn(h, w_ref[:, 2 * aw:3 * aw])
    if emit_bf16:
        n_heads = aw // vd
        kf_ref[0] = kn.T
        kb_ref[...] = kn.astype(BF16)
        vb_ref[0] = zv.T.astype(BF16)
        for hh in range(n_heads):
            vf_ref[pl.ds(hh, zv.shape[0], stride=n_heads), :] = zv[:, hh * vd:(hh + 1) * vd]
    else:
        kf_ref[...] = kn
        vf_ref[...] = zv
    for g in range(4):
        z = _nn(h, w_ref[:, 3 * aw + g * rw:3 * aw + (g + 1) * rw])
        r_ref[:, g * rw:(g + 1) * rw] = z.astype(r_ref.dtype)


def _proj(x2, g1, w_bf, qg, kg, bd, aw, rw, vd, q_scale, emit_bf16, q_dtype, r_dtype, tm, n_batch):
    n, d = x2.shape
    kern = functools.partial(_proj_kernel, aw=aw, rw=rw, vd=vd, q_scale=q_scale, emit_bf16=emit_bf16)
    row = lambda i: (i, 0)
    const = lambda i: (0, 0)
    if emit_bf16:
        t = n // n_batch
        per_b = t // tm
        n_heads = aw // vd
        xposed = lambda i: (i // per_b, 0, i % per_b)
        out_shape = [jax.ShapeDtypeStruct((n, aw), q_dtype),
                     jax.ShapeDtypeStruct((n_batch, aw, t), F32),
                     jax.ShapeDtypeStruct((n * n_heads, vd), F32),
                     jax.ShapeDtypeStruct((n, aw), BF16),
                     jax.ShapeDtypeStruct((n_batch, aw, t), BF16)]
        out_specs = [pl.BlockSpec((tm, aw), row),
                     pl.BlockSpec((1, aw, tm), xposed),
                     pl.BlockSpec((tm * n_heads, vd), row),
                     pl.BlockSpec((tm, aw), row),
                     pl.BlockSpec((1, aw, tm), xposed)]
    else:
        out_shape = [jax.ShapeDtypeStruct((n, aw), q_dtype),
                     jax.ShapeDtypeStruct((n, aw), F32),
                     jax.ShapeDtypeStruct((n, aw), F32)]
        out_specs = [pl.BlockSpec((tm, aw), row)] * 3
    out_shape.append(jax.ShapeDtypeStruct((n, 4 * rw), r_dtype))
    out_specs.append(pl.BlockSpec((tm, 4 * rw), row))
    return pl.pallas_call(
        kern,
        grid=(n // tm,),
        in_specs=[pl.BlockSpec((tm, d), row),
                  pl.BlockSpec((1, d), const),
                  _const_spec(w_bf.shape, const),
                  pl.BlockSpec((1, aw), const),
                  pl.BlockSpec((1, aw), const),
                  pl.BlockSpec((aw, aw), const)],
        out_specs=out_specs,
        out_shape=out_shape,
        compiler_params=pltpu.CompilerParams(dimension_semantics=("arbitrary",),
                                             vmem_limit_bytes=VMEM_LIMIT),
        name="proj_bf16" if emit_bf16 else "proj_f32",
    )(x2, g1, w_bf, qg, kg, bd)


def _attn_kernel(lam_ref, q_ref, k_ref, vt_ref, bias_ref, sg_ref, o_ref,
                 qst_sc, sa_sc, ma_sc, sb_sc, mb_sc, m_sc, l_sc, acc_sc, *, hd, out_scale):
    i = pl.program_id(2)

    qt = q_ref[0].astype(F32).T
    row = lax.broadcasted_iota(jnp.int32, qt.shape, 0)
    qst_sc[...] = jnp.concatenate([jnp.where(row < hd, qt, 0.0), jnp.where(row >= hd, qt, 0.0)],
                                  axis=1).astype(BF16)
    m_sc[...] = jnp.full(m_sc.shape, NEG, F32)
    l_sc[...] = jnp.zeros(l_sc.shape, F32)
    acc_sc[...] = jnp.zeros(acc_sc.shape, F32)

    bufs = ((sa_sc, ma_sc), (sb_sc, mb_sc))

    def issue(buf, g, with_max):
        start = pl.multiple_of(g * TQ, TQ)
        s = _nn(k_ref[0, pl.ds(start, TQ), :], qst_sc[...])
        buf[0][...] = s
        if with_max:
            buf[1][...] = jnp.max(s, axis=0, keepdims=True)

    def consume(buf, g, kind):
        s = buf[0][...]
        if kind is None:
            s_max = buf[1][...]
        else:
            bias = bias_ref[0, kind]
            s = jnp.concatenate([s[:, 0:TQ] + bias, s[:, TQ:2 * TQ] + bias], axis=1)
            s_max = jnp.max(s, axis=0, keepdims=True)
        m_old = m_sc[...]
        m_new = jnp.maximum(m_old, s_max)
        alpha = jnp.exp2(m_old - m_new)
        p = jnp.exp2(s - m_new)
        l_sc[...] = alpha * l_sc[...] + jnp.sum(p, axis=0, keepdims=True)
        start = pl.multiple_of(g * TQ, TQ)
        acc_sc[...] = alpha * acc_sc[...] + _nn(vt_ref[0, :, pl.ds(start, TQ)], p.astype(BF16))
        m_sc[...] = m_new

    n_pure = jnp.maximum(i - 1, 0)
    issue(bufs[0], 0, True)

    def pair(jj, carry):
        g = 2 * jj
        issue(bufs[1], g + 1, True)
        consume(bufs[0], g, None)
        issue(bufs[0], g + 2, True)
        consume(bufs[1], g + 1, None)
        return carry

    lax.fori_loop(0, n_pure // 2, pair, 0)
    odd = (n_pure & 1) == 1

    def finish(cur, other):
        @pl.when(i > 0)
        def _():
            issue(other, i, False)
            consume(cur, i - 1, 1)
            consume(other, i, 0)

        @pl.when(i == 0)
        def _():
            consume(cur, 0, 0)

    @pl.when(odd)
    def _():
        issue(bufs[1], n_pure, False)
        consume(bufs[0], n_pure - 1, None)
        finish(bufs[1], bufs[0])

    @pl.when(jnp.logical_not(odd))
    def _():
        finish(bufs[0], bufs[1])

    a = acc_sc[...] * (1.0 / l_sc[...])
    ot = a[:, 0:TQ] - lam_ref[0, 0] * a[:, TQ:2 * TQ]
    ms = jnp.mean(ot * ot, axis=0, keepdims=True)
    o = (ot * lax.rsqrt(ms + EPS)).T
    o_ref[0] = (o * (sg_ref[...] * out_scale)).astype(o_ref.dtype)


def _attn(lam, q, k, vt, bias, sg, n_heads, hd, out_scale):
    b, t, aw = q.shape
    vd = aw // n_heads
    assert t % TQ == 0
    kern = functools.partial(_attn_kernel, hd=hd, out_scale=out_scale)
    return pl.pallas_call(
        kern,
        grid=(b, n_heads, t // TQ),
        in_specs=[pl.BlockSpec(memory_space=pltpu.SMEM),
                  pl.BlockSpec((1, TQ, vd), lambda bi, h, i: (bi, i, h)),
                  pl.BlockSpec((1, t, vd), lambda bi, h, i: (bi, 0, h)),
                  pl.BlockSpec((1, vd, t), lambda bi, h, i: (bi, h, 0)),
                  pl.BlockSpec((1, 2, TQ, TQ), lambda bi, h, i: (h, 0, 0, 0)),
                  pl.BlockSpec((1, vd), lambda bi, h, i: (0, 0))],
        out_specs=pl.BlockSpec((1, TQ, vd), lambda bi, h, i: (bi, i, h)),
        out_shape=jax.ShapeDtypeStruct((b, t, aw), BF16),
        scratch_shapes=[pltpu.VMEM((vd, 2 * TQ), BF16),
                        pltpu.VMEM((TQ, 2 * TQ), F32),
                        pltpu.VMEM((1, 2 * TQ), F32),
                        pltpu.VMEM((TQ, 2 * TQ), F32),
                        pltpu.VMEM((1, 2 * TQ), F32),
                        pltpu.VMEM((1, 2 * TQ), F32),
                        pltpu.VMEM((1, 2 * TQ), F32),
                        pltpu.VMEM((vd, 2 * TQ), F32)],
        compiler_params=pltpu.CompilerParams(dimension_semantics=("arbitrary",) * 3,
                                             vmem_limit_bytes=VMEM_LIMIT),
        name="attn_prompt",
    )(lam, q, k, vt, bias, sg)


def _decode_seq(lam, q8, kn8, vn8, bias_ref, sg, k_refs, v_refs, o_ref, rows_out, n_heads, hd, out_scale):
    n_pages = len(k_refs)
    aw = q8.shape[1]
    vd = aw // n_heads
    ps = k_refs[0].shape[1]
    lane = lax.broadcasted_iota(jnp.int32, q8.shape, 1)
    pieces = [jnp.where((lane >= hm * hd) & (lane < (hm + 1) * hd), q8, 0.0) for hm in range(2 * n_heads)]
    qbd = jnp.concatenate(pieces, axis=0).astype(BF16)
    s_list = []
    for p in range(n_pages):
        s = _nn(qbd, k_refs[p][...].astype(BF16))
        if p == n_pages - 1:
            s = s + bias_ref[0]
        s_list.append(s)
    pad = jnp.zeros((ps - SAMPLE_PAD, aw), F32)
    k_new = jnp.concatenate([kn8, pad], axis=0).astype(BF16)
    v_new = jnp.concatenate([vn8, pad], axis=0).astype(BF16)
    s_list.append(_nt(qbd, k_new) + bias_ref[1])
    s_all = jnp.concatenate(s_list, axis=1)
    m = jnp.max(s_all, axis=1, keepdims=True)
    e = jnp.exp2(s_all - m)
    inv = 1.0 / jnp.sum(e, axis=1, keepdims=True)
    eb = e.astype(BF16)
    rows = 2 * SAMPLE_PAD
    for h in range(n_heads):
        eh = eb[h * rows:(h + 1) * rows]
        acc = _nn(eh[:, n_pages * ps:(n_pages + 1) * ps], v_new[:, h * vd:(h + 1) * vd])
        for p in range(n_pages):
            vh = v_refs[p][pl.ds(h, ps, stride=n_heads), :].astype(BF16)
            acc = acc + _nn(eh[:, p * ps:(p + 1) * ps], vh)
        acc = acc * inv[h * rows:(h + 1) * rows]
        o = acc[0:SAMPLE_PAD] - lam * acc[SAMPLE_PAD:rows]
        ms = jnp.mean(o * o, axis=-1, keepdims=True)
        o_ref[rows_out, h * vd:(h + 1) * vd] = (o * lax.rsqrt(ms + EPS) * (sg * out_scale)).astype(o_ref.dtype)


def _mid_reference(b, level):
    c, w = b.shape
    p = 1 << level
    half = p // 2
    if p >= SUBLANES:
        bp = b.reshape(c // p, p, w)
        return jnp.broadcast_to(bp[:, half - 1:half, :], (c // p, p, w)).reshape(c, w)
    b8 = b.reshape(c // SUBLANES, SUBLANES, w)
    rig = lax.broadcasted_iota(jnp.int32, b8.shape, 1)
    out = None
    for start in range(SUBLANES - p, -1, -p):
        row = jnp.broadcast_to(b8[:, start + half - 1:start + half, :], b8.shape)
        out = row if out is None else jnp.where(rig < start + p, row, out)
    return out.reshape(c, w)


def _upper_lower(up, low, level):
    c, w = up.shape
    half = (1 << level) // 2
    if half >= SUBLANES:
        shape = (c // (2 * half), 2, half, w)
        return jnp.concatenate([low.reshape(shape)[:, 0:1], up.reshape(shape)[:, 1:2]], axis=1).reshape(c, w)
    rowid = lax.broadcasted_iota(jnp.int32, up.shape, 0)
    return jnp.where((rowid & half) != 0, up, low)


def _hgrn_chunk(r_ref, r0, lb_ref, gn_ref, tri, lev, o_ref, st_sc, n_heads, kd):
    c = HG_CHUNK
    rw = n_heads * kd
    n_levels = c.bit_length() - 1
    head = lambda z, h: z[:, h * kd:(h + 1) * kd]
    q = r_ref[0, r0:r0 + c, 0:rw].astype(F32)
    rf = r_ref[0, r0:r0 + c, rw:2 * rw].astype(F32)
    v = r_ref[0, r0:r0 + c, 2 * rw:3 * rw].astype(BF16)
    rg = r_ref[0, r0:r0 + c, 3 * rw:4 * rw].astype(F32)
    lb = lb_ref[...]
    f = lb + (1.0 - lb) * jax.nn.sigmoid(rf)
    logf = jnp.log(f) * LOG2E
    kk = 1.0 - f
    hi = logf.astype(BF16)
    lo = (logf - hi.astype(F32)).astype(BF16)
    b = _nn(tri, hi) + _nn(tri, lo)
    qb = q.astype(BF16)
    kb = kk.astype(BF16)
    diag = lev == 0
    a = [jnp.where(diag, _nt(head(qb, h), head(kb, h)), 0.0) for h in range(n_heads)]
    for level in range(1, n_levels + 1):
        e = jnp.exp2(-jnp.abs(b - _mid_reference(b, level)))
        z = (_upper_lower(q, kk, level) * e).astype(BF16)
        here = lev == level
        a = [jnp.where(here, _nt(head(z, h), head(z, h)), a[h]) for h in range(n_heads)]
    b_last = b[c - 1:c, :]
    q_hat = (q * jnp.exp2(b)).astype(BF16)
    k_hat = (kk * jnp.exp2(b_last - b)).astype(BF16)
    decay = jnp.exp2(b_last)
    outs = []
    for h in range(n_heads):
        st = st_sc[h]
        o = _nn(a[h].astype(BF16), head(v, h)) + _nt(head(q_hat, h), st.astype(BF16))
        st_sc[h] = head(decay, h) * st + _tn(head(v, h), head(k_hat, h))
        ms = jnp.mean(o * o, axis=-1, keepdims=True)
        outs.append(o * lax.rsqrt(ms + EPS) * gn_ref[...])
    on = jnp.concatenate(outs, axis=1) * (rg * jax.nn.sigmoid(rg))
    o_ref[0, r0:r0 + c, :] = on.astype(o_ref.dtype)


def _mix_kernel(pt_ref, lam_ref, r_ref, lb_ref, gn_ref, tri_ref, lev_ref, q_ref, kn_ref, vn_ref, bias_ref, sg_ref,
                ck_hbm, cv_hbm, o_rec_ref, sfin_ref, o_dec_ref, st_sc, kbuf, vbuf, ksem, vsem,
                *, n_pages, n_heads, hd, r_heads, kd, out_scale, steps_per_batch):
    step = pl.program_id(0)
    slot = lax.rem(step, 2)
    t = lax.rem(step, steps_per_batch)

    def page_copies(at_step, at_slot):
        copies = []
        for j in range(MIX_SEQS):
            for p in range(n_pages):
                page = pt_ref[at_step * MIX_SEQS + j, p]
                i = j * n_pages + p
                copies.append(pltpu.make_async_copy(ck_hbm.at[page], kbuf.at[at_slot, i], ksem.at[at_slot]))
                copies.append(pltpu.make_async_copy(cv_hbm.at[page], vbuf.at[at_slot, i], vsem.at[at_slot]))
        return copies

    @pl.when(step == 0)
    def _():
        for cp in page_copies(0, 0):
            cp.start()

    @pl.when(step + 1 < pl.num_programs(0))
    def _():
        for cp in page_copies(step + 1, 1 - slot):
            cp.start()

    @pl.when(t == 0)
    def _():
        st_sc[...] = jnp.zeros(st_sc.shape, F32)

    for cp in page_copies(step, slot):
        cp.wait()

    lev = lev_ref[...]
    tri = tri_ref[...]
    lam = lam_ref[0, 0]
    for c in range(MIX_SEQS):
        _hgrn_chunk(r_ref, c * HG_CHUNK, lb_ref, gn_ref, tri, lev, o_rec_ref, st_sc, r_heads, kd)
        rows = slice(c * SAMPLE_PAD, (c + 1) * SAMPLE_PAD)
        k_refs = [kbuf.at[slot, c * n_pages + p] for p in range(n_pages)]
        v_refs = [vbuf.at[slot, c * n_pages + p] for p in range(n_pages)]
        _decode_seq(lam, q_ref[rows, :].astype(F32), kn_ref[rows, :].astype(F32), vn_ref[rows, :].astype(F32),
                    bias_ref, sg_ref[...], k_refs, v_refs, o_dec_ref, rows, n_heads, hd, out_scale)

    @pl.when(t == steps_per_batch - 1)
    def _():
        for h in range(r_heads):
            sfin_ref[0, h] = st_sc[h].T


def _mix(page_table, lam, r, lb, gn, q, kn, vn, bias, sg, ck, cv, n_heads, hd, r_heads, kd, out_scale):
    n_seq, n_pages = page_table.shape
    b, t, _ = r.shape
    _, aw, ps = ck.shape
    vd = aw // n_heads
    c = HG_CHUNK
    tokens = MIX_SEQS * c
    steps_per_batch = t // tokens
    n_steps = b * steps_per_batch
    assert t % tokens == 0 and n_seq == n_steps * MIX_SEQS
    idx = np.arange(c)
    x = idx[:, None] ^ idx[None, :]
    lev = np.where(idx[:, None] > idx[None, :], np.floor(np.log2(np.maximum(x, 1))).astype(np.int32) + 1, -1)
    lev = np.where(idx[:, None] == idx[None, :], 0, lev).astype(np.int32)
    tri = (idx[:, None] >= idx[None, :]).astype(np.float32)
    kern = functools.partial(_mix_kernel, n_pages=n_pages, n_heads=n_heads, hd=hd, r_heads=r_heads, kd=kd,
                             out_scale=out_scale, steps_per_batch=steps_per_batch)
    c2 = lambda s, pt: (0, 0)
    tok = lambda s, pt: (s // steps_per_batch, s % steps_per_batch, 0)
    seq_rows = lambda s, pt: (s, 0)
    n_k = MIX_SEQS * n_pages
    rows = MIX_SEQS * SAMPLE_PAD
    grid_spec = pltpu.PrefetchScalarGridSpec(
        num_scalar_prefetch=1,
        grid=(n_steps,),
        in_specs=[pl.BlockSpec(memory_space=pltpu.SMEM),
                  pl.BlockSpec((1, tokens, r.shape[2]), tok),
                  pl.BlockSpec((1, r_heads * kd), c2),
                  pl.BlockSpec((1, kd), c2),
                  pl.BlockSpec((c, c), c2),
                  pl.BlockSpec((c, c), c2),
                  pl.BlockSpec((rows, aw), seq_rows),
                  pl.BlockSpec((rows, aw), seq_rows),
                  pl.BlockSpec((rows, aw), seq_rows),
                  pl.BlockSpec(bias.shape, lambda s, pt: (0, 0, 0)),
                  pl.BlockSpec((1, vd), c2),
                  pl.BlockSpec(memory_space=pl.ANY),
                  pl.BlockSpec(memory_space=pl.ANY)],
        out_specs=[pl.BlockSpec((1, tokens, r_heads * kd), tok),
                   pl.BlockSpec((1, r_heads, kd, kd), lambda s, pt: (s // steps_per_batch, 0, 0, 0)),
                   pl.BlockSpec((rows, aw), seq_rows)],
        scratch_shapes=[pltpu.VMEM((r_heads, kd, kd), F32),
                        pltpu.VMEM((2, n_k) + ck.shape[1:], ck.dtype),
                        pltpu.VMEM((2, n_k) + cv.shape[1:], cv.dtype),
                        pltpu.SemaphoreType.DMA((2,)),
                        pltpu.SemaphoreType.DMA((2,))],
    )
    return pl.pallas_call(
        kern,
        grid_spec=grid_spec,
        out_shape=[jax.ShapeDtypeStruct((b, t, r_heads * kd), BF16),
                   jax.ShapeDtypeStruct((b, r_heads, kd, kd), F32),
                   jax.ShapeDtypeStruct((n_seq * SAMPLE_PAD, aw), F32)],
        compiler_params=pltpu.CompilerParams(dimension_semantics=("arbitrary",),
                                             vmem_limit_bytes=VMEM_LIMIT),
        name="hgrn_prompt_attn_decode",
    )(page_table, lam, r, lb, gn, jnp.asarray(tri, BF16), jnp.asarray(lev), q, kn, vn, bias, sg, ck, cv)


def _hgrn_step_kernel(r_ref, lb_ref, gn_ref, s_ref, o_ref, so_ref, *, n_heads, kd, n_tok, n_seq):
    rw = n_heads * kd
    rowid = lax.broadcasted_iota(jnp.int32, (SAMPLE_PAD, kd), 0)
    live = rowid < n_tok
    zpad = jnp.zeros((SAMPLE_PAD, kd), F32)
    pad16 = lambda z: jnp.concatenate([z, zpad], axis=0).astype(BF16)
    ones16 = jnp.ones((2 * SAMPLE_PAD, kd), BF16)
    for g in range(n_seq):
        rows = slice(g * SAMPLE_PAD, (g + 1) * SAMPLE_PAD)
        for h in range(n_heads):
            cols = slice(h * kd, (h + 1) * kd)
            q = r_ref[rows, h * kd:(h + 1) * kd]
            rf = r_ref[rows, rw + h * kd:rw + (h + 1) * kd]
            v = r_ref[rows, 2 * rw + h * kd:2 * rw + (h + 1) * kd]
            rg = r_ref[rows, 3 * rw + h * kd:3 * rw + (h + 1) * kd]
            lb = lb_ref[:, cols]
            f = lb + (1.0 - lb) * jax.nn.sigmoid(rf)
            kk = jnp.where(live, 1.0 - f, 0.0)
            b = jnp.where(live, jnp.log(f) * LOG2E, 0.0)
            for sh in (1, 2, 4):
                b = b + jnp.where(rowid >= sh, pltpu.roll(b, sh, 0), 0.0)
            s0 = s_ref[g, h]
            o = jnp.sum(q * kk, axis=1, keepdims=True) * v
            for d in range(1, n_tok):
                x = q * pltpu.roll(kk, d, 0) * jnp.exp2(b - pltpu.roll(b, d, 0))
                w = jnp.sum(jnp.where(rowid >= d, x, 0.0), axis=1, keepdims=True)
                o = o + w * pltpu.roll(v, d, 0)
            o = o + _nn(pad16(q * jnp.exp2(b)), s0.astype(BF16))[0:SAMPLE_PAD]
            b_last = b[n_tok - 1:n_tok, :]
            decay = jnp.exp2(b_last)
            d_hi = decay.astype(BF16).astype(F32)
            d_split = jnp.where(rowid == 0, d_hi, jnp.where(rowid == 1, decay - d_hi, 0.0))
            decay_cols = _tn(pad16(d_split), ones16)
            so_ref[g, h] = decay_cols * s0 + _tn(pad16(kk * jnp.exp2(b_last - b)), pad16(v))
            ms = jnp.mean(o * o, axis=-1, keepdims=True)
            on = o * lax.rsqrt(ms + EPS) * gn_ref[...] * (rg * jax.nn.sigmoid(rg))
            o_ref[rows, cols] = on.astype(o_ref.dtype)


def _hgrn_sample(r, lb, gn, state, n_heads, kd, n_tok, seq_per_step=2):
    n_seq = state.shape[0]
    kern = functools.partial(_hgrn_step_kernel, n_heads=n_heads, kd=kd, n_tok=n_tok, n_seq=seq_per_step)
    rows = seq_per_step * SAMPLE_PAD
    return pl.pallas_call(
        kern,
        grid=(n_seq // seq_per_step,),
        in_specs=[pl.BlockSpec((rows, r.shape[1]), lambda i: (i, 0)),
                  pl.BlockSpec((1, n_heads * kd), lambda i: (0, 0)),
                  pl.BlockSpec((1, kd), lambda i: (0, 0)),
                  pl.BlockSpec((seq_per_step, n_heads, kd, kd), lambda i: (i, 0, 0, 0))],
        out_specs=[pl.BlockSpec((rows, n_heads * kd), lambda i: (i, 0)),
                   pl.BlockSpec((seq_per_step, n_heads, kd, kd), lambda i: (i, 0, 0, 0))],
        out_shape=[jax.ShapeDtypeStruct((n_seq * SAMPLE_PAD, n_heads * kd), F32),
                   jax.ShapeDtypeStruct(state.shape, F32)],
        compiler_params=pltpu.CompilerParams(dimension_semantics=("arbitrary",),
                                             vmem_limit_bytes=VMEM_LIMIT),
        name="hgrn_sample",
    )(r, lb, gn, state)


def _ffn_kernel(*refs, sample, aw):
    if sample:
        (x_ref, oa_ref, or_ref, wo_ref, g2_ref, wg_ref, wu_ref, cw_ref, cb_ref, wd_ref, prev_ref,
         y_ref, a_ref) = refs
    else:
        (x_ref, oa_ref, or_ref, wo_ref, g2_ref, wg_ref, wu_ref, cw_ref, cb_ref, wd_ref,
         y_ref, a_ref, carry_sc) = refs

        @pl.when(pl.program_id(1) == 0)
        def _():
            carry_sc[...] = jnp.zeros(carry_sc.shape, F32)

    x = x_ref[...].astype(F32)
    x1 = x + _nn(oa_ref[...].astype(BF16), wo_ref[0:aw, :]) + _nn(or_ref[...].astype(BF16), wo_ref[aw:, :])
    ms = jnp.mean(x1 * x1, axis=-1, keepdims=True)
    h2 = (x1 * lax.rsqrt(ms + EPS) * g2_ref[...]).astype(BF16)
    tm = x.shape[0]
    a = _nn(h2, wg_ref[...])
    u = _nn(h2, wu_ref[...])
    rowid = lax.broadcasted_iota(jnp.int32, a.shape, 0)
    am1 = pltpu.roll(a, 1, 0)
    am2 = pltpu.roll(a, 2, 0)
    if sample:
        rig = rowid & (SAMPLE_PAD - 1)
        prev = prev_ref[...]
        am1 = jnp.where(rig == 0, pltpu.roll(prev, tm - 1, 0), am1)
        am2 = jnp.where(rig < 2, prev, am2)
        a_ref[...] = a
    else:
        last = carry_sc[...]
        am1 = jnp.where(rowid == 0, last[SUBLANES - 1:SUBLANES, :], am1)
        am2 = jnp.where(rowid == 0, last[SUBLANES - 2:SUBLANES - 1, :],
                        jnp.where(rowid == 1, last[SUBLANES - 1:SUBLANES, :], am2))
        tail = a[tm - SUBLANES:tm, :]
        carry_sc[...] = tail
        a_ref[0] = tail
    cw = cw_ref[...]
    cv = cb_ref[...] + cw[0:1, :] * am2 + cw[1:2, :] * am1 + cw[2:3, :] * a
    gate = (cv * jax.nn.sigmoid(cv) * u).astype(BF16)
    y_ref[...] = x1 + _nn(gate, wd_ref[...])


def _ffn(x2, oa, orec, wo, g2, wg, wu, cw, cb, wd, prev, tm, n_batch):
    n, d = x2.shape
    aw = oa.shape[1]
    dff = wg.shape[1]
    sample = prev is not None
    kern = functools.partial(_ffn_kernel, sample=sample, aw=aw)
    scratch = []
    if sample:
        grid = (n // tm,)
        row = lambda i: (i, 0)
        c2 = lambda i: (0, 0)
        extra_in = [pl.BlockSpec((tm, dff), row)]
        extra_args = [prev]
        a_shape = jax.ShapeDtypeStruct((n, dff), F32)
        a_spec = pl.BlockSpec((tm, dff), row)
        sem = ("arbitrary",)
    else:
        per_b = n // n_batch // tm
        grid = (n_batch, per_b)
        row = lambda b, i: (b * per_b + i, 0)
        c2 = lambda b, i: (0, 0)
        extra_in = []
        extra_args = []
        a_shape = jax.ShapeDtypeStruct((n_batch, SUBLANES, dff), F32)
        a_spec = pl.BlockSpec((1, SUBLANES, dff), lambda b, i: (b, 0, 0))
        scratch.append(pltpu.VMEM((SUBLANES, dff), F32))
        sem = ("arbitrary", "arbitrary")
    return pl.pallas_call(
        kern,
        grid=grid,
        in_specs=[pl.BlockSpec((tm, d), row),
                  pl.BlockSpec((tm, aw), row),
                  pl.BlockSpec((tm, aw), row),
                  _const_spec(wo.shape, c2),
                  pl.BlockSpec((1, d), c2),
                  _const_spec(wg.shape, c2),
                  _const_spec(wu.shape, c2),
                  pl.BlockSpec(cw.shape, c2),
                  pl.BlockSpec(cb.shape, c2),
                  _const_spec(wd.shape, c2)] + extra_in,
        out_specs=[pl.BlockSpec((tm, d), row), a_spec],
        out_shape=[jax.ShapeDtypeStruct((n, d), F32), a_shape],
        scratch_shapes=scratch,
        compiler_params=pltpu.CompilerParams(dimension_semantics=sem, vmem_limit_bytes=VMEM_LIMIT),
        name="ffn_sample" if sample else "ffn_prompt",
    )(x2, oa, orec, wo, g2, wg, wu, cw, cb, wd, *extra_args)


def kernel(x_prompt, x_sample, cache_k, cache_v, page_table, state_hgrn, state_conv, rel_bias, norm1_g, w_in,
           qk_norm_g, lambda_qk, subln_g, hgrn_lb, hgrn_onorm_g, w_out, norm2_g, w_gate, w_up, conv_w, conv_b,
           w_down):
    bp, tp, d = x_prompt.shape
    bs, ts, _ = x_sample.shape
    depth = w_in.shape[0]
    assert depth == 1 and CONV_W - 1 <= ts <= SAMPLE_PAD and conv_w.shape[1] == CONV_W
    n_heads, _, hd = cache_k.shape[3:]
    vd = cache_v.shape[4]
    aw = n_heads * vd
    r_heads, kd = state_hgrn.shape[2], state_hgrn.shape[3]
    rw = r_heads * kd
    dff = w_gate.shape[2]
    page_size = cache_k.shape[2]
    assert page_size >= MAX_DISTANCE and tp % TQ == 0
    l = 0
    lam_init = 0.8 - 0.6 * math.exp(-0.3 * l)
    out_scale = 1.0 - lam_init

    biasp, biass, lam, lb = _tables(rel_bias, lambda_qk[l], hgrn_lb, n_heads, ts, page_size, lam_init)

    w_in_bf = w_in[l].astype(BF16)
    wo_bf = w_out[l].astype(BF16)
    wg_bf = w_gate[l].astype(BF16)
    wu_bf = w_up[l].astype(BF16)
    wd_bf = w_down[l].astype(BF16)
    cw = conv_w[l]
    cb = conv_b[l].reshape(1, dff)
    g1 = norm1_g[l].reshape(1, d)
    g2 = norm2_g[l].reshape(1, d)
    qg = jnp.tile(qk_norm_g[l, 0].reshape(1, 2 * hd), (1, n_heads))
    kg = jnp.tile(qk_norm_g[l, 1].reshape(1, 2 * hd), (1, n_heads))
    seg = np.arange(aw) // hd
    bd = jnp.asarray((seg[:, None] == seg[None, :]).astype(np.float32) / hd, BF16)
    sg = subln_g[l].reshape(1, vd)
    gn = hgrn_onorm_g[l].reshape(1, kd)
    q_scale = hd ** -0.5 * LOG2E

    xp2 = x_prompt.reshape(bp * tp, d)
    qb, kft, vf, kb, vbt, rp = _proj(xp2, g1, w_in_bf, qg, kg, bd, aw, rw, vd, q_scale, True, BF16, BF16, 512, bp)
    xs_pad = jnp.pad(x_sample, ((0, 0), (0, SAMPLE_PAD - ts), (0, 0))).reshape(bs * SAMPLE_PAD, d)
    qs, ksf, vsf, rs = _proj(xs_pad, g1, w_in_bf, qg, kg, bd, aw, rw, vd, q_scale, False, F32, F32, 512, bs)

    o_att = _attn(lam, qb.reshape(bp, tp, aw), kb.reshape(bp, tp, aw), vbt, biasp, sg, n_heads, hd, out_scale)
    ck = jnp.transpose(cache_k[l], (0, 2, 3, 4, 1)).reshape(cache_k.shape[1], aw, page_size)
    cv = cache_v[l].reshape(cache_v.shape[1], page_size * n_heads, vd)
    o_rec, s_p, o_att_s = _mix(page_table, lam, rp.reshape(bp, tp, 4 * rw), lb, gn, qs, ksf, vsf, biass, sg, ck, cv,
                               n_heads, hd, r_heads, kd, out_scale)
    o_rec_s, s_s = _hgrn_sample(rs, lb, gn, state_hgrn[l], r_heads, kd, ts)

    yp, a_tail = _ffn(xp2, o_att.reshape(bp * tp, aw), o_rec.reshape(bp * tp, rw), wo_bf, g2, wg_bf, wu_bf, cw, cb,
                      wd_bf, None, 512, bp)
    k_prompt = kft.reshape(bp, n_heads, 2, hd, tp).transpose(0, 4, 1, 2, 3)[None]
    v_prompt = vf.reshape(1, bp, tp, n_heads, vd)
    conv_prompt = a_tail[:, SUBLANES - (CONV_W - 1):][None]
    prev = jnp.pad(state_conv[l], ((0, 0), (0, SAMPLE_PAD - (CONV_W - 1)), (0, 0))).reshape(bs * SAMPLE_PAD, dff)
    ys, a_s = _ffn(xs_pad, o_att_s, o_rec_s, wo_bf, g2, wg_bf, wu_bf, cw, cb, wd_bf, prev, 256, 1)
    unpad = lambda z: z.reshape(bs, SAMPLE_PAD, -1)[:, :ts]
    y_sample = unpad(ys)
    k_sample = unpad(ksf).reshape(1, bs, ts, n_heads, 2, hd)
    v_sample = unpad(vsf).reshape(1, bs, ts, n_heads, vd)
    conv_sample = a_s.reshape(bs, SAMPLE_PAD, dff)[:, ts - (CONV_W - 1):ts][None]

    return (yp.reshape(bp, tp, d), y_sample, k_prompt, v_prompt, s_p[None], conv_prompt,
            k_sample, v_sample, s_s[None], conv_sample)
```

```python
import functools
import math

import numpy as np
import jax
import jax.numpy as jnp
from jax import lax
from jax.experimental import pallas as pl
from jax.experimental.pallas import tpu as pltpu

F32 = jnp.float32
BF16 = jnp.bfloat16

EPS = 1e-6
N_BUCKETS = 32
MAX_DISTANCE = 128
CONV_W = 3
NEG = -1e30
LOG2E = math.log2(math.e)
LANES = 128
SUBLANES = 8

TQ = 512
HG_CHUNK = 128
MIX_SEQS = 2
SAMPLE_PAD = 8
VMEM_LIMIT = 56 * 1024 * 1024


def _const_spec(shape, index_map):
    return pl.BlockSpec(shape, index_map, pipeline_mode=pl.Buffered(1))


def _nt(a, b):
    return lax.dot_general(a, b, (((1,), (1,)), ((), ())), preferred_element_type=F32)


def _tn(a, b):
    return lax.dot_general(a, b, (((0,), (0,)), ((), ())), preferred_element_type=F32)


def _nn(a, b):
    return jnp.dot(a, b, preferred_element_type=F32)


def _rel_bucket_np(n):
    n = np.maximum(n, 0)
    max_exact = N_BUCKETS // 2
    nf = np.maximum(n, 1).astype(np.float32)
    large = max_exact + (np.log(nf / np.float32(max_exact)) / np.float32(math.log(MAX_DISTANCE / max_exact))
                         * np.float32(N_BUCKETS - max_exact)).astype(np.int32)
    large = np.minimum(large, N_BUCKETS - 1)
    return np.where(n < max_exact, n, large).astype(np.int32)


def _tables_kernel(relb_ref, lq_ref, hlb_ref, bkp_ref, bks_ref,
                   biasp_ref, biass_ref, lam_ref, lb_ref, *, n_heads, lam_init, near_blocks):
    for h in range(n_heads):
        far = relb_ref[N_BUCKETS - 1, h]
        for kind in range(2):
            biasp_ref[h, kind] = jnp.where(bkp_ref[kind] < 0, NEG, 0.0).astype(F32)
            for (r0, c0) in near_blocks[kind]:
                bk = bkp_ref[kind, r0:r0 + LANES, c0:c0 + LANES]
                acc = jnp.where(bk < 0, NEG, 0.0).astype(F32)
                for b in range(N_BUCKETS - 1):
                    acc = jnp.where(bk == b, (relb_ref[b, h] - far) * LOG2E, acc)
                biasp_ref[h, kind, r0:r0 + LANES, c0:c0 + LANES] = acc
            rows = bks_ref.shape[1] // n_heads
            bk = bks_ref[kind, h * rows:(h + 1) * rows, :]
            acc = jnp.where(bk < 0, NEG, 0.0).astype(F32)
            for b in range(N_BUCKETS - 1):
                acc = jnp.where(bk == b, (relb_ref[b, h] - far) * LOG2E, acc)
            biass_ref[kind, h * rows:(h + 1) * rows, :] = acc
    lq = lq_ref[...].astype(F32)
    s1 = jnp.sum(lq[0:1] * lq[1:2], axis=1, keepdims=True)
    s2 = jnp.sum(lq[2:3] * lq[3:4], axis=1, keepdims=True)
    lam = jnp.exp(s1) - jnp.exp(s2) + lam_init
    lam_ref[...] = jnp.broadcast_to(lam, lam_ref.shape)
    hl = hlb_ref[...].astype(F32)
    mx = jnp.max(hl, axis=0, keepdims=True)
    e = jnp.exp(hl - mx)
    lb_ref[...] = e[0:1] / jnp.sum(e, axis=0, keepdims=True)


def _tables(rel_bias, lambda_qk_l, hgrn_lb, n_heads, n_tok, page_size, lam_init):
    c = np.arange(TQ)[:, None]
    r = np.arange(TQ)[None, :]
    diag = np.where(c <= r, _rel_bucket_np(r - c), -1)
    prev = _rel_bucket_np(TQ + r - c)
    bkp_np = np.stack([diag, prev]).astype(np.int32)
    bkp = jnp.asarray(bkp_np)
    near_blocks = tuple(
        tuple((r0, c0) for r0 in range(0, TQ, LANES) for c0 in range(0, TQ, LANES)
              if np.any((bkp_np[kind, r0:r0 + LANES, c0:c0 + LANES] >= 0)
                        & (bkp_np[kind, r0:r0 + LANES, c0:c0 + LANES] < N_BUCKETS - 1)))
        for kind in range(2))
    t = np.tile(np.arange(SAMPLE_PAD), 2 * n_heads)[:, None]
    cc = np.arange(page_size)[None, :]
    last_page = _rel_bucket_np(page_size + t - cc)
    new_page = np.where(cc <= np.minimum(t, n_tok - 1), _rel_bucket_np(t - cc), -1)
    bks = jnp.asarray(np.stack([last_page, new_page]).astype(np.int32))
    rows_s = 2 * n_heads * SAMPLE_PAD
    kern = functools.partial(_tables_kernel, n_heads=n_heads, lam_init=lam_init, near_blocks=near_blocks)
    return pl.pallas_call(
        kern,
        out_shape=(jax.ShapeDtypeStruct((n_heads, 2, TQ, TQ), F32),
                   jax.ShapeDtypeStruct((2, rows_s, page_size), F32),
                   jax.ShapeDtypeStruct((SUBLANES, LANES), F32),
                   jax.ShapeDtypeStruct((1, hgrn_lb.shape[1]), F32)),
        in_specs=[pl.BlockSpec(memory_space=pltpu.SMEM),
                  pl.BlockSpec(memory_space=pltpu.VMEM),
                  pl.BlockSpec(memory_space=pltpu.VMEM),
                  pl.BlockSpec(memory_space=pltpu.VMEM),
                  pl.BlockSpec(memory_space=pltpu.VMEM)],
        name="tables",
    )(rel_bias, lambda_qk_l, hgrn_lb, bkp, bks)


def _proj_kernel(x_ref, g1_ref, w_ref, qg_ref, kg_ref, bd_ref, *out_refs, aw, rw, vd, q_scale, emit_bf16):
    if emit_bf16:
        q_ref, kf_ref, vf_ref, kb_ref, vb_ref, r_ref = out_refs
    else:
        q_ref, kf_ref, vf_ref, r_ref = out_refs
    x = x_ref[...].astype(F32)
    ms = jnp.mean(x * x, axis=-1, keepdims=True)
    h = (x * lax.rsqrt(ms + EPS) * g1_ref[...]).astype(BF16)

    def seg_norm(z, g):
        msq = _nn((z * z).astype(BF16), bd_ref[...])
        return z * lax.rsqrt(msq + EPS) * g

    zq = _nn(h, w_ref[:, 0:aw])
    qn = seg_norm(zq, qg_ref[...]) * q_scale
    if emit_bf16:
        q_ref[0] = qn.T.astype(q_ref.dtype)
    else:
        q_ref[...] = qn.astype(q_ref.dtype)
    zk = _nn(h, w_ref[:, aw:2 * aw])
    kn = seg_norm(zk, kg_ref[...])
    zv = _nn(h, w_ref[:, 2 * aw:3 * aw])
    if emit_bf16:
        n_heads = aw // vd
        kf_ref[0] = kn.T
        kb_ref[...] = kn.astype(BF16)
        vb_ref[0] = zv.T.astype(BF16)
        for hh in range(n_heads):
            vf_ref[pl.ds(hh, zv.shape[0], stride=n_heads), :] = zv[:, hh * vd:(hh + 1) * vd]
    else:
        kf_ref[...] = kn
        vf_ref[...] = zv
    for g in range(4):
        z = _nn(h, w_ref[:, 3 * aw + g * rw:3 * aw + (g + 1) * rw])
        r_ref[:, g * rw:(g + 1) * rw] = z.astype(r_ref.dtype)


def _proj(x2, g1, w_bf, qg, kg, bd, aw, rw, vd, q_scale, emit_bf16, q_dtype, r_dtype, tm, n_batch):
    n, d = x2.shape
    kern = functools.partial(_proj_kernel, aw=aw, rw=rw, vd=vd, q_scale=q_scale, emit_bf16=emit_bf16)
    row = lambda i: (i, 0)
    const = lambda i: (0, 0)
    if emit_bf16:
        t = n // n_batch
        per_b = t // tm
        n_heads = aw // vd
        xposed = lambda i: (i // per_b, 0, i % per_b)
        out_shape = [jax.ShapeDtypeStruct((n_batch, aw, t), q_dtype),
                     jax.ShapeDtypeStruct((n_batch, aw, t), F32),
                     jax.ShapeDtypeStruct((n * n_heads, vd), F32),
                     jax.ShapeDtypeStruct((n, aw), BF16),
                     jax.ShapeDtypeStruct((n_batch, aw, t), BF16)]
        out_specs = [pl.BlockSpec((1, aw, tm), xposed),
                     pl.BlockSpec((1, aw, tm), xposed),
                     pl.BlockSpec((tm * n_heads, vd), row),
                     pl.BlockSpec((tm, aw), row),
                     pl.BlockSpec((1, aw, tm), xposed)]
    else:
        out_shape = [jax.ShapeDtypeStruct((n, aw), q_dtype),
                     jax.ShapeDtypeStruct((n, aw), F32),
                     jax.ShapeDtypeStruct((n, aw), F32)]
        out_specs = [pl.BlockSpec((tm, aw), row)] * 3
    out_shape.append(jax.ShapeDtypeStruct((n, 4 * rw), r_dtype))
    out_specs.append(pl.BlockSpec((tm, 4 * rw), row))
    return pl.pallas_call(
        kern,
        grid=(n // tm,),
        in_specs=[pl.BlockSpec((tm, d), row),
                  pl.BlockSpec((1, d), const),
                  _const_spec(w_bf.shape, const),
                  pl.BlockSpec((1, aw), const),
                  pl.BlockSpec((1, aw), const),
                  pl.BlockSpec((aw, aw), const)],
        out_specs=out_specs,
        out_shape=out_shape,
        compiler_params=pltpu.CompilerParams(dimension_semantics=("arbitrary",),
                                             vmem_limit_bytes=VMEM_LIMIT),
        name="proj_bf16" if emit_bf16 else "proj_f32",
    )(x2, g1, w_bf, qg, kg, bd)


def _attn_kernel(lam_ref, q_ref, k_ref, vt_ref, bias_ref, sg_ref, o_ref,
                 qst_sc, sa_sc, ma_sc, sb_sc, mb_sc, m_sc, l_sc, acc_sc, *, hd, out_scale):
    i = pl.program_id(2)

    qt = q_ref[0].astype(F32)
    row = lax.broadcasted_iota(jnp.int32, qt.shape, 0)
    qst_sc[...] = jnp.concatenate([jnp.where(row < hd, qt, 0.0), jnp.where(row >= hd, qt, 0.0)],
                                  axis=1).astype(BF16)
    m_sc[...] = jnp.full(m_sc.shape, NEG, F32)
    l_sc[...] = jnp.zeros(l_sc.shape, F32)
    acc_sc[...] = jnp.zeros(acc_sc.shape, F32)

    bufs = ((sa_sc, ma_sc), (sb_sc, mb_sc))

    def issue(buf, g, with_max):
        start = pl.multiple_of(g * TQ, TQ)
        s = _nn(k_ref[0, pl.ds(start, TQ), :], qst_sc[...])
        buf[0][...] = s
        if with_max:
            buf[1][...] = jnp.max(s, axis=0, keepdims=True)

    def consume(buf, g, kind):
        s = buf[0][...]
        if kind is None:
            s_max = buf[1][...]
        else:
            bias = bias_ref[0, kind]
            s = jnp.concatenate([s[:, 0:TQ] + bias, s[:, TQ:2 * TQ] + bias], axis=1)
            s_max = jnp.max(s, axis=0, keepdims=True)
        m_old = m_sc[...]
        m_new = jnp.maximum(m_old, s_max)
        alpha = jnp.exp2(m_old - m_new)
        p = jnp.exp2(s - m_new)
        l_sc[...] = alpha * l_sc[...] + jnp.sum(p, axis=0, keepdims=True)
        start = pl.multiple_of(g * TQ, TQ)
        acc_sc[...] = alpha * acc_sc[...] + _nn(vt_ref[0, :, pl.ds(start, TQ)], p.astype(BF16))
        m_sc[...] = m_new

    n_pure = jnp.maximum(i - 1, 0)
    issue(bufs[0], 0, True)

    def pair(jj, carry):
        g = 2 * jj
        issue(bufs[1], g + 1, True)
        consume(bufs[0], g, None)
        issue(bufs[0], g + 2, True)
        consume(bufs[1], g + 1, None)
        return carry

    lax.fori_loop(0, n_pure // 2, pair, 0)
    odd = (n_pure & 1) == 1

    def finish(cur, other):
        @pl.when(i > 0)
        def _():
            issue(other, i, False)
            consume(cur, i - 1, 1)
            consume(other, i, 0)

        @pl.when(i == 0)
        def _():
            consume(cur, 0, 0)

    @pl.when(odd)
    def _():
        issue(bufs[1], n_pure, False)
        consume(bufs[0], n_pure - 1, None)
        finish(bufs[1], bufs[0])

    @pl.when(jnp.logical_not(odd))
    def _():
        finish(bufs[0], bufs[1])

    a = acc_sc[...] * (1.0 / l_sc[...])
    ot = a[:, 0:TQ] - lam_ref[0, 0] * a[:, TQ:2 * TQ]
    ms = jnp.mean(ot * ot, axis=0, keepdims=True)
    o = (ot * lax.rsqrt(ms + EPS)).T
    o_ref[0] = (o * (sg_ref[...] * out_scale)).astype(o_ref.dtype)


def _attn(lam, qt, k, vt, bias, sg, n_heads, hd, out_scale):
    b, t, aw = k.shape
    vd = aw // n_heads
    assert t % TQ == 0
    kern = functools.partial(_attn_kernel, hd=hd, out_scale=out_scale)
    return pl.pallas_call(
        kern,
        grid=(b, n_heads, t // TQ),
        in_specs=[pl.BlockSpec(memory_space=pltpu.SMEM),
                  pl.BlockSpec((1, vd, TQ), lambda bi, h, i: (bi, h, i)),
                  pl.BlockSpec((1, t, vd), lambda bi, h, i: (bi, 0, h)),
                  pl.BlockSpec((1, vd, t), lambda bi, h, i: (bi, h, 0)),
                  pl.BlockSpec((1, 2, TQ, TQ), lambda bi, h, i: (h, 0, 0, 0)),
                  pl.BlockSpec((1, vd), lambda bi, h, i: (0, 0))],
        out_specs=pl.BlockSpec((1, TQ, vd), lambda bi, h, i: (bi, i, h)),
        out_shape=jax.ShapeDtypeStruct((b, t, aw), BF16),
        scratch_shapes=[pltpu.VMEM((vd, 2 * TQ), BF16),
                        pltpu.VMEM((TQ, 2 * TQ), F32),
                        pltpu.VMEM((1, 2 * TQ), F32),
                        pltpu.VMEM((TQ, 2 * TQ), F32),
                        pltpu.VMEM((1, 2 * TQ), F32),
                        pltpu.VMEM((1, 2 * TQ), F32),
                        pltpu.VMEM((1, 2 * TQ), F32),
                        pltpu.VMEM((vd, 2 * TQ), F32)],
        compiler_params=pltpu.CompilerParams(dimension_semantics=("arbitrary",) * 3,
                                             vmem_limit_bytes=VMEM_LIMIT),
        name="attn_prompt",
    )(lam, qt, k, vt, bias, sg)


def _decode_seq(lam, q8, kn8, vn8, bias_ref, sg, k_refs, v_refs, o_ref, rows_out, n_heads, hd, out_scale):
    n_pages = len(k_refs)
    aw = q8.shape[1]
    vd = aw // n_heads
    ps = k_refs[0].shape[1]
    lane = lax.broadcasted_iota(jnp.int32, q8.shape, 1)
    pieces = [jnp.where((lane >= hm * hd) & (lane < (hm + 1) * hd), q8, 0.0) for hm in range(2 * n_heads)]
    qbd = jnp.concatenate(pieces, axis=0).astype(BF16)
    s_list = []
    for p in range(n_pages):
        s = _nn(qbd, k_refs[p][...].astype(BF16))
        if p == n_pages - 1:
            s = s + bias_ref[0]
        s_list.append(s)
    pad = jnp.zeros((ps - SAMPLE_PAD, aw), F32)
    k_new = jnp.concatenate([kn8, pad], axis=0).astype(BF16)
    v_new = jnp.concatenate([vn8, pad], axis=0).astype(BF16)
    s_list.append(_nt(qbd, k_new) + bias_ref[1])
    s_all = jnp.concatenate(s_list, axis=1)
    m = jnp.max(s_all, axis=1, keepdims=True)
    e = jnp.exp2(s_all - m)
    inv = 1.0 / jnp.sum(e, axis=1, keepdims=True)
    eb = e.astype(BF16)
    rows = 2 * SAMPLE_PAD
    for h in range(n_heads):
        eh = eb[h * rows:(h + 1) * rows]
        acc = _nn(eh[:, n_pages * ps:(n_pages + 1) * ps], v_new[:, h * vd:(h + 1) * vd])
        for p in range(n_pages):
            vh = v_refs[p][pl.ds(h, ps, stride=n_heads), :].astype(BF16)
            acc = acc + _nn(eh[:, p * ps:(p + 1) * ps], vh)
        acc = acc * inv[h * rows:(h + 1) * rows]
        o = acc[0:SAMPLE_PAD] - lam * acc[SAMPLE_PAD:rows]
        ms = jnp.mean(o * o, axis=-1, keepdims=True)
        o_ref[rows_out, h * vd:(h + 1) * vd] = (o * lax.rsqrt(ms + EPS) * (sg * out_scale)).astype(o_ref.dtype)


def _mid_reference(b, level):
    c, w = b.shape
    p = 1 << level
    half = p // 2
    if p >= SUBLANES:
        bp = b.reshape(c // p, p, w)
        return jnp.broadcast_to(bp[:, half - 1:half, :], (c // p, p, w)).reshape(c, w)
    b8 = b.reshape(c // SUBLANES, SUBLANES, w)
    rig = lax.broadcasted_iota(jnp.int32, b8.shape, 1)
    out = None
    for start in range(SUBLANES - p, -1, -p):
        row = jnp.broadcast_to(b8[:, start + half - 1:start + half, :], b8.shape)
        out = row if out is None else jnp.where(rig < start + p, row, out)
    return out.reshape(c, w)


def _upper_lower(up, low, level):
    c, w = up.shape
    half = (1 << level) // 2
    if half >= SUBLANES:
        shape = (c // (2 * half), 2, half, w)
        return jnp.concatenate([low.reshape(shape)[:, 0:1], up.reshape(shape)[:, 1:2]], axis=1).reshape(c, w)
    rowid = lax.broadcasted_iota(jnp.int32, up.shape, 0)
    return jnp.where((rowid & half) != 0, up, low)


def _hgrn_chunk(r_ref, r0, lb_ref, gn_ref, tri, lev, o_ref, st_sc, n_heads, kd):
    c = HG_CHUNK
    rw = n_heads * kd
    n_levels = c.bit_length() - 1
    head = lambda z, h: z[:, h * kd:(h + 1) * kd]
    q = r_ref[0, r0:r0 + c, 0:rw].astype(F32)
    rf = r_ref[0, r0:r0 + c, rw:2 * rw].astype(F32)
    v = r_ref[0, r0:r0 + c, 2 * rw:3 * rw].astype(BF16)
    rg = r_ref[0, r0:r0 + c, 3 * rw:4 * rw].astype(F32)
    lb = lb_ref[...]
    f = lb + (1.0 - lb) * jax.nn.sigmoid(rf)
    logf = jnp.log(f) * LOG2E
    kk = 1.0 - f
    hi = logf.astype(BF16)
    lo = (logf - hi.astype(F32)).astype(BF16)
    b = _nn(tri, hi) + _nn(tri, lo)
    qb = q.astype(BF16)
    kb = kk.astype(BF16)
    diag = lev == 0
    a = [jnp.where(diag, _nt(head(qb, h), head(kb, h)), 0.0) for h in range(n_heads)]
    for level in range(1, n_levels + 1):
        e = jnp.exp2(-jnp.abs(b - _mid_reference(b, level)))
        z = (_upper_lower(q, kk, level) * e).astype(BF16)
        here = lev == level
        a = [jnp.where(here, _nt(head(z, h), head(z, h)), a[h]) for h in range(n_heads)]
    b_last = b[c - 1:c, :]
    q_hat = (q * jnp.exp2(b)).astype(BF16)
    k_hat = (kk * jnp.exp2(b_last - b)).astype(BF16)
    decay = jnp.exp2(b_last)
    outs = []
    for h in range(n_heads):
        st = st_sc[h]
        o = _nn(a[h].astype(BF16), head(v, h)) + _nt(head(q_hat, h), st.astype(BF16))
        st_sc[h] = head(decay, h) * st + _tn(head(v, h), head(k_hat, h))
        ms = jnp.mean(o * o, axis=-1, keepdims=True)
        outs.append(o * lax.rsqrt(ms + EPS) * gn_ref[...])
    on = jnp.concatenate(outs, axis=1) * (rg * jax.nn.sigmoid(rg))
    o_ref[0, r0:r0 + c, :] = on.astype(o_ref.dtype)


def _mix_kernel(pt_ref, lam_ref, r_ref, lb_ref, gn_ref, tri_ref, lev_ref, q_ref, kn_ref, vn_ref, bias_ref, sg_ref,
                ck_hbm, cv_hbm, o_rec_ref, sfin_ref, o_dec_ref, st_sc, kbuf, vbuf, ksem, vsem,
                *, n_pages, n_heads, hd, r_heads, kd, out_scale, steps_per_batch):
    step = pl.program_id(0)
    slot = lax.rem(step, 2)
    t = lax.rem(step, steps_per_batch)

    def page_copies(at_step, at_slot):
        copies = []
        for j in range(MIX_SEQS):
            for p in range(n_pages):
                page = pt_ref[at_step * MIX_SEQS + j, p]
                i = j * n_pages + p
                copies.append(pltpu.make_async_copy(ck_hbm.at[page], kbuf.at[at_slot, i], ksem.at[at_slot]))
                copies.append(pltpu.make_async_copy(cv_hbm.at[page], vbuf.at[at_slot, i], vsem.at[at_slot]))
        return copies

    @pl.when(step == 0)
    def _():
        for cp in page_copies(0, 0):
            cp.start()

    @pl.when(step + 1 < pl.num_programs(0))
    def _():
        for cp in page_copies(step + 1, 1 - slot):
            cp.start()

    @pl.when(t == 0)
    def _():
        st_sc[...] = jnp.zeros(st_sc.shape, F32)

    for cp in page_copies(step, slot):
        cp.wait()

    lev = lev_ref[...]
    tri = tri_ref[...]
    lam = lam_ref[0, 0]
    for c in range(MIX_SEQS):
        _hgrn_chunk(r_ref, c * HG_CHUNK, lb_ref, gn_ref, tri, lev, o_rec_ref, st_sc, r_heads, kd)
        rows = slice(c * SAMPLE_PAD, (c + 1) * SAMPLE_PAD)
        k_refs = [kbuf.at[slot, c * n_pages + p] for p in range(n_pages)]
        v_refs = [vbuf.at[slot, c * n_pages + p] for p in range(n_pages)]
        _decode_seq(lam, q_ref[rows, :].astype(F32), kn_ref[rows, :].astype(F32), vn_ref[rows, :].astype(F32),
                    bias_ref, sg_ref[...], k_refs, v_refs, o_dec_ref, rows, n_heads, hd, out_scale)

    @pl.when(t == steps_per_batch - 1)
    def _():
        for h in range(r_heads):
            sfin_ref[0, h] = st_sc[h].T


def _mix(page_table, lam, r, lb, gn, q, kn, vn, bias, sg, ck, cv, n_heads, hd, r_heads, kd, out_scale):
    n_seq, n_pages = page_table.shape
    b, t, _ = r.shape
    _, aw, ps = ck.shape
    vd = aw // n_heads
    c = HG_CHUNK
    tokens = MIX_SEQS * c
    steps_per_batch = t // tokens
    n_steps = b * steps_per_batch
    assert t % tokens == 0 and n_seq == n_steps * MIX_SEQS
    idx = np.arange(c)
    x = idx[:, None] ^ idx[None, :]
    lev = np.where(idx[:, None] > idx[None, :], np.floor(np.log2(np.maximum(x, 1))).astype(np.int32) + 1, -1)
    lev = np.where(idx[:, None] == idx[None, :], 0, lev).astype(np.int32)
    tri = (idx[:, None] >= idx[None, :]).astype(np.float32)
    kern = functools.partial(_mix_kernel, n_pages=n_pages, n_heads=n_heads, hd=hd, r_heads=r_heads, kd=kd,
                             out_scale=out_scale, steps_per_batch=steps_per_batch)
    c2 = lambda s, pt: (0, 0)
    tok = lambda s, pt: (s // steps_per_batch, s % steps_per_batch, 0)
    seq_rows = lambda s, pt: (s, 0)
    n_k = MIX_SEQS * n_pages
    rows = MIX_SEQS * SAMPLE_PAD
    grid_spec = pltpu.PrefetchScalarGridSpec(
        num_scalar_prefetch=1,
        grid=(n_steps,),
        in_specs=[pl.BlockSpec(memory_space=pltpu.SMEM),
                  pl.BlockSpec((1, tokens, r.shape[2]), tok),
                  pl.BlockSpec((1, r_heads * kd), c2),
                  pl.BlockSpec((1, kd), c2),
                  pl.BlockSpec((c, c), c2),
                  pl.BlockSpec((c, c), c2),
                  pl.BlockSpec((rows, aw), seq_rows),
                  pl.BlockSpec((rows, aw), seq_rows),
                  pl.BlockSpec((rows, aw), seq_rows),
                  pl.BlockSpec(bias.shape, lambda s, pt: (0, 0, 0)),
                  pl.BlockSpec((1, vd), c2),
                  pl.BlockSpec(memory_space=pl.ANY),
                  pl.BlockSpec(memory_space=pl.ANY)],
        out_specs=[pl.BlockSpec((1, tokens, r_heads * kd), tok),
                   pl.BlockSpec((1, r_heads, kd, kd), lambda s, pt: (s // steps_per_batch, 0, 0, 0)),
                   pl.BlockSpec((rows, aw), seq_rows)],
        scratch_shapes=[pltpu.VMEM((r_heads, kd, kd), F32),
                        pltpu.VMEM((2, n_k) + ck.shape[1:], ck.dtype),
                        pltpu.VMEM((2, n_k) + cv.shape[1:], cv.dtype),
                        pltpu.SemaphoreType.DMA((2,)),
                        pltpu.SemaphoreType.DMA((2,))],
    )
    return pl.pallas_call(
        kern,
        grid_spec=grid_spec,
        out_shape=[jax.ShapeDtypeStruct((b, t, r_heads * kd), BF16),
                   jax.ShapeDtypeStruct((b, r_heads, kd, kd), F32),
                   jax.ShapeDtypeStruct((n_seq * SAMPLE_PAD, aw), F32)],
        compiler_params=pltpu.CompilerParams(dimension_semantics=("arbitrary",),
                                             vmem_limit_bytes=VMEM_LIMIT),
        name="hgrn_prompt_attn_decode",
    )(page_table, lam, r, lb, gn, jnp.asarray(tri, BF16), jnp.asarray(lev), q, kn, vn, bias, sg, ck, cv)


def _hgrn_step_kernel(r_ref, lb_ref, gn_ref, s_ref, o_ref, so_ref, *, n_heads, kd, n_tok, n_seq):
    rw = n_heads * kd
    rowid = lax.broadcasted_iota(jnp.int32, (SAMPLE_PAD, kd), 0)
    live = rowid < n_tok
    zpad = jnp.zeros((SAMPLE_PAD, kd), F32)
    pad16 = lambda z: jnp.concatenate([z, zpad], axis=0).astype(BF16)
    ones16 = jnp.ones((2 * SAMPLE_PAD, kd), BF16)
    for g in range(n_seq):
        rows = slice(g * SAMPLE_PAD, (g + 1) * SAMPLE_PAD)
        for h in range(n_heads):
            cols = slice(h * kd, (h + 1) * kd)
            q = r_ref[rows, h * kd:(h + 1) * kd]
            rf = r_ref[rows, rw + h * kd:rw + (h + 1) * kd]
            v = r_ref[rows, 2 * rw + h * kd:2 * rw + (h + 1) * kd]
            rg = r_ref[rows, 3 * rw + h * kd:3 * rw + (h + 1) * kd]
            lb = lb_ref[:, cols]
            f = lb + (1.0 - lb) * jax.nn.sigmoid(rf)
            kk = jnp.where(live, 1.0 - f, 0.0)
            b = jnp.where(live, jnp.log(f) * LOG2E, 0.0)
            for sh in (1, 2, 4):
                b = b + jnp.where(rowid >= sh, pltpu.roll(b, sh, 0), 0.0)
            s0 = s_ref[g, h]
            o = jnp.sum(q * kk, axis=1, keepdims=True) * v
            for d in range(1, n_tok):
                x = q * pltpu.roll(kk, d, 0) * jnp.exp2(b - pltpu.roll(b, d, 0))
                w = jnp.sum(jnp.where(rowid >= d, x, 0.0), axis=1, keepdims=True)
                o = o + w * pltpu.roll(v, d, 0)
            o = o + _nn(pad16(q * jnp.exp2(b)), s0.astype(BF16))[0:SAMPLE_PAD]
            b_last = b[n_tok - 1:n_tok, :]
            decay = jnp.exp2(b_last)
            d_hi = decay.astype(BF16).astype(F32)
            d_split = jnp.where(rowid == 0, d_hi, jnp.where(rowid == 1, decay - d_hi, 0.0))
            decay_cols = _tn(pad16(d_split), ones16)
            so_ref[g, h] = decay_cols * s0 + _tn(pad16(kk * jnp.exp2(b_last - b)), pad16(v))
            ms = jnp.mean(o * o, axis=-1, keepdims=True)
            on = o * lax.rsqrt(ms + EPS) * gn_ref[...] * (rg * jax.nn.sigmoid(rg))
            o_ref[rows, cols] = on.astype(o_ref.dtype)


def _hgrn_sample(r, lb, gn, state, n_heads, kd, n_tok, seq_per_step=4):
    n_seq = state.shape[0]
    kern = functools.partial(_hgrn_step_kernel, n_heads=n_heads, kd=kd, n_tok=n_tok, n_seq=seq_per_step)
    rows = seq_per_step * SAMPLE_PAD
    return pl.pallas_call(
        kern,
        grid=(n_seq // seq_per_step,),
        in_specs=[pl.BlockSpec((rows, r.shape[1]), lambda i: (i, 0)),
                  pl.BlockSpec((1, n_heads * kd), lambda i: (0, 0)),
                  pl.BlockSpec((1, kd), lambda i: (0, 0)),
                  pl.BlockSpec((seq_per_step, n_heads, kd, kd), lambda i: (i, 0, 0, 0))],
        out_specs=[pl.BlockSpec((rows, n_heads * kd), lambda i: (i, 0)),
                   pl.BlockSpec((seq_per_step, n_heads, kd, kd), lambda i: (i, 0, 0, 0))],
        out_shape=[jax.ShapeDtypeStruct((n_seq * SAMPLE_PAD, n_heads * kd), F32),
                   jax.ShapeDtypeStruct(state.shape, F32)],
        compiler_params=pltpu.CompilerParams(dimension_semantics=("arbitrary",),
                                             vmem_limit_bytes=VMEM_LIMIT),
        name="hgrn_sample",
    )(r, lb, gn, state)


def _ffn_kernel(*refs, sample, aw, seq_rows):
    if sample:
        (x_ref, oa_ref, or_ref, wo_ref, g2_ref, wg_ref, wu_ref, cw_ref, cb_ref, wd_ref, prev_ref,
         y_ref, a_ref) = refs
    else:
        (x_ref, oa_ref, or_ref, wo_ref, g2_ref, wg_ref, wu_ref, cw_ref, cb_ref, wd_ref,
         y_ref, a_ref, carry_sc) = refs

        @pl.when(pl.program_id(1) == 0)
        def _():
            carry_sc[...] = jnp.zeros(carry_sc.shape, F32)

    x = x_ref[...].astype(F32)
    x1 = x + _nn(oa_ref[...].astype(BF16), wo_ref[0:aw, :]) + _nn(or_ref[...].astype(BF16), wo_ref[aw:, :])
    ms = jnp.mean(x1 * x1, axis=-1, keepdims=True)
    h2 = (x1 * lax.rsqrt(ms + EPS) * g2_ref[...]).astype(BF16)
    tm = x.shape[0]
    a = _nn(h2, wg_ref[...])
    u = _nn(h2, wu_ref[...])
    rowid = lax.broadcasted_iota(jnp.int32, a.shape, 0)
    am1 = pltpu.roll(a, 1, 0)
    am2 = pltpu.roll(a, 2, 0)
    if sample:
        rig = rowid & (seq_rows - 1)
        prev = prev_ref[...]
        am1 = jnp.where(rig == 0, pltpu.roll(prev, tm - 1, 0), am1)
        am2 = jnp.where(rig < 2, prev, am2)
        a_ref[...] = a
    else:
        last = carry_sc[...]
        am1 = jnp.where(rowid == 0, last[SUBLANES - 1:SUBLANES, :], am1)
        am2 = jnp.where(rowid == 0, last[SUBLANES - 2:SUBLANES - 1, :],
                        jnp.where(rowid == 1, last[SUBLANES - 1:SUBLANES, :], am2))
        tail = a[tm - SUBLANES:tm, :]
        carry_sc[...] = tail
        a_ref[0] = tail
    cw = cw_ref[...]
    cv = cb_ref[...] + cw[0:1, :] * am2 + cw[1:2, :] * am1 + cw[2:3, :] * a
    gate = (cv * jax.nn.sigmoid(cv) * u).astype(BF16)
    y_ref[...] = x1 + _nn(gate, wd_ref[...])


def _ffn(x2, oa, orec, wo, g2, wg, wu, cw, cb, wd, prev, tm, n_batch, seq_rows):
    n, d = x2.shape
    aw = oa.shape[1]
    dff = wg.shape[1]
    sample = prev is not None
    assert not sample or (seq_rows >= CONV_W - 1 and seq_rows & (seq_rows - 1) == 0 and tm % seq_rows == 0)
    kern = functools.partial(_ffn_kernel, sample=sample, aw=aw, seq_rows=seq_rows)
    scratch = []
    if sample:
        grid = (n // tm,)
        row = lambda i: (i, 0)
        c2 = lambda i: (0, 0)
        extra_in = [pl.BlockSpec((tm, dff), row)]
        extra_args = [prev]
        a_shape = jax.ShapeDtypeStruct((n, dff), F32)
        a_spec = pl.BlockSpec((tm, dff), row)
        sem = ("arbitrary",)
    else:
        per_b = n // n_batch // tm
        grid = (n_batch, per_b)
        row = lambda b, i: (b * per_b + i, 0)
        c2 = lambda b, i: (0, 0)
        extra_in = []
        extra_args = []
        a_shape = jax.ShapeDtypeStruct((n_batch, SUBLANES, dff), F32)
        a_spec = pl.BlockSpec((1, SUBLANES, dff), lambda b, i: (b, 0, 0))
        scratch.append(pltpu.VMEM((SUBLANES, dff), F32))
        sem = ("arbitrary", "arbitrary")
    return pl.pallas_call(
        kern,
        grid=grid,
        in_specs=[pl.BlockSpec((tm, d), row),
                  pl.BlockSpec((tm, aw), row),
                  pl.BlockSpec((tm, aw), row),
                  _const_spec(wo.shape, c2),
                  pl.BlockSpec((1, d), c2),
                  _const_spec(wg.shape, c2),
                  _const_spec(wu.shape, c2),
                  pl.BlockSpec(cw.shape, c2),
                  pl.BlockSpec(cb.shape, c2),
                  _const_spec(wd.shape, c2)] + extra_in,
        out_specs=[pl.BlockSpec((tm, d), row), a_spec],
        out_shape=[jax.ShapeDtypeStruct((n, d), F32), a_shape],
        scratch_shapes=scratch,
        compiler_params=pltpu.CompilerParams(dimension_semantics=sem, vmem_limit_bytes=VMEM_LIMIT),
        name="ffn_sample" if sample else "ffn_prompt",
    )(x2, oa, orec, wo, g2, wg, wu, cw, cb, wd, *extra_args)


def kernel(x_prompt, x_sample, cache_k, cache_v, page_table, state_hgrn, state_conv, rel_bias, norm1_g, w_in,
           qk_norm_g, lambda_qk, subln_g, hgrn_lb, hgrn_onorm_g, w_out, norm2_g, w_gate, w_up, conv_w, conv_b,
           w_down):
    bp, tp, d = x_prompt.shape
    bs, ts, _ = x_sample.shape
    depth = w_in.shape[0]
    assert depth == 1 and CONV_W - 1 <= ts <= SAMPLE_PAD and conv_w.shape[1] == CONV_W
    n_heads, _, hd = cache_k.shape[3:]
    vd = cache_v.shape[4]
    aw = n_heads * vd
    r_heads, kd = state_hgrn.shape[2], state_hgrn.shape[3]
    rw = r_heads * kd
    dff = w_gate.shape[2]
    page_size = cache_k.shape[2]
    assert page_size >= MAX_DISTANCE and tp % TQ == 0
    l = 0
    lam_init = 0.8 - 0.6 * math.exp(-0.3 * l)
    out_scale = 1.0 - lam_init

    biasp, biass, lam, lb = _tables(rel_bias, lambda_qk[l], hgrn_lb, n_heads, ts, page_size, lam_init)

    w_in_bf = w_in[l].astype(BF16)
    wo_bf = w_out[l].astype(BF16)
    wg_bf = w_gate[l].astype(BF16)
    wu_bf = w_up[l].astype(BF16)
    wd_bf = w_down[l].astype(BF16)
    cw = conv_w[l]
    cb = conv_b[l].reshape(1, dff)
    g1 = norm1_g[l].reshape(1, d)
    g2 = norm2_g[l].reshape(1, d)
    qg = jnp.tile(qk_norm_g[l, 0].reshape(1, 2 * hd), (1, n_heads))
    kg = jnp.tile(qk_norm_g[l, 1].reshape(1, 2 * hd), (1, n_heads))
    seg = np.arange(aw) // hd
    bd = jnp.asarray((seg[:, None] == seg[None, :]).astype(np.float32) / hd, BF16)
    sg = subln_g[l].reshape(1, vd)
    gn = hgrn_onorm_g[l].reshape(1, kd)
    q_scale = hd ** -0.5 * LOG2E

    xp2 = x_prompt.reshape(bp * tp, d)
    qbt, kft, vf, kb, vbt, rp = _proj(xp2, g1, w_in_bf, qg, kg, bd, aw, rw, vd, q_scale, True, BF16, BF16, 512, bp)
    xs_pad = jnp.pad(x_sample, ((0, 0), (0, SAMPLE_PAD - ts), (0, 0))).reshape(bs * SAMPLE_PAD, d)
    qs, ksf, vsf, rs = _proj(xs_pad, g1, w_in_bf, qg, kg, bd, aw, rw, vd, q_scale, False, F32, F32, 512, bs)

    o_att = _attn(lam, qbt, kb.reshape(bp, tp, aw), vbt, biasp, sg, n_heads, hd, out_scale)
    ck = jnp.transpose(cache_k[l], (0, 2, 3, 4, 1)).reshape(cache_k.shape[1], aw, page_size)
    cv = cache_v[l].reshape(cache_v.shape[1], page_size * n_heads, vd)
    o_rec, s_p, o_att_s = _mix(page_table, lam, rp.reshape(bp, tp, 4 * rw), lb, gn, qs, ksf, vsf, biass, sg, ck, cv,
                               n_heads, hd, r_heads, kd, out_scale)
    o_rec_s, s_s = _hgrn_sample(rs, lb, gn, state_hgrn[l], r_heads, kd, ts)

    yp, a_tail = _ffn(xp2, o_att.reshape(bp * tp, aw), o_rec.reshape(bp * tp, rw), wo_bf, g2, wg_bf, wu_bf, cw, cb,
                      wd_bf, None, 512, bp, None)
    k_prompt = kft.reshape(bp, n_heads, 2, hd, tp).transpose(0, 4, 1, 2, 3)[None]
    v_prompt = vf.reshape(1, bp, tp, n_heads, vd)
    conv_prompt = a_tail[:, SUBLANES - (CONV_W - 1):][None]
    unpad = lambda z: z.reshape(bs, SAMPLE_PAD, -1)[:, :ts]
    flat = lambda z: unpad(z).reshape(bs * ts, -1)
    prev = jnp.pad(state_conv[l], ((0, 0), (0, ts - (CONV_W - 1)), (0, 0))).reshape(bs * ts, dff)
    ys, a_s = _ffn(x_sample.reshape(bs * ts, d), flat(o_att_s), flat(o_rec_s), wo_bf, g2, wg_bf, wu_bf, cw, cb,
                   wd_bf, prev, 256, 1, ts)
    y_sample = ys.reshape(bs, ts, d)
    k_sample = unpad(ksf).reshape(1, bs, ts, n_heads, 2, hd)
    v_sample = unpad(vsf).reshape(1, bs, ts, n_heads, vd)
    conv_sample = a_s.reshape(bs, ts, dff)[:, ts - (CONV_W - 1):ts][None]

    return (yp.reshape(bp, tp, d), y_sample, k_prompt, v_prompt, s_p[None], conv_prompt,
            k_sample, v_sample, s_s[None], conv_sample)
```

```python
import functools
import math

import numpy as np
import jax
import jax.numpy as jnp
from jax import lax
from jax.experimental import pallas as pl
from jax.experimental.pallas import tpu as pltpu

F32 = jnp.float32
BF16 = jnp.bfloat16

EPS = 1e-6
N_BUCKETS = 32
MAX_DISTANCE = 128
CONV_W = 3
NEG = -1e30
LOG2E = math.log2(math.e)
LANES = 128
SUBLANES = 8

TQ = 512
HG_CHUNK = 128
MIX_SEQS = 2
SAMPLE_PAD = 8
VMEM_LIMIT = 56 * 1024 * 1024


def _const_spec(shape, index_map):
    return pl.BlockSpec(shape, index_map, pipeline_mode=pl.Buffered(1))


def _nt(a, b):
    return lax.dot_general(a, b, (((1,), (1,)), ((), ())), preferred_element_type=F32)


def _tn(a, b):
    return lax.dot_general(a, b, (((0,), (0,)), ((), ())), preferred_element_type=F32)


def _nn(a, b):
    return jnp.dot(a, b, preferred_element_type=F32)


def _rel_bucket_np(n):
    n = np.maximum(n, 0)
    max_exact = N_BUCKETS // 2
    nf = np.maximum(n, 1).astype(np.float32)
    large = max_exact + (np.log(nf / np.float32(max_exact)) / np.float32(math.log(MAX_DISTANCE / max_exact))
                         * np.float32(N_BUCKETS - max_exact)).astype(np.int32)
    large = np.minimum(large, N_BUCKETS - 1)
    return np.where(n < max_exact, n, large).astype(np.int32)


def _tables_kernel(relb_ref, lq_ref, hlb_ref, bkp_ref, bks_ref,
                   biasp_ref, biass_ref, lam_ref, lb_ref, *, n_heads, lam_init, near_blocks):
    for h in range(n_heads):
        far = relb_ref[N_BUCKETS - 1, h]
        for kind in range(2):
            biasp_ref[h, kind] = jnp.where(bkp_ref[kind] < 0, NEG, 0.0).astype(F32)
            for (r0, c0) in near_blocks[kind]:
                bk = bkp_ref[kind, r0:r0 + LANES, c0:c0 + LANES]
                acc = jnp.where(bk < 0, NEG, 0.0).astype(F32)
                for b in range(N_BUCKETS - 1):
                    acc = jnp.where(bk == b, (relb_ref[b, h] - far) * LOG2E, acc)
                biasp_ref[h, kind, r0:r0 + LANES, c0:c0 + LANES] = acc
            rows = bks_ref.shape[1] // n_heads
            bk = bks_ref[kind, h * rows:(h + 1) * rows, :]
            acc = jnp.where(bk < 0, NEG, 0.0).astype(F32)
            for b in range(N_BUCKETS - 1):
                acc = jnp.where(bk == b, (relb_ref[b, h] - far) * LOG2E, acc)
            biass_ref[kind, h * rows:(h + 1) * rows, :] = acc
    lq = lq_ref[...].astype(F32)
    s1 = jnp.sum(lq[0:1] * lq[1:2], axis=1, keepdims=True)
    s2 = jnp.sum(lq[2:3] * lq[3:4], axis=1, keepdims=True)
    lam = jnp.exp(s1) - jnp.exp(s2) + lam_init
    lam_ref[...] = jnp.broadcast_to(lam, lam_ref.shape)
    hl = hlb_ref[...].astype(F32)
    mx = jnp.max(hl, axis=0, keepdims=True)
    e = jnp.exp(hl - mx)
    lb_ref[...] = e[0:1] / jnp.sum(e, axis=0, keepdims=True)


def _tables(rel_bias, lambda_qk_l, hgrn_lb, n_heads, n_tok, page_size, lam_init):
    c = np.arange(TQ)[:, None]
    r = np.arange(TQ)[None, :]
    diag = np.where(c <= r, _rel_bucket_np(r - c), -1)
    prev = _rel_bucket_np(TQ + r - c)
    bkp_np = np.stack([diag, prev]).astype(np.int32)
    bkp = jnp.asarray(bkp_np)
    near_blocks = tuple(
        tuple((r0, c0) for r0 in range(0, TQ, LANES) for c0 in range(0, TQ, LANES)
              if np.any((bkp_np[kind, r0:r0 + LANES, c0:c0 + LANES] >= 0)
                        & (bkp_np[kind, r0:r0 + LANES, c0:c0 + LANES] < N_BUCKETS - 1)))
        for kind in range(2))
    t = np.tile(np.arange(SAMPLE_PAD), 2 * n_heads)[:, None]
    cc = np.arange(page_size)[None, :]
    last_page = _rel_bucket_np(page_size + t - cc)
    new_page = np.where(cc <= np.minimum(t, n_tok - 1), _rel_bucket_np(t - cc), -1)
    bks = jnp.asarray(np.stack([last_page, new_page]).astype(np.int32))
    rows_s = 2 * n_heads * SAMPLE_PAD
    kern = functools.partial(_tables_kernel, n_heads=n_heads, lam_init=lam_init, near_blocks=near_blocks)
    return pl.pallas_call(
        kern,
        out_shape=(jax.ShapeDtypeStruct((n_heads, 2, TQ, TQ), F32),
                   jax.ShapeDtypeStruct((2, rows_s, page_size), F32),
                   jax.ShapeDtypeStruct((SUBLANES, LANES), F32),
                   jax.ShapeDtypeStruct((1, hgrn_lb.shape[1]), F32)),
        in_specs=[pl.BlockSpec(memory_space=pltpu.SMEM),
                  pl.BlockSpec(memory_space=pltpu.VMEM),
                  pl.BlockSpec(memory_space=pltpu.VMEM),
                  pl.BlockSpec(memory_space=pltpu.VMEM),
                  pl.BlockSpec(memory_space=pltpu.VMEM)],
        name="tables",
    )(rel_bias, lambda_qk_l, hgrn_lb, bkp, bks)


def _proj_kernel(x_ref, g1_ref, w_ref, qg_ref, kg_ref, bd_ref, *out_refs, aw, rw, vd, q_scale, emit_bf16):
    if emit_bf16:
        q_ref, kf_ref, vf_ref, kb_ref, vb_ref, r_ref = out_refs
    else:
        q_ref, kf_ref, vf_ref, r_ref = out_refs
    x = x_ref[...].astype(F32)
    ms = jnp.mean(x * x, axis=-1, keepdims=True)
    h = (x * lax.rsqrt(ms + EPS) * g1_ref[...]).astype(BF16)

    def seg_norm(z, g):
        msq = _nn((z * z).astype(BF16), bd_ref[...])
        return z * lax.rsqrt(msq + EPS) * g

    zq = _nn(h, w_ref[:, 0:aw])
    qn = seg_norm(zq, qg_ref[...]) * q_scale
    if emit_bf16:
        q_ref[0] = qn.T.astype(q_ref.dtype)
    else:
        q_ref[...] = qn.astype(q_ref.dtype)
    zk = _nn(h, w_ref[:, aw:2 * aw])
    kn = seg_norm(zk, kg_ref[...])
    zv = _nn(h, w_ref[:, 2 * aw:3 * aw])
    if emit_bf16:
        n_heads = aw // vd
        kf_ref[0] = kn.T
        kb_ref[...] = kn.astype(BF16)
        vb_ref[0] = zv.T.astype(BF16)
        for hh in range(n_heads):
            vf_ref[pl.ds(hh, zv.shape[0], stride=n_heads), :] = zv[:, hh * vd:(hh + 1) * vd]
    else:
        kf_ref[...] = kn
        vf_ref[...] = zv
    for g in range(4):
        z = _nn(h, w_ref[:, 3 * aw + g * rw:3 * aw + (g + 1) * rw])
        r_ref[:, g * rw:(g + 1) * rw] = z.astype(r_ref.dtype)


def _proj(x2, g1, w_bf, qg, kg, bd, aw, rw, vd, q_scale, emit_bf16, q_dtype, r_dtype, tm, n_batch):
    n, d = x2.shape
    kern = functools.partial(_proj_kernel, aw=aw, rw=rw, vd=vd, q_scale=q_scale, emit_bf16=emit_bf16)
    row = lambda i: (i, 0)
    const = lambda i: (0, 0)
    if emit_bf16:
        t = n // n_batch
        per_b = t // tm
        n_heads = aw // vd
        xposed = lambda i: (i // per_b, 0, i % per_b)
        out_shape = [jax.ShapeDtypeStruct((n_batch, aw, t), q_dtype),
                     jax.ShapeDtypeStruct((n_batch, aw, t), F32),
                     jax.ShapeDtypeStruct((n * n_heads, vd), F32),
                     jax.ShapeDtypeStruct((n, aw), BF16),
                     jax.ShapeDtypeStruct((n_batch, aw, t), BF16)]
        out_specs = [pl.BlockSpec((1, aw, tm), xposed),
                     pl.BlockSpec((1, aw, tm), xposed),
                     pl.BlockSpec((tm * n_heads, vd), row),
                     pl.BlockSpec((tm, aw), row),
                     pl.BlockSpec((1, aw, tm), xposed)]
    else:
        out_shape = [jax.ShapeDtypeStruct((n, aw), q_dtype),
                     jax.ShapeDtypeStruct((n, aw), F32),
                     jax.ShapeDtypeStruct((n, aw), F32)]
        out_specs = [pl.BlockSpec((tm, aw), row)] * 3
    out_shape.append(jax.ShapeDtypeStruct((n, 4 * rw), r_dtype))
    out_specs.append(pl.BlockSpec((tm, 4 * rw), row))
    return pl.pallas_call(
        kern,
        grid=(n // tm,),
        in_specs=[pl.BlockSpec((tm, d), row),
                  pl.BlockSpec((1, d), const),
                  _const_spec(w_bf.shape, const),
                  pl.BlockSpec((1, aw), const),
                  pl.BlockSpec((1, aw), const),
                  pl.BlockSpec((aw, aw), const)],
        out_specs=out_specs,
        out_shape=out_shape,
        compiler_params=pltpu.CompilerParams(dimension_semantics=("arbitrary",),
                                             vmem_limit_bytes=VMEM_LIMIT),
        name="proj_bf16" if emit_bf16 else "proj_f32",
    )(x2, g1, w_bf, qg, kg, bd)


def _attn_kernel(lam_ref, q_ref, k_ref, vt_ref, bias_ref, sg_ref, o_ref,
                 qst_sc, sa_sc, ma_sc, sb_sc, mb_sc, m_sc, l_sc, acc_sc, *, hd, out_scale):
    i = pl.program_id(2)

    qt = q_ref[0].astype(F32)
    row = lax.broadcasted_iota(jnp.int32, qt.shape, 0)
    qst_sc[...] = jnp.concatenate([jnp.where(row < hd, qt, 0.0), jnp.where(row >= hd, qt, 0.0)],
                                  axis=1).astype(BF16)
    m_sc[...] = jnp.full(m_sc.shape, NEG, F32)
    l_sc[...] = jnp.zeros(l_sc.shape, F32)
    acc_sc[...] = jnp.zeros(acc_sc.shape, F32)

    bufs = ((sa_sc, ma_sc), (sb_sc, mb_sc))

    def issue(buf, g, with_max):
        start = pl.multiple_of(g * TQ, TQ)
        s = _nn(k_ref[0, pl.ds(start, TQ), :], qst_sc[...])
        buf[0][...] = s
        if with_max:
            buf[1][...] = jnp.max(s, axis=0, keepdims=True)

    def consume(buf, g, kind):
        s = buf[0][...]
        if kind is None:
            s_max = buf[1][...]
        else:
            bias = bias_ref[0, kind]
            s = jnp.concatenate([s[:, 0:TQ] + bias, s[:, TQ:2 * TQ] + bias], axis=1)
            s_max = jnp.max(s, axis=0, keepdims=True)
        m_old = m_sc[...]
        m_new = jnp.maximum(m_old, s_max)
        alpha = jnp.exp2(m_old - m_new)
        p = jnp.exp2(s - m_new)
        l_sc[...] = alpha * l_sc[...] + jnp.sum(p, axis=0, keepdims=True)
        start = pl.multiple_of(g * TQ, TQ)
        acc_sc[...] = alpha * acc_sc[...] + _nn(vt_ref[0, :, pl.ds(start, TQ)], p.astype(BF16))
        m_sc[...] = m_new

    n_pure = jnp.maximum(i - 1, 0)
    issue(bufs[0], 0, True)

    def pair(jj, carry):
        g = 2 * jj
        issue(bufs[1], g + 1, True)
        consume(bufs[0], g, None)
        issue(bufs[0], g + 2, True)
        consume(bufs[1], g + 1, None)
        return carry

    lax.fori_loop(0, n_pure // 2, pair, 0)
    odd = (n_pure & 1) == 1

    def finish(cur, other):
        @pl.when(i > 0)
        def _():
            issue(other, i, False)
            consume(cur, i - 1, 1)
            consume(other, i, 0)

        @pl.when(i == 0)
        def _():
            consume(cur, 0, 0)

    @pl.when(odd)
    def _():
        issue(bufs[1], n_pure, False)
        consume(bufs[0], n_pure - 1, None)
        finish(bufs[1], bufs[0])

    @pl.when(jnp.logical_not(odd))
    def _():
        finish(bufs[0], bufs[1])

    a = acc_sc[...] * (1.0 / l_sc[...])
    ot = a[:, 0:TQ] - lam_ref[0, 0] * a[:, TQ:2 * TQ]
    ms = jnp.mean(ot * ot, axis=0, keepdims=True)
    o = (ot * lax.rsqrt(ms + EPS)).T
    o_ref[0] = (o * (sg_ref[...] * out_scale)).astype(o_ref.dtype)


def _attn(lam, qt, k, vt, bias, sg, n_heads, hd, out_scale):
    b, t, aw = k.shape
    vd = aw // n_heads
    assert t % TQ == 0
    kern = functools.partial(_attn_kernel, hd=hd, out_scale=out_scale)
    return pl.pallas_call(
        kern,
        grid=(b, n_heads, t // TQ),
        in_specs=[pl.BlockSpec(memory_space=pltpu.SMEM),
                  pl.BlockSpec((1, vd, TQ), lambda bi, h, i: (bi, h, i)),
                  pl.BlockSpec((1, t, vd), lambda bi, h, i: (bi, 0, h)),
                  pl.BlockSpec((1, vd, t), lambda bi, h, i: (bi, h, 0)),
                  pl.BlockSpec((1, 2, TQ, TQ), lambda bi, h, i: (h, 0, 0, 0)),
                  pl.BlockSpec((1, vd), lambda bi, h, i: (0, 0))],
        out_specs=pl.BlockSpec((1, TQ, vd), lambda bi, h, i: (bi, i, h)),
        out_shape=jax.ShapeDtypeStruct((b, t, aw), BF16),
        scratch_shapes=[pltpu.VMEM((vd, 2 * TQ), BF16),
                        pltpu.VMEM((TQ, 2 * TQ), F32),
                        pltpu.VMEM((1, 2 * TQ), F32),
                        pltpu.VMEM((TQ, 2 * TQ), F32),
                        pltpu.VMEM((1, 2 * TQ), F32),
                        pltpu.VMEM((1, 2 * TQ), F32),
                        pltpu.VMEM((1, 2 * TQ), F32),
                        pltpu.VMEM((vd, 2 * TQ), F32)],
        compiler_params=pltpu.CompilerParams(dimension_semantics=("arbitrary",) * 3,
                                             vmem_limit_bytes=VMEM_LIMIT),
        name="attn_prompt",
    )(lam, qt, k, vt, bias, sg)


def _decode_seq(lam, q8, kn8, vn8, bias_ref, sg, k_refs, v_refs, o_ref, rows_out, n_heads, hd, out_scale):
    n_pages = len(k_refs)
    aw = q8.shape[1]
    vd = aw // n_heads
    ps = k_refs[0].shape[1]
    lane = lax.broadcasted_iota(jnp.int32, q8.shape, 1)
    pieces = [jnp.where((lane >= hm * hd) & (lane < (hm + 1) * hd), q8, 0.0) for hm in range(2 * n_heads)]
    qbd = jnp.concatenate(pieces, axis=0).astype(BF16)
    s_list = []
    for p in range(n_pages):
        s = _nn(qbd, k_refs[p][...].astype(BF16))
        if p == n_pages - 1:
            s = s + bias_ref[0]
        s_list.append(s)
    pad = jnp.zeros((ps - SAMPLE_PAD, aw), F32)
    k_new = jnp.concatenate([kn8, pad], axis=0).astype(BF16)
    v_new = jnp.concatenate([vn8, pad], axis=0).astype(BF16)
    s_list.append(_nt(qbd, k_new) + bias_ref[1])
    s_all = jnp.concatenate(s_list, axis=1)
    m = jnp.max(s_all, axis=1, keepdims=True)
    e = jnp.exp2(s_all - m)
    inv = 1.0 / jnp.sum(e, axis=1, keepdims=True)
    eb = e.astype(BF16)
    rows = 2 * SAMPLE_PAD
    for h in range(n_heads):
        eh = eb[h * rows:(h + 1) * rows]
        acc = _nn(eh[:, n_pages * ps:(n_pages + 1) * ps], v_new[:, h * vd:(h + 1) * vd])
        for p in range(n_pages):
            vh = v_refs[p][pl.ds(h, ps, stride=n_heads), :].astype(BF16)
            acc = acc + _nn(eh[:, p * ps:(p + 1) * ps], vh)
        acc = acc * inv[h * rows:(h + 1) * rows]
        o = acc[0:SAMPLE_PAD] - lam * acc[SAMPLE_PAD:rows]
        ms = jnp.mean(o * o, axis=-1, keepdims=True)
        o_ref[rows_out, h * vd:(h + 1) * vd] = (o * lax.rsqrt(ms + EPS) * (sg * out_scale)).astype(o_ref.dtype)


def _mid_reference(b, level):
    c, w = b.shape
    p = 1 << level
    half = p // 2
    if p >= SUBLANES:
        bp = b.reshape(c // p, p, w)
        return jnp.broadcast_to(bp[:, half - 1:half, :], (c // p, p, w)).reshape(c, w)
    b8 = b.reshape(c // SUBLANES, SUBLANES, w)
    rig = lax.broadcasted_iota(jnp.int32, b8.shape, 1)
    out = None
    for start in range(SUBLANES - p, -1, -p):
        row = jnp.broadcast_to(b8[:, start + half - 1:start + half, :], b8.shape)
        out = row if out is None else jnp.where(rig < start + p, row, out)
    return out.reshape(c, w)


def _upper_lower(up, low, level):
    c, w = up.shape
    half = (1 << level) // 2
    if half >= SUBLANES:
        shape = (c // (2 * half), 2, half, w)
        return jnp.concatenate([low.reshape(shape)[:, 0:1], up.reshape(shape)[:, 1:2]], axis=1).reshape(c, w)
    rowid = lax.broadcasted_iota(jnp.int32, up.shape, 0)
    return jnp.where((rowid & half) != 0, up, low)


def _hgrn_chunk(r_ref, r0, lb_ref, gn_ref, tri, lev, o_ref, st_sc, n_heads, kd):
    c = HG_CHUNK
    rw = n_heads * kd
    n_levels = c.bit_length() - 1
    head = lambda z, h: z[:, h * kd:(h + 1) * kd]
    q = r_ref[0, r0:r0 + c, 0:rw].astype(F32)
    rf = r_ref[0, r0:r0 + c, rw:2 * rw].astype(F32)
    v = r_ref[0, r0:r0 + c, 2 * rw:3 * rw].astype(BF16)
    rg = r_ref[0, r0:r0 + c, 3 * rw:4 * rw].astype(F32)
    lb = lb_ref[...]
    f = lb + (1.0 - lb) * jax.nn.sigmoid(rf)
    logf = jnp.log(f) * LOG2E
    kk = 1.0 - f
    hi = logf.astype(BF16)
    lo = (logf - hi.astype(F32)).astype(BF16)
    b = _nn(tri, hi) + _nn(tri, lo)
    qb = q.astype(BF16)
    kb = kk.astype(BF16)
    diag = lev == 0
    a = [jnp.where(diag, _nt(head(qb, h), head(kb, h)), 0.0) for h in range(n_heads)]
    for level in range(1, n_levels + 1):
        e = jnp.exp2(-jnp.abs(b - _mid_reference(b, level)))
        z = (_upper_lower(q, kk, level) * e).astype(BF16)
        here = lev == level
        a = [jnp.where(here, _nt(head(z, h), head(z, h)), a[h]) for h in range(n_heads)]
    b_last = b[c - 1:c, :]
    q_hat = (q * jnp.exp2(b)).astype(BF16)
    k_hat = (kk * jnp.exp2(b_last - b)).astype(BF16)
    decay = jnp.exp2(b_last)
    outs = []
    for h in range(n_heads):
        st = st_sc[h]
        o = _nn(a[h].astype(BF16), head(v, h)) + _nt(head(q_hat, h), st.astype(BF16))
        st_sc[h] = head(decay, h) * st + _tn(head(v, h), head(k_hat, h))
        ms = jnp.mean(o * o, axis=-1, keepdims=True)
        outs.append(o * lax.rsqrt(ms + EPS) * gn_ref[...])
    on = jnp.concatenate(outs, axis=1) * (rg * jax.nn.sigmoid(rg))
    o_ref[0, r0:r0 + c, :] = on.astype(o_ref.dtype)


def _mix_kernel(pt_ref, lam_ref, r_ref, lb_ref, gn_ref, tri_ref, lev_ref, q_ref, kn_ref, vn_ref, bias_ref, sg_ref,
                rs_ref, s_ref, ck_hbm, cv_hbm, o_rec_ref, sfin_ref, o_dec_ref, o_rs_ref, so_ref,
                st_sc, kbuf, vbuf, ksem, vsem,
                *, n_pages, n_heads, hd, r_heads, kd, out_scale, steps_per_batch, n_tok):
    step = pl.program_id(0)
    slot = lax.rem(step, 2)
    t = lax.rem(step, steps_per_batch)

    def page_copies(at_step, at_slot):
        copies = []
        for j in range(MIX_SEQS):
            for p in range(n_pages):
                page = pt_ref[at_step * MIX_SEQS + j, p]
                i = j * n_pages + p
                copies.append(pltpu.make_async_copy(ck_hbm.at[page], kbuf.at[at_slot, i], ksem.at[at_slot]))
                copies.append(pltpu.make_async_copy(cv_hbm.at[page], vbuf.at[at_slot, i], vsem.at[at_slot]))
        return copies

    @pl.when(step == 0)
    def _():
        for cp in page_copies(0, 0):
            cp.start()

    @pl.when(step + 1 < pl.num_programs(0))
    def _():
        for cp in page_copies(step + 1, 1 - slot):
            cp.start()

    @pl.when(t == 0)
    def _():
        st_sc[...] = jnp.zeros(st_sc.shape, F32)

    for cp in page_copies(step, slot):
        cp.wait()

    lev = lev_ref[...]
    tri = tri_ref[...]
    lam = lam_ref[0, 0]
    for c in range(MIX_SEQS):
        _hgrn_chunk(r_ref, c * HG_CHUNK, lb_ref, gn_ref, tri, lev, o_rec_ref, st_sc, r_heads, kd)
        rows = slice(c * SAMPLE_PAD, (c + 1) * SAMPLE_PAD)
        k_refs = [kbuf.at[slot, c * n_pages + p] for p in range(n_pages)]
        v_refs = [vbuf.at[slot, c * n_pages + p] for p in range(n_pages)]
        _decode_seq(lam, q_ref[rows, :].astype(F32), kn_ref[rows, :].astype(F32), vn_ref[rows, :].astype(F32),
                    bias_ref, sg_ref[...], k_refs, v_refs, o_dec_ref, rows, n_heads, hd, out_scale)
    _hgrn_new_tokens(rs_ref, lb_ref, gn_ref, s_ref, o_rs_ref, so_ref, r_heads, kd, n_tok, MIX_SEQS)

    @pl.when(t == steps_per_batch - 1)
    def _():
        for h in range(r_heads):
            sfin_ref[0, h] = st_sc[h].T


def _mix(page_table, lam, r, lb, gn, q, kn, vn, bias, sg, rs, state, ck, cv, n_heads, hd, r_heads, kd, out_scale,
         n_tok):
    n_seq, n_pages = page_table.shape
    b, t, _ = r.shape
    _, aw, ps = ck.shape
    vd = aw // n_heads
    c = HG_CHUNK
    tokens = MIX_SEQS * c
    steps_per_batch = t // tokens
    n_steps = b * steps_per_batch
    assert t % tokens == 0 and n_seq == n_steps * MIX_SEQS
    idx = np.arange(c)
    x = idx[:, None] ^ idx[None, :]
    lev = np.where(idx[:, None] > idx[None, :], np.floor(np.log2(np.maximum(x, 1))).astype(np.int32) + 1, -1)
    lev = np.where(idx[:, None] == idx[None, :], 0, lev).astype(np.int32)
    tri = (idx[:, None] >= idx[None, :]).astype(np.float32)
    kern = functools.partial(_mix_kernel, n_pages=n_pages, n_heads=n_heads, hd=hd, r_heads=r_heads, kd=kd,
                             out_scale=out_scale, steps_per_batch=steps_per_batch, n_tok=n_tok)
    c2 = lambda s, pt: (0, 0)
    tok = lambda s, pt: (s // steps_per_batch, s % steps_per_batch, 0)
    seq_rows = lambda s, pt: (s, 0)
    seq_state = lambda s, pt: (s, 0, 0, 0)
    n_k = MIX_SEQS * n_pages
    rows = MIX_SEQS * SAMPLE_PAD
    grid_spec = pltpu.PrefetchScalarGridSpec(
        num_scalar_prefetch=1,
        grid=(n_steps,),
        in_specs=[pl.BlockSpec(memory_space=pltpu.SMEM),
                  pl.BlockSpec((1, tokens, r.shape[2]), tok),
                  pl.BlockSpec((1, r_heads * kd), c2),
                  pl.BlockSpec((1, kd), c2),
                  pl.BlockSpec((c, c), c2),
                  pl.BlockSpec((c, c), c2),
                  pl.BlockSpec((rows, aw), seq_rows),
                  pl.BlockSpec((rows, aw), seq_rows),
                  pl.BlockSpec((rows, aw), seq_rows),
                  pl.BlockSpec(bias.shape, lambda s, pt: (0, 0, 0)),
                  pl.BlockSpec((1, vd), c2),
                  pl.BlockSpec((rows, rs.shape[1]), seq_rows),
                  pl.BlockSpec((MIX_SEQS, r_heads, kd, kd), seq_state),
                  pl.BlockSpec(memory_space=pl.ANY),
                  pl.BlockSpec(memory_space=pl.ANY)],
        out_specs=[pl.BlockSpec((1, tokens, r_heads * kd), tok),
                   pl.BlockSpec((1, r_heads, kd, kd), lambda s, pt: (s // steps_per_batch, 0, 0, 0)),
                   pl.BlockSpec((rows, aw), seq_rows),
                   pl.BlockSpec((rows, r_heads * kd), seq_rows),
                   pl.BlockSpec((MIX_SEQS, r_heads, kd, kd), seq_state)],
        scratch_shapes=[pltpu.VMEM((r_heads, kd, kd), F32),
                        pltpu.VMEM((2, n_k) + ck.shape[1:], ck.dtype),
                        pltpu.VMEM((2, n_k) + cv.shape[1:], cv.dtype),
                        pltpu.SemaphoreType.DMA((2,)),
                        pltpu.SemaphoreType.DMA((2,))],
    )
    return pl.pallas_call(
        kern,
        grid_spec=grid_spec,
        out_shape=[jax.ShapeDtypeStruct((b, t, r_heads * kd), BF16),
                   jax.ShapeDtypeStruct((b, r_heads, kd, kd), F32),
                   jax.ShapeDtypeStruct((n_seq * SAMPLE_PAD, aw), F32),
                   jax.ShapeDtypeStruct((n_seq * SAMPLE_PAD, r_heads * kd), F32),
                   jax.ShapeDtypeStruct(state.shape, F32)],
        compiler_params=pltpu.CompilerParams(dimension_semantics=("arbitrary",),
                                             vmem_limit_bytes=VMEM_LIMIT),
        name="hgrn_attn_decode",
    )(page_table, lam, r, lb, gn, jnp.asarray(tri, BF16), jnp.asarray(lev), q, kn, vn, bias, sg, rs, state, ck, cv)


def _hgrn_new_tokens(r_ref, lb_ref, gn_ref, s_ref, o_ref, so_ref, n_heads, kd, n_tok, n_seq):
    rw = n_heads * kd
    rowid = lax.broadcasted_iota(jnp.int32, (SAMPLE_PAD, kd), 0)
    live = rowid < n_tok
    zpad = jnp.zeros((SAMPLE_PAD, kd), F32)
    pad16 = lambda z: jnp.concatenate([z, zpad], axis=0).astype(BF16)
    ones16 = jnp.ones((2 * SAMPLE_PAD, kd), BF16)
    for g in range(n_seq):
        rows = slice(g * SAMPLE_PAD, (g + 1) * SAMPLE_PAD)
        for h in range(n_heads):
            cols = slice(h * kd, (h + 1) * kd)
            q = r_ref[rows, h * kd:(h + 1) * kd]
            rf = r_ref[rows, rw + h * kd:rw + (h + 1) * kd]
            v = r_ref[rows, 2 * rw + h * kd:2 * rw + (h + 1) * kd]
            rg = r_ref[rows, 3 * rw + h * kd:3 * rw + (h + 1) * kd]
            lb = lb_ref[:, cols]
            f = lb + (1.0 - lb) * jax.nn.sigmoid(rf)
            kk = jnp.where(live, 1.0 - f, 0.0)
            b = jnp.where(live, jnp.log(f) * LOG2E, 0.0)
            for sh in (1, 2, 4):
                b = b + jnp.where(rowid >= sh, pltpu.roll(b, sh, 0), 0.0)
            s0 = s_ref[g, h]
            o = jnp.sum(q * kk, axis=1, keepdims=True) * v
            for d in range(1, n_tok):
                x = q * pltpu.roll(kk, d, 0) * jnp.exp2(b - pltpu.roll(b, d, 0))
                w = jnp.sum(jnp.where(rowid >= d, x, 0.0), axis=1, keepdims=True)
                o = o + w * pltpu.roll(v, d, 0)
            o = o + _nn(pad16(q * jnp.exp2(b)), s0.astype(BF16))[0:SAMPLE_PAD]
            b_last = b[n_tok - 1:n_tok, :]
            decay = jnp.exp2(b_last)
            d_hi = decay.astype(BF16).astype(F32)
            d_split = jnp.where(rowid == 0, d_hi, jnp.where(rowid == 1, decay - d_hi, 0.0))
            decay_cols = _tn(pad16(d_split), ones16)
            so_ref[g, h] = decay_cols * s0 + _tn(pad16(kk * jnp.exp2(b_last - b)), pad16(v))
            ms = jnp.mean(o * o, axis=-1, keepdims=True)
            on = o * lax.rsqrt(ms + EPS) * gn_ref[...] * (rg * jax.nn.sigmoid(rg))
            o_ref[rows, cols] = on.astype(o_ref.dtype)


def _ffn_kernel(*refs, sample, aw, seq_rows):
    if sample:
        (x_ref, oa_ref, or_ref, wo_ref, g2_ref, wg_ref, wu_ref, cw_ref, cb_ref, wd_ref, prev_ref,
         y_ref, a_ref) = refs
    else:
        (x_ref, oa_ref, or_ref, wo_ref, g2_ref, wg_ref, wu_ref, cw_ref, cb_ref, wd_ref,
         y_ref, a_ref, carry_sc) = refs

        @pl.when(pl.program_id(1) == 0)
        def _():
            carry_sc[...] = jnp.zeros(carry_sc.shape, F32)

    x = x_ref[...].astype(F32)
    x1 = x + _nn(oa_ref[...].astype(BF16), wo_ref[0:aw, :]) + _nn(or_ref[...].astype(BF16), wo_ref[aw:, :])
    ms = jnp.mean(x1 * x1, axis=-1, keepdims=True)
    h2 = (x1 * lax.rsqrt(ms + EPS) * g2_ref[...]).astype(BF16)
    tm = x.shape[0]
    a = _nn(h2, wg_ref[...])
    u = _nn(h2, wu_ref[...])
    rowid = lax.broadcasted_iota(jnp.int32, a.shape, 0)
    am1 = pltpu.roll(a, 1, 0)
    am2 = pltpu.roll(a, 2, 0)
    if sample:
        rig = rowid & (seq_rows - 1)
        prev = prev_ref[...]
        am1 = jnp.where(rig == 0, pltpu.roll(prev, tm - 1, 0), am1)
        am2 = jnp.where(rig < 2, prev, am2)
        a_ref[...] = a
    else:
        last = carry_sc[...]
        am1 = jnp.where(rowid == 0, last[SUBLANES - 1:SUBLANES, :], am1)
        am2 = jnp.where(rowid == 0, last[SUBLANES - 2:SUBLANES - 1, :],
                        jnp.where(rowid == 1, last[SUBLANES - 1:SUBLANES, :], am2))
        tail = a[tm - SUBLANES:tm, :]
        carry_sc[...] = tail
        a_ref[0] = tail
    cw = cw_ref[...]
    cv = cb_ref[...] + cw[0:1, :] * am2 + cw[1:2, :] * am1 + cw[2:3, :] * a
    gate = (cv * jax.nn.sigmoid(cv) * u).astype(BF16)
    y_ref[...] = x1 + _nn(gate, wd_ref[...])


def _ffn(x2, oa, orec, wo, g2, wg, wu, cw, cb, wd, prev, tm, n_batch, seq_rows):
    n, d = x2.shape
    aw = oa.shape[1]
    dff = wg.shape[1]
    sample = prev is not None
    assert not sample or (seq_rows >= CONV_W - 1 and seq_rows & (seq_rows - 1) == 0 and tm % seq_rows == 0)
    kern = functools.partial(_ffn_kernel, sample=sample, aw=aw, seq_rows=seq_rows)
    scratch = []
    if sample:
        grid = (n // tm,)
        row = lambda i: (i, 0)
        c2 = lambda i: (0, 0)
        extra_in = [pl.BlockSpec((tm, dff), row)]
        extra_args = [prev]
        a_shape = jax.ShapeDtypeStruct((n, dff), F32)
        a_spec = pl.BlockSpec((tm, dff), row)
        sem = ("arbitrary",)
    else:
        per_b = n // n_batch // tm
        grid = (n_batch, per_b)
        row = lambda b, i: (b * per_b + i, 0)
        c2 = lambda b, i: (0, 0)
        extra_in = []
        extra_args = []
        a_shape = jax.ShapeDtypeStruct((n_batch, SUBLANES, dff), F32)
        a_spec = pl.BlockSpec((1, SUBLANES, dff), lambda b, i: (b, 0, 0))
        scratch.append(pltpu.VMEM((SUBLANES, dff), F32))
        sem = ("arbitrary", "arbitrary")
    return pl.pallas_call(
        kern,
        grid=grid,
        in_specs=[pl.BlockSpec((tm, d), row),
                  pl.BlockSpec((tm, aw), row),
                  pl.BlockSpec((tm, aw), row),
                  _const_spec(wo.shape, c2),
                  pl.BlockSpec((1, d), c2),
                  _const_spec(wg.shape, c2),
                  _const_spec(wu.shape, c2),
                  pl.BlockSpec(cw.shape, c2),
                  pl.BlockSpec(cb.shape, c2),
                  _const_spec(wd.shape, c2)] + extra_in,
        out_specs=[pl.BlockSpec((tm, d), row), a_spec],
        out_shape=[jax.ShapeDtypeStruct((n, d), F32), a_shape],
        scratch_shapes=scratch,
        compiler_params=pltpu.CompilerParams(dimension_semantics=sem, vmem_limit_bytes=VMEM_LIMIT),
        name="ffn_sample" if sample else "ffn_prompt",
    )(x2, oa, orec, wo, g2, wg, wu, cw, cb, wd, *extra_args)


def kernel(x_prompt, x_sample, cache_k, cache_v, page_table, state_hgrn, state_conv, rel_bias, norm1_g, w_in,
           qk_norm_g, lambda_qk, subln_g, hgrn_lb, hgrn_onorm_g, w_out, norm2_g, w_gate, w_up, conv_w, conv_b,
           w_down):
    bp, tp, d = x_prompt.shape
    bs, ts, _ = x_sample.shape
    depth = w_in.shape[0]
    assert depth == 1 and CONV_W - 1 <= ts <= SAMPLE_PAD and conv_w.shape[1] == CONV_W
    n_heads, _, hd = cache_k.shape[3:]
    vd = cache_v.shape[4]
    aw = n_heads * vd
    r_heads, kd = state_hgrn.shape[2], state_hgrn.shape[3]
    rw = r_heads * kd
    dff = w_gate.shape[2]
    page_size = cache_k.shape[2]
    assert page_size >= MAX_DISTANCE and tp % TQ == 0
    l = 0
    lam_init = 0.8 - 0.6 * math.exp(-0.3 * l)
    out_scale = 1.0 - lam_init

    biasp, biass, lam, lb = _tables(rel_bias, lambda_qk[l], hgrn_lb, n_heads, ts, page_size, lam_init)

    w_in_bf = w_in[l].astype(BF16)
    wo_bf = w_out[l].astype(BF16)
    wg_bf = w_gate[l].astype(BF16)
    wu_bf = w_up[l].astype(BF16)
    wd_bf = w_down[l].astype(BF16)
    cw = conv_w[l]
    cb = conv_b[l].reshape(1, dff)
    g1 = norm1_g[l].reshape(1, d)
    g2 = norm2_g[l].reshape(1, d)
    qg = jnp.tile(qk_norm_g[l, 0].reshape(1, 2 * hd), (1, n_heads))
    kg = jnp.tile(qk_norm_g[l, 1].reshape(1, 2 * hd), (1, n_heads))
    seg = np.arange(aw) // hd
    bd = jnp.asarray((seg[:, None] == seg[None, :]).astype(np.float32) / hd, BF16)
    sg = subln_g[l].reshape(1, vd)
    gn = hgrn_onorm_g[l].reshape(1, kd)
    q_scale = hd ** -0.5 * LOG2E

    xp2 = x_prompt.reshape(bp * tp, d)
    qbt, kft, vf, kb, vbt, rp = _proj(xp2, g1, w_in_bf, qg, kg, bd, aw, rw, vd, q_scale, True, BF16, BF16, 512, bp)
    xs_pad = jnp.pad(x_sample, ((0, 0), (0, SAMPLE_PAD - ts), (0, 0))).reshape(bs * SAMPLE_PAD, d)
    qs, ksf, vsf, rs = _proj(xs_pad, g1, w_in_bf, qg, kg, bd, aw, rw, vd, q_scale, False, F32, F32, 512, bs)

    o_att = _attn(lam, qbt, kb.reshape(bp, tp, aw), vbt, biasp, sg, n_heads, hd, out_scale)
    ck = jnp.transpose(cache_k[l], (0, 2, 3, 4, 1)).reshape(cache_k.shape[1], aw, page_size)
    cv = cache_v[l].reshape(cache_v.shape[1], page_size * n_heads, vd)
    o_rec, s_p, o_att_s, o_rec_s, s_s = _mix(page_table, lam, rp.reshape(bp, tp, 4 * rw), lb, gn, qs, ksf, vsf, biass,
                                             sg, rs, state_hgrn[l], ck, cv, n_heads, hd, r_heads, kd, out_scale, ts)

    yp, a_tail = _ffn(xp2, o_att.reshape(bp * tp, aw), o_rec.reshape(bp * tp, rw), wo_bf, g2, wg_bf, wu_bf, cw, cb,
                      wd_bf, None, 512, bp, None)
    k_prompt = kft.reshape(bp, n_heads, 2, hd, tp).transpose(0, 4, 1, 2, 3)[None]
    v_prompt = vf.reshape(1, bp, tp, n_heads, vd)
    conv_prompt = a_tail[:, SUBLANES - (CONV_W - 1):][None]
    unpad = lambda z: z.reshape(bs, SAMPLE_PAD, -1)[:, :ts]
    flat = lambda z: unpad(z).reshape(bs * ts, -1)
    prev = jnp.pad(state_conv[l], ((0, 0), (0, ts - (CONV_W - 1)), (0, 0))).reshape(bs * ts, dff)
    ys, a_s = _ffn(x_sample.reshape(bs * ts, d), flat(o_att_s), flat(o_rec_s), wo_bf, g2, wg_bf, wu_bf, cw, cb,
                   wd_bf, prev, 256, 1, ts)
    y_sample = ys.reshape(bs, ts, d)
    k_sample = unpad(ksf).reshape(1, bs, ts, n_heads, 2, hd)
    v_sample = unpad(vsf).reshape(1, bs, ts, n_heads, vd)
    conv_sample = a_s.reshape(bs, ts, dff)[:, ts - (CONV_W - 1):ts][None]

    return (yp.reshape(bp, tp, d), y_sample, k_prompt, v_prompt, s_p[None], conv_prompt,
            k_sample, v_sample, s_s[None], conv_sample)
```

```python
import functools
import math

import numpy as np
import jax
import jax.numpy as jnp
from jax import lax
from jax.experimental import pallas as pl
from jax.experimental.pallas import tpu as pltpu

F32 = jnp.float32
BF16 = jnp.bfloat16

EPS = 1e-6
N_BUCKETS = 32
MAX_DISTANCE = 128
CONV_W = 3
NEG = -1e30
LOG2E = math.log2(math.e)
LANES = 128
SUBLANES = 8

TQ = 512
HG_CHUNK = 128
MIX_SEQS = 2
SAMPLE_PAD = 8
VMEM_LIMIT = 56 * 1024 * 1024


def _const_spec(shape, index_map):
    return pl.BlockSpec(shape, index_map, pipeline_mode=pl.Buffered(1))


def _nt(a, b):
    return lax.dot_general(a, b, (((1,), (1,)), ((), ())), preferred_element_type=F32)


def _tn(a, b):
    return lax.dot_general(a, b, (((0,), (0,)), ((), ())), preferred_element_type=F32)


def _nn(a, b):
    return jnp.dot(a, b, preferred_element_type=F32)


def _rel_bucket_np(n):
    n = np.maximum(n, 0)
    max_exact = N_BUCKETS // 2
    nf = np.maximum(n, 1).astype(np.float32)
    large = max_exact + (np.log(nf / np.float32(max_exact)) / np.float32(math.log(MAX_DISTANCE / max_exact))
                         * np.float32(N_BUCKETS - max_exact)).astype(np.int32)
    large = np.minimum(large, N_BUCKETS - 1)
    return np.where(n < max_exact, n, large).astype(np.int32)


def _tables_kernel(relb_ref, lq_ref, hlb_ref, bkp_ref, bks_ref,
                   biasp_ref, biass_ref, lam_ref, lb_ref, *, n_heads, lam_init, near_blocks):
    for h in range(n_heads):
        far = relb_ref[N_BUCKETS - 1, h]
        for kind in range(2):
            biasp_ref[h, kind] = jnp.where(bkp_ref[kind] < 0, NEG, 0.0).astype(F32)
            for (r0, c0) in near_blocks[kind]:
                bk = bkp_ref[kind, r0:r0 + LANES, c0:c0 + LANES]
                acc = jnp.where(bk < 0, NEG, 0.0).astype(F32)
                for b in range(N_BUCKETS - 1):
                    acc = jnp.where(bk == b, (relb_ref[b, h] - far) * LOG2E, acc)
                biasp_ref[h, kind, r0:r0 + LANES, c0:c0 + LANES] = acc
            rows = bks_ref.shape[1] // n_heads
            bk = bks_ref[kind, h * rows:(h + 1) * rows, :]
            acc = jnp.where(bk < 0, NEG, 0.0).astype(F32)
            for b in range(N_BUCKETS - 1):
                acc = jnp.where(bk == b, (relb_ref[b, h] - far) * LOG2E, acc)
            biass_ref[kind, h * rows:(h + 1) * rows, :] = acc
    lq = lq_ref[...].astype(F32)
    s1 = jnp.sum(lq[0:1] * lq[1:2], axis=1, keepdims=True)
    s2 = jnp.sum(lq[2:3] * lq[3:4], axis=1, keepdims=True)
    lam = jnp.exp(s1) - jnp.exp(s2) + lam_init
    lam_ref[...] = jnp.broadcast_to(lam, lam_ref.shape)
    hl = hlb_ref[...].astype(F32)
    mx = jnp.max(hl, axis=0, keepdims=True)
    e = jnp.exp(hl - mx)
    lb_ref[...] = e[0:1] / jnp.sum(e, axis=0, keepdims=True)


def _tables(rel_bias, lambda_qk_l, hgrn_lb, n_heads, n_tok, page_size, lam_init):
    c = np.arange(TQ)[:, None]
    r = np.arange(TQ)[None, :]
    diag = np.where(c <= r, _rel_bucket_np(r - c), -1)
    prev = _rel_bucket_np(TQ + r - c)
    bkp_np = np.stack([diag, prev]).astype(np.int32)
    bkp = jnp.asarray(bkp_np)
    near_blocks = tuple(
        tuple((r0, c0) for r0 in range(0, TQ, LANES) for c0 in range(0, TQ, LANES)
              if np.any((bkp_np[kind, r0:r0 + LANES, c0:c0 + LANES] >= 0)
                        & (bkp_np[kind, r0:r0 + LANES, c0:c0 + LANES] < N_BUCKETS - 1)))
        for kind in range(2))
    t = np.tile(np.arange(SAMPLE_PAD), 2 * n_heads)[:, None]
    cc = np.arange(page_size)[None, :]
    last_page = _rel_bucket_np(page_size + t - cc)
    new_page = np.where(cc <= np.minimum(t, n_tok - 1), _rel_bucket_np(t - cc), -1)
    bks = jnp.asarray(np.stack([last_page, new_page]).astype(np.int32))
    rows_s = 2 * n_heads * SAMPLE_PAD
    kern = functools.partial(_tables_kernel, n_heads=n_heads, lam_init=lam_init, near_blocks=near_blocks)
    return pl.pallas_call(
        kern,
        out_shape=(jax.ShapeDtypeStruct((n_heads, 2, TQ, TQ), F32),
                   jax.ShapeDtypeStruct((2, rows_s, page_size), F32),
                   jax.ShapeDtypeStruct((SUBLANES, LANES), F32),
                   jax.ShapeDtypeStruct((1, hgrn_lb.shape[1]), F32)),
        in_specs=[pl.BlockSpec(memory_space=pltpu.SMEM),
                  pl.BlockSpec(memory_space=pltpu.VMEM),
                  pl.BlockSpec(memory_space=pltpu.VMEM),
                  pl.BlockSpec(memory_space=pltpu.VMEM),
                  pl.BlockSpec(memory_space=pltpu.VMEM)],
        name="tables",
    )(rel_bias, lambda_qk_l, hgrn_lb, bkp, bks)


def _proj_kernel(x_ref, g1_ref, w_ref, qg_ref, kg_ref, bd_ref, *out_refs, aw, rw, vd, q_scale, emit_bf16):
    if emit_bf16:
        q_ref, kf_ref, vf_ref, kb_ref, vb_ref, r_ref = out_refs
    else:
        q_ref, kf_ref, vf_ref, r_ref = out_refs
    x = x_ref[...].astype(F32)
    ms = jnp.mean(x * x, axis=-1, keepdims=True)
    h = (x * lax.rsqrt(ms + EPS) * g1_ref[...]).astype(BF16)

    def seg_norm(z, g):
        msq = _nn((z * z).astype(BF16), bd_ref[...])
        return z * lax.rsqrt(msq + EPS) * g

    zq = _nn(h, w_ref[:, 0:aw])
    qn = seg_norm(zq, qg_ref[...]) * q_scale
    if emit_bf16:
        q_ref[0] = qn.T.astype(q_ref.dtype)
    else:
        q_ref[...] = qn.astype(q_ref.dtype)
    zk = _nn(h, w_ref[:, aw:2 * aw])
    kn = seg_norm(zk, kg_ref[...])
    zv = _nn(h, w_ref[:, 2 * aw:3 * aw])
    if emit_bf16:
        n_heads = aw // vd
        kf_ref[0] = kn.T
        kb_ref[...] = kn.astype(BF16)
        vb_ref[0] = zv.T.astype(BF16)
        for hh in range(n_heads):
            vf_ref[pl.ds(hh, zv.shape[0], stride=n_heads), :] = zv[:, hh * vd:(hh + 1) * vd]
    else:
        kf_ref[...] = kn
        vf_ref[...] = zv
    for g in range(4):
        z = _nn(h, w_ref[:, 3 * aw + g * rw:3 * aw + (g + 1) * rw])
        r_ref[:, g * rw:(g + 1) * rw] = z.astype(r_ref.dtype)


def _proj(x2, g1, w_bf, qg, kg, bd, aw, rw, vd, q_scale, emit_bf16, q_dtype, r_dtype, tm, n_batch):
    n, d = x2.shape
    kern = functools.partial(_proj_kernel, aw=aw, rw=rw, vd=vd, q_scale=q_scale, emit_bf16=emit_bf16)
    row = lambda i: (i, 0)
    const = lambda i: (0, 0)
    if emit_bf16:
        t = n // n_batch
        per_b = t // tm
        n_heads = aw // vd
        xposed = lambda i: (i // per_b, 0, i % per_b)
        out_shape = [jax.ShapeDtypeStruct((n_batch, aw, t), q_dtype),
                     jax.ShapeDtypeStruct((n_batch, aw, t), F32),
                     jax.ShapeDtypeStruct((n * n_heads, vd), F32),
                     jax.ShapeDtypeStruct((n, aw), BF16),
                     jax.ShapeDtypeStruct((n_batch, aw, t), BF16)]
        out_specs = [pl.BlockSpec((1, aw, tm), xposed),
                     pl.BlockSpec((1, aw, tm), xposed),
                     pl.BlockSpec((tm * n_heads, vd), row),
                     pl.BlockSpec((tm, aw), row),
                     pl.BlockSpec((1, aw, tm), xposed)]
    else:
        out_shape = [jax.ShapeDtypeStruct((n, aw), q_dtype),
                     jax.ShapeDtypeStruct((n, aw), F32),
                     jax.ShapeDtypeStruct((n, aw), F32)]
        out_specs = [pl.BlockSpec((tm, aw), row)] * 3
    out_shape.append(jax.ShapeDtypeStruct((n, 4 * rw), r_dtype))
    out_specs.append(pl.BlockSpec((tm, 4 * rw), row))
    return pl.pallas_call(
        kern,
        grid=(n // tm,),
        in_specs=[pl.BlockSpec((tm, d), row),
                  pl.BlockSpec((1, d), const),
                  _const_spec(w_bf.shape, const),
                  pl.BlockSpec((1, aw), const),
                  pl.BlockSpec((1, aw), const),
                  pl.BlockSpec((aw, aw), const)],
        out_specs=out_specs,
        out_shape=out_shape,
        compiler_params=pltpu.CompilerParams(dimension_semantics=("arbitrary",),
                                             vmem_limit_bytes=VMEM_LIMIT),
        name="proj_bf16" if emit_bf16 else "proj_f32",
    )(x2, g1, w_bf, qg, kg, bd)


def _attn_kernel(lam_ref, q_ref, k_ref, vt_ref, bias_ref, sg_ref, o_ref,
                 qst_sc, sa_sc, ma_sc, sb_sc, mb_sc, m_sc, l_sc, acc_sc, *, hd, out_scale):
    i = pl.program_id(2)

    qt = q_ref[0].astype(F32)
    row = lax.broadcasted_iota(jnp.int32, qt.shape, 0)
    qst_sc[...] = jnp.concatenate([jnp.where(row < hd, qt, 0.0), jnp.where(row >= hd, qt, 0.0)],
                                  axis=1).astype(BF16)
    m_sc[...] = jnp.full(m_sc.shape, NEG, F32)
    l_sc[...] = jnp.zeros(l_sc.shape, F32)
    acc_sc[...] = jnp.zeros(acc_sc.shape, F32)

    bufs = ((sa_sc, ma_sc), (sb_sc, mb_sc))

    def issue(buf, g, with_max):
        start = pl.multiple_of(g * TQ, TQ)
        s = _nn(k_ref[0, pl.ds(start, TQ), :], qst_sc[...])
        buf[0][...] = s
        if with_max:
            buf[1][...] = jnp.max(s, axis=0, keepdims=True)

    def consume(buf, g, kind):
        s = buf[0][...]
        if kind is None:
            s_max = buf[1][...]
        elif kind == 1:
            n = MAX_DISTANCE
            bias = bias_ref[0, 1, TQ - n:TQ, 0:n]
            near = s[TQ - n:TQ]
            near = jnp.concatenate([near[:, 0:n] + bias, near[:, n:TQ], near[:, TQ:TQ + n] + bias,
                                    near[:, TQ + n:2 * TQ]], axis=1)
            s = jnp.concatenate([s[0:TQ - n], near], axis=0)
            far_max = buf[1][...]
            s_max = jnp.concatenate([jnp.max(s[:, 0:n], axis=0, keepdims=True), far_max[:, n:TQ],
                                     jnp.max(s[:, TQ:TQ + n], axis=0, keepdims=True), far_max[:, TQ + n:2 * TQ]],
                                    axis=1)
        else:
            bias = bias_ref[0, kind]
            s = jnp.concatenate([s[:, 0:TQ] + bias, s[:, TQ:2 * TQ] + bias], axis=1)
            s_max = jnp.max(s, axis=0, keepdims=True)
        m_old = m_sc[...]
        m_new = jnp.maximum(m_old, s_max)
        alpha = jnp.exp2(m_old - m_new)
        p = jnp.exp2(s - m_new)
        l_sc[...] = alpha * l_sc[...] + jnp.sum(p, axis=0, keepdims=True)
        start = pl.multiple_of(g * TQ, TQ)
        acc_sc[...] = alpha * acc_sc[...] + _nn(vt_ref[0, :, pl.ds(start, TQ)], p.astype(BF16))
        m_sc[...] = m_new

    n_pure = jnp.maximum(i - 1, 0)
    issue(bufs[0], 0, True)

    def pair(jj, carry):
        g = 2 * jj
        issue(bufs[1], g + 1, True)
        consume(bufs[0], g, None)
        issue(bufs[0], g + 2, True)
        consume(bufs[1], g + 1, None)
        return carry

    lax.fori_loop(0, n_pure // 2, pair, 0)
    odd = (n_pure & 1) == 1

    def finish(cur, other):
        @pl.when(i > 0)
        def _():
            issue(other, i, False)
            consume(cur, i - 1, 1)
            consume(other, i, 0)

        @pl.when(i == 0)
        def _():
            consume(cur, 0, 0)

    @pl.when(odd)
    def _():
        issue(bufs[1], n_pure, True)
        consume(bufs[0], n_pure - 1, None)
        finish(bufs[1], bufs[0])

    @pl.when(jnp.logical_not(odd))
    def _():
        finish(bufs[0], bufs[1])

    a = acc_sc[...] * (1.0 / l_sc[...])
    ot = a[:, 0:TQ] - lam_ref[0, 0] * a[:, TQ:2 * TQ]
    ms = jnp.mean(ot * ot, axis=0, keepdims=True)
    o = (ot * lax.rsqrt(ms + EPS)).T
    o_ref[0] = (o * (sg_ref[...] * out_scale)).astype(o_ref.dtype)


def _attn(lam, qt, k, vt, bias, sg, n_heads, hd, out_scale):
    b, t, aw = k.shape
    vd = aw // n_heads
    assert t % TQ == 0
    kern = functools.partial(_attn_kernel, hd=hd, out_scale=out_scale)
    return pl.pallas_call(
        kern,
        grid=(b, n_heads, t // TQ),
        in_specs=[pl.BlockSpec(memory_space=pltpu.SMEM),
                  pl.BlockSpec((1, vd, TQ), lambda bi, h, i: (bi, h, i)),
                  pl.BlockSpec((1, t, vd), lambda bi, h, i: (bi, 0, h)),
                  pl.BlockSpec((1, vd, t), lambda bi, h, i: (bi, h, 0)),
                  pl.BlockSpec((1, 2, TQ, TQ), lambda bi, h, i: (h, 0, 0, 0)),
                  pl.BlockSpec((1, vd), lambda bi, h, i: (0, 0))],
        out_specs=pl.BlockSpec((1, TQ, vd), lambda bi, h, i: (bi, i, h)),
        out_shape=jax.ShapeDtypeStruct((b, t, aw), BF16),
        scratch_shapes=[pltpu.VMEM((vd, 2 * TQ), BF16),
                        pltpu.VMEM((TQ, 2 * TQ), F32),
                        pltpu.VMEM((1, 2 * TQ), F32),
                        pltpu.VMEM((TQ, 2 * TQ), F32),
                        pltpu.VMEM((1, 2 * TQ), F32),
                        pltpu.VMEM((1, 2 * TQ), F32),
                        pltpu.VMEM((1, 2 * TQ), F32),
                        pltpu.VMEM((vd, 2 * TQ), F32)],
        compiler_params=pltpu.CompilerParams(dimension_semantics=("arbitrary",) * 3,
                                             vmem_limit_bytes=VMEM_LIMIT),
        name="attn_prompt",
    )(lam, qt, k, vt, bias, sg)


def _decode_seq(lam, q8, kn8, vn8, bias_ref, sg, k_refs, v_refs, o_ref, rows_out, n_heads, hd, out_scale):
    n_pages = len(k_refs)
    aw = q8.shape[1]
    vd = aw // n_heads
    ps = k_refs[0].shape[1]
    lane = lax.broadcasted_iota(jnp.int32, q8.shape, 1)
    pieces = [jnp.where((lane >= hm * hd) & (lane < (hm + 1) * hd), q8, 0.0) for hm in range(2 * n_heads)]
    qbd = jnp.concatenate(pieces, axis=0).astype(BF16)
    s_list = []
    for p in range(n_pages):
        s = _nn(qbd, k_refs[p][...].astype(BF16))
        if p == n_pages - 1:
            s = s + bias_ref[0]
        s_list.append(s)
    pad = jnp.zeros((ps - SAMPLE_PAD, aw), F32)
    k_new = jnp.concatenate([kn8, pad], axis=0).astype(BF16)
    v_new = jnp.concatenate([vn8, pad], axis=0).astype(BF16)
    s_list.append(_nt(qbd, k_new) + bias_ref[1])
    s_all = jnp.concatenate(s_list, axis=1)
    m = jnp.max(s_all, axis=1, keepdims=True)
    e = jnp.exp2(s_all - m)
    inv = 1.0 / jnp.sum(e, axis=1, keepdims=True)
    eb = e.astype(BF16)
    rows = 2 * SAMPLE_PAD
    for h in range(n_heads):
        eh = eb[h * rows:(h + 1) * rows]
        acc = _nn(eh[:, n_pages * ps:(n_pages + 1) * ps], v_new[:, h * vd:(h + 1) * vd])
        for p in range(n_pages):
            vh = v_refs[p][pl.ds(h, ps, stride=n_heads), :].astype(BF16)
            acc = acc + _nn(eh[:, p * ps:(p + 1) * ps], vh)
        acc = acc * inv[h * rows:(h + 1) * rows]
        o = acc[0:SAMPLE_PAD] - lam * acc[SAMPLE_PAD:rows]
        ms = jnp.mean(o * o, axis=-1, keepdims=True)
        o_ref[rows_out, h * vd:(h + 1) * vd] = (o * lax.rsqrt(ms + EPS) * (sg * out_scale)).astype(o_ref.dtype)


def _mid_reference(b, level):
    c, w = b.shape
    p = 1 << level
    half = p // 2
    if p >= SUBLANES:
        bp = b.reshape(c // p, p, w)
        return jnp.broadcast_to(bp[:, half - 1:half, :], (c // p, p, w)).reshape(c, w)
    b8 = b.reshape(c // SUBLANES, SUBLANES, w)
    rig = lax.broadcasted_iota(jnp.int32, b8.shape, 1)
    out = None
    for start in range(SUBLANES - p, -1, -p):
        row = jnp.broadcast_to(b8[:, start + half - 1:start + half, :], b8.shape)
        out = row if out is None else jnp.where(rig < start + p, row, out)
    return out.reshape(c, w)


def _upper_lower(up, low, level):
    c, w = up.shape
    half = (1 << level) // 2
    if half >= SUBLANES:
        shape = (c // (2 * half), 2, half, w)
        return jnp.concatenate([low.reshape(shape)[:, 0:1], up.reshape(shape)[:, 1:2]], axis=1).reshape(c, w)
    rowid = lax.broadcasted_iota(jnp.int32, up.shape, 0)
    return jnp.where((rowid & half) != 0, up, low)


def _hgrn_chunk(r_ref, r0, lb_ref, gn_ref, tri, lev, o_ref, st_sc, n_heads, kd):
    c = HG_CHUNK
    rw = n_heads * kd
    n_levels = c.bit_length() - 1
    head = lambda z, h: z[:, h * kd:(h + 1) * kd]
    q = r_ref[0, r0:r0 + c, 0:rw].astype(F32)
    rf = r_ref[0, r0:r0 + c, rw:2 * rw].astype(F32)
    v = r_ref[0, r0:r0 + c, 2 * rw:3 * rw].astype(BF16)
    rg = r_ref[0, r0:r0 + c, 3 * rw:4 * rw].astype(F32)
    lb = lb_ref[...]
    f = lb + (1.0 - lb) * jax.nn.sigmoid(rf)
    logf = jnp.log(f) * LOG2E
    kk = 1.0 - f
    hi = logf.astype(BF16)
    lo = (logf - hi.astype(F32)).astype(BF16)
    b = _nn(tri, hi) + _nn(tri, lo)
    qb = q.astype(BF16)
    kb = kk.astype(BF16)
    diag = lev == 0
    a = [jnp.where(diag, _nt(head(qb, h), head(kb, h)), 0.0) for h in range(n_heads)]
    for level in range(1, n_levels + 1):
        e = jnp.exp2(-jnp.abs(b - _mid_reference(b, level)))
        z = (_upper_lower(q, kk, level) * e).astype(BF16)
        here = lev == level
        a = [jnp.where(here, _nt(head(z, h), head(z, h)), a[h]) for h in range(n_heads)]
    b_last = b[c - 1:c, :]
    q_hat = (q * jnp.exp2(b)).astype(BF16)
    k_hat = (kk * jnp.exp2(b_last - b)).astype(BF16)
    decay = jnp.exp2(b_last)
    outs = []
    for h in range(n_heads):
        st = st_sc[h]
        o = _nn(a[h].astype(BF16), head(v, h)) + _nt(head(q_hat, h), st.astype(BF16))
        st_sc[h] = head(decay, h) * st + _tn(head(v, h), head(k_hat, h))
        ms = jnp.mean(o * o, axis=-1, keepdims=True)
        outs.append(o * lax.rsqrt(ms + EPS) * gn_ref[...])
    on = jnp.concatenate(outs, axis=1) * (rg * jax.nn.sigmoid(rg))
    o_ref[0, r0:r0 + c, :] = on.astype(o_ref.dtype)


def _mix_kernel(pt_ref, lam_ref, r_ref, lb_ref, gn_ref, tri_ref, lev_ref, q_ref, kn_ref, vn_ref, bias_ref, sg_ref,
                rs_ref, s_ref, ck_hbm, cv_hbm, o_rec_ref, sfin_ref, o_dec_ref, o_rs_ref, so_ref,
                st_sc, kbuf, vbuf, ksem, vsem,
                *, n_pages, n_heads, hd, r_heads, kd, out_scale, steps_per_batch, n_tok):
    step = pl.program_id(0)
    slot = lax.rem(step, 2)
    t = lax.rem(step, steps_per_batch)

    def page_copies(at_step, at_slot):
        copies = []
        for j in range(MIX_SEQS):
            for p in range(n_pages):
                page = pt_ref[at_step * MIX_SEQS + j, p]
                i = j * n_pages + p
                copies.append(pltpu.make_async_copy(ck_hbm.at[page], kbuf.at[at_slot, i], ksem.at[at_slot]))
                copies.append(pltpu.make_async_copy(cv_hbm.at[page], vbuf.at[at_slot, i], vsem.at[at_slot]))
        return copies

    @pl.when(step == 0)
    def _():
        for cp in page_copies(0, 0):
            cp.start()

    @pl.when(step + 1 < pl.num_programs(0))
    def _():
        for cp in page_copies(step + 1, 1 - slot):
            cp.start()

    @pl.when(t == 0)
    def _():
        st_sc[...] = jnp.zeros(st_sc.shape, F32)

    for cp in page_copies(step, slot):
        cp.wait()

    lev = lev_ref[...]
    tri = tri_ref[...]
    lam = lam_ref[0, 0]
    for c in range(MIX_SEQS):
        _hgrn_chunk(r_ref, c * HG_CHUNK, lb_ref, gn_ref, tri, lev, o_rec_ref, st_sc, r_heads, kd)
        rows = slice(c * SAMPLE_PAD, (c + 1) * SAMPLE_PAD)
        k_refs = [kbuf.at[slot, c * n_pages + p] for p in range(n_pages)]
        v_refs = [vbuf.at[slot, c * n_pages + p] for p in range(n_pages)]
        _decode_seq(lam, q_ref[rows, :].astype(F32), kn_ref[rows, :].astype(F32), vn_ref[rows, :].astype(F32),
                    bias_ref, sg_ref[...], k_refs, v_refs, o_dec_ref, rows, n_heads, hd, out_scale)
    _hgrn_new_tokens(rs_ref, lb_ref, gn_ref, s_ref, o_rs_ref, so_ref, r_heads, kd, n_tok, MIX_SEQS)

    @pl.when(t == steps_per_batch - 1)
    def _():
        for h in range(r_heads):
            sfin_ref[0, h] = st_sc[h].T


def _mix(page_table, lam, r, lb, gn, q, kn, vn, bias, sg, rs, state, ck, cv, n_heads, hd, r_heads, kd, out_scale,
         n_tok):
    n_seq, n_pages = page_table.shape
    b, t, _ = r.shape
    _, aw, ps = ck.shape
    vd = aw // n_heads
    c = HG_CHUNK
    tokens = MIX_SEQS * c
    steps_per_batch = t // tokens
    n_steps = b * steps_per_batch
    assert t % tokens == 0 and n_seq == n_steps * MIX_SEQS
    idx = np.arange(c)
    x = idx[:, None] ^ idx[None, :]
    lev = np.where(idx[:, None] > idx[None, :], np.floor(np.log2(np.maximum(x, 1))).astype(np.int32) + 1, -1)
    lev = np.where(idx[:, None] == idx[None, :], 0, lev).astype(np.int32)
    tri = (idx[:, None] >= idx[None, :]).astype(np.float32)
    kern = functools.partial(_mix_kernel, n_pages=n_pages, n_heads=n_heads, hd=hd, r_heads=r_heads, kd=kd,
                             out_scale=out_scale, steps_per_batch=steps_per_batch, n_tok=n_tok)
    c2 = lambda s, pt: (0, 0)
    tok = lambda s, pt: (s // steps_per_batch, s % steps_per_batch, 0)
    seq_rows = lambda s, pt: (s, 0)
    seq_state = lambda s, pt: (s, 0, 0, 0)
    n_k = MIX_SEQS * n_pages
    rows = MIX_SEQS * SAMPLE_PAD
    grid_spec = pltpu.PrefetchScalarGridSpec(
        num_scalar_prefetch=1,
        grid=(n_steps,),
        in_specs=[pl.BlockSpec(memory_space=pltpu.SMEM),
                  pl.BlockSpec((1, tokens, r.shape[2]), tok),
                  pl.BlockSpec((1, r_heads * kd), c2),
                  pl.BlockSpec((1, kd), c2),
                  pl.BlockSpec((c, c), c2),
                  pl.BlockSpec((c, c), c2),
                  pl.BlockSpec((rows, aw), seq_rows),
                  pl.BlockSpec((rows, aw), seq_rows),
                  pl.BlockSpec((rows, aw), seq_rows),
                  pl.BlockSpec(bias.shape, lambda s, pt: (0, 0, 0)),
                  pl.BlockSpec((1, vd), c2),
                  pl.BlockSpec((rows, rs.shape[1]), seq_rows),
                  pl.BlockSpec((MIX_SEQS, r_heads, kd, kd), seq_state),
                  pl.BlockSpec(memory_space=pl.ANY),
                  pl.BlockSpec(memory_space=pl.ANY)],
        out_specs=[pl.BlockSpec((1, tokens, r_heads * kd), tok),
                   pl.BlockSpec((1, r_heads, kd, kd), lambda s, pt: (s // steps_per_batch, 0, 0, 0)),
                   pl.BlockSpec((rows, aw), seq_rows),
                   pl.BlockSpec((rows, r_heads * kd), seq_rows),
                   pl.BlockSpec((MIX_SEQS, r_heads, kd, kd), seq_state)],
        scratch_shapes=[pltpu.VMEM((r_heads, kd, kd), F32),
                        pltpu.VMEM((2, n_k) + ck.shape[1:], ck.dtype),
                        pltpu.VMEM((2, n_k) + cv.shape[1:], cv.dtype),
                        pltpu.SemaphoreType.DMA((2,)),
                        pltpu.SemaphoreType.DMA((2,))],
    )
    return pl.pallas_call(
        kern,
        grid_spec=grid_spec,
        out_shape=[jax.ShapeDtypeStruct((b, t, r_heads * kd), BF16),
                   jax.ShapeDtypeStruct((b, r_heads, kd, kd), F32),
                   jax.ShapeDtypeStruct((n_seq * SAMPLE_PAD, aw), F32),
                   jax.ShapeDtypeStruct((n_seq * SAMPLE_PAD, r_heads * kd), F32),
                   jax.ShapeDtypeStruct(state.shape, F32)],
        compiler_params=pltpu.CompilerParams(dimension_semantics=("arbitrary",),
                                             vmem_limit_bytes=VMEM_LIMIT),
        name="hgrn_attn_decode",
    )(page_table, lam, r, lb, gn, jnp.asarray(tri, BF16), jnp.asarray(lev), q, kn, vn, bias, sg, rs, state, ck, cv)


def _hgrn_new_tokens(r_ref, lb_ref, gn_ref, s_ref, o_ref, so_ref, n_heads, kd, n_tok, n_seq):
    rw = n_heads * kd
    rowid = lax.broadcasted_iota(jnp.int32, (SAMPLE_PAD, kd), 0)
    live = rowid < n_tok
    zpad = jnp.zeros((SAMPLE_PAD, kd), F32)
    pad16 = lambda z: jnp.concatenate([z, zpad], axis=0).astype(BF16)
    ones16 = jnp.ones((2 * SAMPLE_PAD, kd), BF16)
    for g in range(n_seq):
        rows = slice(g * SAMPLE_PAD, (g + 1) * SAMPLE_PAD)
        for h in range(n_heads):
            cols = slice(h * kd, (h + 1) * kd)
            q = r_ref[rows, h * kd:(h + 1) * kd]
            rf = r_ref[rows, rw + h * kd:rw + (h + 1) * kd]
            v = r_ref[rows, 2 * rw + h * kd:2 * rw + (h + 1) * kd]
            rg = r_ref[rows, 3 * rw + h * kd:3 * rw + (h + 1) * kd]
            lb = lb_ref[:, cols]
            f = lb + (1.0 - lb) * jax.nn.sigmoid(rf)
            kk = jnp.where(live, 1.0 - f, 0.0)
            b = jnp.where(live, jnp.log(f) * LOG2E, 0.0)
            for sh in (1, 2, 4):
                b = b + jnp.where(rowid >= sh, pltpu.roll(b, sh, 0), 0.0)
            s0 = s_ref[g, h]
            o = jnp.sum(q * kk, axis=1, keepdims=True) * v
            for d in range(1, n_tok):
                x = q * pltpu.roll(kk, d, 0) * jnp.exp2(b - pltpu.roll(b, d, 0))
                w = jnp.sum(jnp.where(rowid >= d, x, 0.0), axis=1, keepdims=True)
                o = o + w * pltpu.roll(v, d, 0)
            o = o + _nn(pad16(q * jnp.exp2(b)), s0.astype(BF16))[0:SAMPLE_PAD]
            b_last = b[n_tok - 1:n_tok, :]
            decay = jnp.exp2(b_last)
            d_hi = decay.astype(BF16).astype(F32)
            d_split = jnp.where(rowid == 0, d_hi, jnp.where(rowid == 1, decay - d_hi, 0.0))
            decay_cols = _tn(pad16(d_split), ones16)
            so_ref[g, h] = decay_cols * s0 + _tn(pad16(kk * jnp.exp2(b_last - b)), pad16(v))
            ms = jnp.mean(o * o, axis=-1, keepdims=True)
            on = o * lax.rsqrt(ms + EPS) * gn_ref[...] * (rg * jax.nn.sigmoid(rg))
            o_ref[rows, cols] = on.astype(o_ref.dtype)


def _ffn_kernel(*refs, sample, aw, seq_rows):
    if sample:
        (x_ref, oa_ref, or_ref, wo_ref, g2_ref, wg_ref, wu_ref, cw_ref, cb_ref, wd_ref, prev_ref,
         y_ref, a_ref) = refs
    else:
        (x_ref, oa_ref, or_ref, wo_ref, g2_ref, wg_ref, wu_ref, cw_ref, cb_ref, wd_ref,
         y_ref, a_ref, carry_sc) = refs

        @pl.when(pl.program_id(1) == 0)
        def _():
            carry_sc[...] = jnp.zeros(carry_sc.shape, F32)

    x = x_ref[...].astype(F32)
    x1 = x + _nn(oa_ref[...].astype(BF16), wo_ref[0:aw, :]) + _nn(or_ref[...].astype(BF16), wo_ref[aw:, :])
    ms = jnp.mean(x1 * x1, axis=-1, keepdims=True)
    h2 = (x1 * lax.rsqrt(ms + EPS) * g2_ref[...]).astype(BF16)
    tm = x.shape[0]
    a = _nn(h2, wg_ref[...])
    u = _nn(h2, wu_ref[...])
    rowid = lax.broadcasted_iota(jnp.int32, a.shape, 0)
    am1 = pltpu.roll(a, 1, 0)
    am2 = pltpu.roll(a, 2, 0)
    if sample:
        rig = rowid & (seq_rows - 1)
        prev = prev_ref[...]
        am1 = jnp.where(rig == 0, pltpu.roll(prev, tm - 1, 0), am1)
        am2 = jnp.where(rig < 2, prev, am2)
        a_ref[...] = a
    else:
        last = carry_sc[...]
        am1 = jnp.where(rowid == 0, last[SUBLANES - 1:SUBLANES, :], am1)
        am2 = jnp.where(rowid == 0, last[SUBLANES - 2:SUBLANES - 1, :],
                        jnp.where(rowid == 1, last[SUBLANES - 1:SUBLANES, :], am2))
        tail = a[tm - SUBLANES:tm, :]
        carry_sc[...] = tail
        a_ref[0] = tail
    cw = cw_ref[...]
    cv = cb_ref[...] + cw[0:1, :] * am2 + cw[1:2, :] * am1 + cw[2:3, :] * a
    gate = (cv * jax.nn.sigmoid(cv) * u).astype(BF16)
    y_ref[...] = x1 + _nn(gate, wd_ref[...])


def _ffn(x2, oa, orec, wo, g2, wg, wu, cw, cb, wd, prev, tm, n_batch, seq_rows):
    n, d = x2.shape
    aw = oa.shape[1]
    dff = wg.shape[1]
    sample = prev is not None
    assert not sample or (seq_rows >= CONV_W - 1 and seq_rows & (seq_rows - 1) == 0 and tm % seq_rows == 0)
    kern = functools.partial(_ffn_kernel, sample=sample, aw=aw, seq_rows=seq_rows)
    scratch = []
    if sample:
        grid = (n // tm,)
        row = lambda i: (i, 0)
        c2 = lambda i: (0, 0)
        extra_in = [pl.BlockSpec((tm, dff), row)]
        extra_args = [prev]
        a_shape = jax.ShapeDtypeStruct((n, dff), F32)
        a_spec = pl.BlockSpec((tm, dff), row)
        sem = ("arbitrary",)
    else:
        per_b = n // n_batch // tm
        grid = (n_batch, per_b)
        row = lambda b, i: (b * per_b + i, 0)
        c2 = lambda b, i: (0, 0)
        extra_in = []
        extra_args = []
        a_shape = jax.ShapeDtypeStruct((n_batch, SUBLANES, dff), F32)
        a_spec = pl.BlockSpec((1, SUBLANES, dff), lambda b, i: (b, 0, 0))
        scratch.append(pltpu.VMEM((SUBLANES, dff), F32))
        sem = ("arbitrary", "arbitrary")
    return pl.pallas_call(
        kern,
        grid=grid,
        in_specs=[pl.BlockSpec((tm, d), row),
                  pl.BlockSpec((tm, aw), row),
                  pl.BlockSpec((tm, aw), row),
                  _const_spec(wo.shape, c2),
                  pl.BlockSpec((1, d), c2),
                  _const_spec(wg.shape, c2),
                  _const_spec(wu.shape, c2),
                  pl.BlockSpec(cw.shape, c2),
                  pl.BlockSpec(cb.shape, c2),
                  _const_spec(wd.shape, c2)] + extra_in,
        out_specs=[pl.BlockSpec((tm, d), row), a_spec],
        out_shape=[jax.ShapeDtypeStruct((n, d), F32), a_shape],
        scratch_shapes=scratch,
        compiler_params=pltpu.CompilerParams(dimension_semantics=sem, vmem_limit_bytes=VMEM_LIMIT),
        name="ffn_sample" if sample else "ffn_prompt",
    )(x2, oa, orec, wo, g2, wg, wu, cw, cb, wd, *extra_args)


def kernel(x_prompt, x_sample, cache_k, cache_v, page_table, state_hgrn, state_conv, rel_bias, norm1_g, w_in,
           qk_norm_g, lambda_qk, subln_g, hgrn_lb, hgrn_onorm_g, w_out, norm2_g, w_gate, w_up, conv_w, conv_b,
           w_down):
    bp, tp, d = x_prompt.shape
    bs, ts, _ = x_sample.shape
    depth = w_in.shape[0]
    assert depth == 1 and CONV_W - 1 <= ts <= SAMPLE_PAD and conv_w.shape[1] == CONV_W
    n_heads, _, hd = cache_k.shape[3:]
    vd = cache_v.shape[4]
    aw = n_heads * vd
    r_heads, kd = state_hgrn.shape[2], state_hgrn.shape[3]
    rw = r_heads * kd
    dff = w_gate.shape[2]
    page_size = cache_k.shape[2]
    assert page_size >= MAX_DISTANCE and tp % TQ == 0
    l = 0
    lam_init = 0.8 - 0.6 * math.exp(-0.3 * l)
    out_scale = 1.0 - lam_init

    biasp, biass, lam, lb = _tables(rel_bias, lambda_qk[l], hgrn_lb, n_heads, ts, page_size, lam_init)

    w_in_bf = w_in[l].astype(BF16)
    wo_bf = w_out[l].astype(BF16)
    wg_bf = w_gate[l].astype(BF16)
    wu_bf = w_up[l].astype(BF16)
    wd_bf = w_down[l].astype(BF16)
    cw = conv_w[l]
    cb = conv_b[l].reshape(1, dff)
    g1 = norm1_g[l].reshape(1, d)
    g2 = norm2_g[l].reshape(1, d)
    qg = jnp.tile(qk_norm_g[l, 0].reshape(1, 2 * hd), (1, n_heads))
    kg = jnp.tile(qk_norm_g[l, 1].reshape(1, 2 * hd), (1, n_heads))
    seg = np.arange(aw) // hd
    bd = jnp.asarray((seg[:, None] == seg[None, :]).astype(np.float32) / hd, BF16)
    sg = subln_g[l].reshape(1, vd)
    gn = hgrn_onorm_g[l].reshape(1, kd)
    q_scale = hd ** -0.5 * LOG2E

    xp2 = x_prompt.reshape(bp * tp, d)
    qbt, kft, vf, kb, vbt, rp = _proj(xp2, g1, w_in_bf, qg, kg, bd, aw, rw, vd, q_scale, True, BF16, BF16, 512, bp)
    xs_pad = jnp.pad(x_sample, ((0, 0), (0, SAMPLE_PAD - ts), (0, 0))).reshape(bs * SAMPLE_PAD, d)
    qs, ksf, vsf, rs = _proj(xs_pad, g1, w_in_bf, qg, kg, bd, aw, rw, vd, q_scale, False, F32, F32, 512, bs)

    o_att = _attn(lam, qbt, kb.reshape(bp, tp, aw), vbt, biasp, sg, n_heads, hd, out_scale)
    ck = jnp.transpose(cache_k[l], (0, 2, 3, 4, 1)).reshape(cache_k.shape[1], aw, page_size)
    cv = cache_v[l].reshape(cache_v.shape[1], page_size * n_heads, vd)
    o_rec, s_p, o_att_s, o_rec_s, s_s = _mix(page_table, lam, rp.reshape(bp, tp, 4 * rw), lb, gn, qs, ksf, vsf, biass,
                                             sg, rs, state_hgrn[l], ck, cv, n_heads, hd, r_heads, kd, out_scale, ts)

    yp, a_tail = _ffn(xp2, o_att.reshape(bp * tp, aw), o_rec.reshape(bp * tp, rw), wo_bf, g2, wg_bf, wu_bf, cw, cb,
                      wd_bf, None, 512, bp, None)
    k_prompt = kft.reshape(bp, n_heads, 2, hd, tp).transpose(0, 4, 1, 2, 3)[None]
    v_prompt = vf.reshape(1, bp, tp, n_heads, vd)
    conv_prompt = a_tail[:, SUBLANES - (CONV_W - 1):][None]
    unpad = lambda z: z.reshape(bs, SAMPLE_PAD, -1)[:, :ts]
    flat = lambda z: unpad(z).reshape(bs * ts, -1)
    prev = jnp.pad(state_conv[l], ((0, 0), (0, ts - (CONV_W - 1)), (0, 0))).reshape(bs * ts, dff)
    ys, a_s = _ffn(x_sample.reshape(bs * ts, d), flat(o_att_s), flat(o_rec_s), wo_bf, g2, wg_bf, wu_bf, cw, cb,
                   wd_bf, prev, 256, 1, ts)
    y_sample = ys.reshape(bs, ts, d)
    k_sample = unpad(ksf).reshape(1, bs, ts, n_heads, 2, hd)
    v_sample = unpad(vsf).reshape(1, bs, ts, n_heads, vd)
    conv_sample = a_s.reshape(bs, ts, dff)[:, ts - (CONV_W - 1):ts][None]

    return (yp.reshape(bp, tp, d), y_sample, k_prompt, v_prompt, s_p[None], conv_prompt,
            k_sample, v_sample, s_s[None], conv_sample)
```

```python
import functools
import math

import numpy as np
import jax
import jax.numpy as jnp
from jax import lax
from jax.experimental import pallas as pl
from jax.experimental.pallas import tpu as pltpu

F32 = jnp.float32
BF16 = jnp.bfloat16

EPS = 1e-6
N_BUCKETS = 32
MAX_DISTANCE = 128
CONV_W = 3
NEG = -1e30
LOG2E = math.log2(math.e)
LANES = 128
SUBLANES = 8

TQ = 512
HG_CHUNK = 128
MIX_SEQS = 2
SAMPLE_PAD = 8
VMEM_LIMIT = 56 * 1024 * 1024


def _const_spec(shape, index_map):
    return pl.BlockSpec(shape, index_map, pipeline_mode=pl.Buffered(1))


def _nt(a, b):
    return lax.dot_general(a, b, (((1,), (1,)), ((), ())), preferred_element_type=F32)


def _tn(a, b):
    return lax.dot_general(a, b, (((0,), (0,)), ((), ())), preferred_element_type=F32)


def _nn(a, b):
    return jnp.dot(a, b, preferred_element_type=F32)


def _rel_bucket_np(n):
    n = np.maximum(n, 0)
    max_exact = N_BUCKETS // 2
    nf = np.maximum(n, 1).astype(np.float32)
    large = max_exact + (np.log(nf / np.float32(max_exact)) / np.float32(math.log(MAX_DISTANCE / max_exact))
                         * np.float32(N_BUCKETS - max_exact)).astype(np.int32)
    large = np.minimum(large, N_BUCKETS - 1)
    return np.where(n < max_exact, n, large).astype(np.int32)


def _tables_kernel(relb_ref, lq_ref, hlb_ref, bkp_ref, bks_ref,
                   biasp_ref, biass_ref, lam_ref, lb_ref, *, n_heads, lam_init, near_blocks):
    for h in range(n_heads):
        far = relb_ref[N_BUCKETS - 1, h]
        for kind in range(2):
            biasp_ref[h, kind] = jnp.where(bkp_ref[kind] < 0, NEG, 0.0).astype(F32)
            for (r0, c0) in near_blocks[kind]:
                bk = bkp_ref[kind, r0:r0 + LANES, c0:c0 + LANES]
                acc = jnp.where(bk < 0, NEG, 0.0).astype(F32)
                for b in range(N_BUCKETS - 1):
                    acc = jnp.where(bk == b, (relb_ref[b, h] - far) * LOG2E, acc)
                biasp_ref[h, kind, r0:r0 + LANES, c0:c0 + LANES] = acc
            rows = bks_ref.shape[1] // n_heads
            bk = bks_ref[kind, h * rows:(h + 1) * rows, :]
            acc = jnp.where(bk < 0, NEG, 0.0).astype(F32)
            for b in range(N_BUCKETS - 1):
                acc = jnp.where(bk == b, (relb_ref[b, h] - far) * LOG2E, acc)
            biass_ref[kind, h * rows:(h + 1) * rows, :] = acc
    lq = lq_ref[...].astype(F32)
    s1 = jnp.sum(lq[0:1] * lq[1:2], axis=1, keepdims=True)
    s2 = jnp.sum(lq[2:3] * lq[3:4], axis=1, keepdims=True)
    lam = jnp.exp(s1) - jnp.exp(s2) + lam_init
    lam_ref[...] = jnp.broadcast_to(lam, lam_ref.shape)
    hl = hlb_ref[...].astype(F32)
    mx = jnp.max(hl, axis=0, keepdims=True)
    e = jnp.exp(hl - mx)
    lb_ref[...] = e[0:1] / jnp.sum(e, axis=0, keepdims=True)


def _tables(rel_bias, lambda_qk_l, hgrn_lb, n_heads, n_tok, page_size, lam_init):
    c = np.arange(TQ)[:, None]
    r = np.arange(TQ)[None, :]
    diag = np.where(c <= r, _rel_bucket_np(r - c), -1)
    prev = _rel_bucket_np(TQ + r - c)
    bkp_np = np.stack([diag, prev]).astype(np.int32)
    bkp = jnp.asarray(bkp_np)
    near_blocks = tuple(
        tuple((r0, c0) for r0 in range(0, TQ, LANES) for c0 in range(0, TQ, LANES)
              if np.any((bkp_np[kind, r0:r0 + LANES, c0:c0 + LANES] >= 0)
                        & (bkp_np[kind, r0:r0 + LANES, c0:c0 + LANES] < N_BUCKETS - 1)))
        for kind in range(2))
    t = np.tile(np.arange(SAMPLE_PAD), 2 * n_heads)[:, None]
    cc = np.arange(page_size)[None, :]
    last_page = _rel_bucket_np(page_size + t - cc)
    new_page = np.where(cc <= np.minimum(t, n_tok - 1), _rel_bucket_np(t - cc), -1)
    bks = jnp.asarray(np.stack([last_page, new_page]).astype(np.int32))
    rows_s = 2 * n_heads * SAMPLE_PAD
    kern = functools.partial(_tables_kernel, n_heads=n_heads, lam_init=lam_init, near_blocks=near_blocks)
    return pl.pallas_call(
        kern,
        out_shape=(jax.ShapeDtypeStruct((n_heads, 2, TQ, TQ), F32),
                   jax.ShapeDtypeStruct((2, rows_s, page_size), F32),
                   jax.ShapeDtypeStruct((SUBLANES, LANES), F32),
                   jax.ShapeDtypeStruct((1, hgrn_lb.shape[1]), F32)),
        in_specs=[pl.BlockSpec(memory_space=pltpu.SMEM),
                  pl.BlockSpec(memory_space=pltpu.VMEM),
                  pl.BlockSpec(memory_space=pltpu.VMEM),
                  pl.BlockSpec(memory_space=pltpu.VMEM),
                  pl.BlockSpec(memory_space=pltpu.VMEM)],
        name="tables",
    )(rel_bias, lambda_qk_l, hgrn_lb, bkp, bks)


def _proj_kernel(x_ref, g1_ref, w_ref, qg_ref, kg_ref, bd_ref, *out_refs, aw, rw, vd, q_scale, emit_bf16):
    if emit_bf16:
        q_ref, kf_ref, vf_ref, kb_ref, vb_ref, r_ref = out_refs
    else:
        q_ref, kf_ref, vf_ref, r_ref = out_refs
    x = x_ref[...].astype(F32)
    ms = jnp.mean(x * x, axis=-1, keepdims=True)
    h = (x * lax.rsqrt(ms + EPS) * g1_ref[...]).astype(BF16)

    def seg_norm(z, g):
        msq = _nn((z * z).astype(BF16), bd_ref[...])
        return z * lax.rsqrt(msq + EPS) * g

    zq = _nn(h, w_ref[:, 0:aw])
    qn = seg_norm(zq, qg_ref[...]) * q_scale
    if emit_bf16:
        q_ref[0] = qn.T.astype(q_ref.dtype)
    else:
        q_ref[...] = qn.astype(q_ref.dtype)
    zk = _nn(h, w_ref[:, aw:2 * aw])
    kn = seg_norm(zk, kg_ref[...])
    zv = _nn(h, w_ref[:, 2 * aw:3 * aw])
    if emit_bf16:
        n_heads = aw // vd
        kf_ref[0] = kn.T
        kb_ref[...] = kn.astype(BF16)
        vb_ref[0] = zv.T.astype(BF16)
        for hh in range(n_heads):
            vf_ref[pl.ds(hh, zv.shape[0], stride=n_heads), :] = zv[:, hh * vd:(hh + 1) * vd]
    else:
        kf_ref[...] = kn
        vf_ref[...] = zv
    for g in range(4):
        z = _nn(h, w_ref[:, 3 * aw + g * rw:3 * aw + (g + 1) * rw])
        r_ref[:, g * rw:(g + 1) * rw] = z.astype(r_ref.dtype)


def _proj(x2, g1, w_bf, qg, kg, bd, aw, rw, vd, q_scale, emit_bf16, q_dtype, r_dtype, tm, n_batch):
    n, d = x2.shape
    kern = functools.partial(_proj_kernel, aw=aw, rw=rw, vd=vd, q_scale=q_scale, emit_bf16=emit_bf16)
    row = lambda i: (i, 0)
    const = lambda i: (0, 0)
    if emit_bf16:
        t = n // n_batch
        per_b = t // tm
        n_heads = aw // vd
        xposed = lambda i: (i // per_b, 0, i % per_b)
        out_shape = [jax.ShapeDtypeStruct((n_batch, aw, t), q_dtype),
                     jax.ShapeDtypeStruct((n_batch, aw, t), F32),
                     jax.ShapeDtypeStruct((n * n_heads, vd), F32),
                     jax.ShapeDtypeStruct((n, aw), BF16),
                     jax.ShapeDtypeStruct((n_batch, aw, t), BF16)]
        out_specs = [pl.BlockSpec((1, aw, tm), xposed),
                     pl.BlockSpec((1, aw, tm), xposed),
                     pl.BlockSpec((tm * n_heads, vd), row),
                     pl.BlockSpec((tm, aw), row),
                     pl.BlockSpec((1, aw, tm), xposed)]
    else:
        out_shape = [jax.ShapeDtypeStruct((n, aw), q_dtype),
                     jax.ShapeDtypeStruct((n, aw), F32),
                     jax.ShapeDtypeStruct((n, aw), F32)]
        out_specs = [pl.BlockSpec((tm, aw), row)] * 3
    out_shape.append(jax.ShapeDtypeStruct((n, 4 * rw), r_dtype))
    out_specs.append(pl.BlockSpec((tm, 4 * rw), row))
    return pl.pallas_call(
        kern,
        grid=(n // tm,),
        in_specs=[pl.BlockSpec((tm, d), row),
                  pl.BlockSpec((1, d), const),
                  _const_spec(w_bf.shape, const),
                  pl.BlockSpec((1, aw), const),
                  pl.BlockSpec((1, aw), const),
                  pl.BlockSpec((aw, aw), const)],
        out_specs=out_specs,
        out_shape=out_shape,
        compiler_params=pltpu.CompilerParams(dimension_semantics=("arbitrary",),
                                             vmem_limit_bytes=VMEM_LIMIT),
        name="proj_bf16" if emit_bf16 else "proj_f32",
    )(x2, g1, w_bf, qg, kg, bd)


def _attn_kernel(lam_ref, q_ref, k_ref, vt_ref, bias_ref, sg_ref, o_ref,
                 qst_sc, sa_sc, ma_sc, sb_sc, mb_sc, m_sc, l_sc, acc_sc, *, hd, out_scale):
    i = pl.program_id(2)
    half = TQ // 2

    qt = q_ref[0].astype(F32)
    row = lax.broadcasted_iota(jnp.int32, qt.shape, 0)
    q0 = jnp.where(row < hd, qt, 0.0)
    q1 = jnp.where(row >= hd, qt, 0.0)
    qst_sc[...] = jnp.concatenate([q0[:, 0:half], q1[:, 0:half], q0[:, half:TQ], q1[:, half:TQ]],
                                  axis=1).astype(BF16)
    m_sc[...] = jnp.full(m_sc.shape, NEG, F32)
    l_sc[...] = jnp.zeros(l_sc.shape, F32)
    acc_sc[...] = jnp.zeros(acc_sc.shape, F32)

    bufs = ((sa_sc, ma_sc), (sb_sc, mb_sc))

    def issue(buf, g, with_max):
        start = pl.multiple_of(g * TQ, TQ)
        s = _nn(k_ref[0, pl.ds(start, TQ), :], qst_sc[...])
        buf[0][...] = s
        if with_max:
            buf[1][...] = jnp.max(s, axis=0, keepdims=True)

    def issue_diag(buf, g):
        start = pl.multiple_of(g * TQ, TQ)
        buf[0][0:half, :] = _nn(k_ref[0, pl.ds(start, half), :], qst_sc[...])
        buf[0][half:TQ, TQ:2 * TQ] = _nn(k_ref[0, pl.ds(pl.multiple_of(start + half, half), half), :],
                                          qst_sc[:, TQ:2 * TQ])

    def update(s, s_max, lane0, n_lanes, key0, n_keys):
        lanes = slice(lane0, lane0 + n_lanes)
        m_old = m_sc[:, lanes]
        m_new = jnp.maximum(m_old, s_max)
        alpha = jnp.exp2(m_old - m_new)
        p = jnp.exp2(s - m_new)
        l_sc[:, lanes] = alpha * l_sc[:, lanes] + jnp.sum(p, axis=0, keepdims=True)
        acc_sc[:, lanes] = alpha * acc_sc[:, lanes] + _nn(vt_ref[0, :, pl.ds(key0, n_keys)], p.astype(BF16))
        m_sc[:, lanes] = m_new

    def both_maps(s, bias):
        w = bias.shape[1]
        return jnp.concatenate([s[:, 0:w] + bias, s[:, w:2 * w] + bias], axis=1)

    def consume(buf, g, kind):
        start = pl.multiple_of(g * TQ, TQ)
        if kind == 0:
            s_lo = both_maps(buf[0][0:half, 0:TQ], bias_ref[0, 0, 0:half, 0:half])
            update(s_lo, jnp.max(s_lo, axis=0, keepdims=True), 0, TQ, start, half)
            s_hi = both_maps(buf[0][:, TQ:2 * TQ], bias_ref[0, 0, :, half:TQ])
            update(s_hi, jnp.max(s_hi, axis=0, keepdims=True), TQ, TQ, start, TQ)
            return
        s = buf[0][...]
        s_max = buf[1][...]
        if kind == 1:
            n = MAX_DISTANCE
            bias = bias_ref[0, 1, TQ - n:TQ, 0:n]
            near = s[TQ - n:TQ]
            near = jnp.concatenate([near[:, 0:n] + bias, near[:, n:half], near[:, half:half + n] + bias,
                                    near[:, half + n:2 * TQ]], axis=1)
            s = jnp.concatenate([s[0:TQ - n], near], axis=0)
            s_max = jnp.concatenate([jnp.max(s[:, 0:n], axis=0, keepdims=True), s_max[:, n:half],
                                     jnp.max(s[:, half:half + n], axis=0, keepdims=True), s_max[:, half + n:2 * TQ]],
                                    axis=1)
        update(s, s_max, 0, 2 * TQ, start, TQ)

    n_pure = jnp.maximum(i - 1, 0)
    issue(bufs[0], 0, True)

    def pair(jj, carry):
        g = 2 * jj
        issue(bufs[1], g + 1, True)
        consume(bufs[0], g, None)
        issue(bufs[0], g + 2, True)
        consume(bufs[1], g + 1, None)
        return carry

    lax.fori_loop(0, n_pure // 2, pair, 0)
    odd = (n_pure & 1) == 1

    def finish(cur, other):
        @pl.when(i > 0)
        def _():
            issue_diag(other, i)
            consume(cur, i - 1, 1)
            consume(other, i, 0)

        @pl.when(i == 0)
        def _():
            consume(cur, 0, 0)

    @pl.when(odd)
    def _():
        issue(bufs[1], n_pure, True)
        consume(bufs[0], n_pure - 1, None)
        finish(bufs[1], bufs[0])

    @pl.when(jnp.logical_not(odd))
    def _():
        finish(bufs[0], bufs[1])

    a = acc_sc[...] * (1.0 / l_sc[...])
    map0 = jnp.concatenate([a[:, 0:half], a[:, TQ:TQ + half]], axis=1)
    map1 = jnp.concatenate([a[:, half:TQ], a[:, TQ + half:2 * TQ]], axis=1)
    ot = map0 - lam_ref[0, 0] * map1
    ms = jnp.mean(ot * ot, axis=0, keepdims=True)
    o = (ot * lax.rsqrt(ms + EPS)).T
    o_ref[0] = (o * (sg_ref[...] * out_scale)).astype(o_ref.dtype)


def _attn(lam, qt, k, vt, bias, sg, n_heads, hd, out_scale):
    b, t, aw = k.shape
    vd = aw // n_heads
    assert t % TQ == 0
    kern = functools.partial(_attn_kernel, hd=hd, out_scale=out_scale)
    return pl.pallas_call(
        kern,
        grid=(b, n_heads, t // TQ),
        in_specs=[pl.BlockSpec(memory_space=pltpu.SMEM),
                  pl.BlockSpec((1, vd, TQ), lambda bi, h, i: (bi, h, i)),
                  pl.BlockSpec((1, t, vd), lambda bi, h, i: (bi, 0, h)),
                  pl.BlockSpec((1, vd, t), lambda bi, h, i: (bi, h, 0)),
                  pl.BlockSpec((1, 2, TQ, TQ), lambda bi, h, i: (h, 0, 0, 0)),
                  pl.BlockSpec((1, vd), lambda bi, h, i: (0, 0))],
        out_specs=pl.BlockSpec((1, TQ, vd), lambda bi, h, i: (bi, i, h)),
        out_shape=jax.ShapeDtypeStruct((b, t, aw), BF16),
        scratch_shapes=[pltpu.VMEM((vd, 2 * TQ), BF16),
                        pltpu.VMEM((TQ, 2 * TQ), F32),
                        pltpu.VMEM((1, 2 * TQ), F32),
                        pltpu.VMEM((TQ, 2 * TQ), F32),
                        pltpu.VMEM((1, 2 * TQ), F32),
                        pltpu.VMEM((1, 2 * TQ), F32),
                        pltpu.VMEM((1, 2 * TQ), F32),
                        pltpu.VMEM((vd, 2 * TQ), F32)],
        compiler_params=pltpu.CompilerParams(dimension_semantics=("arbitrary",) * 3,
                                             vmem_limit_bytes=VMEM_LIMIT),
        name="attn_prompt",
    )(lam, qt, k, vt, bias, sg)


def _decode_seq(lam, q8, kn8, vn8, bias_ref, sg, k_refs, v_refs, o_ref, rows_out, n_heads, hd, out_scale):
    n_pages = len(k_refs)
    aw = q8.shape[1]
    vd = aw // n_heads
    ps = k_refs[0].shape[1]
    lane = lax.broadcasted_iota(jnp.int32, q8.shape, 1)
    pieces = [jnp.where((lane >= hm * hd) & (lane < (hm + 1) * hd), q8, 0.0) for hm in range(2 * n_heads)]
    qbd = jnp.concatenate(pieces, axis=0).astype(BF16)
    s_list = []
    for p in range(n_pages):
        s = _nn(qbd, k_refs[p][...].astype(BF16))
        if p == n_pages - 1:
            s = s + bias_ref[0]
        s_list.append(s)
    pad = jnp.zeros((ps - SAMPLE_PAD, aw), F32)
    k_new = jnp.concatenate([kn8, pad], axis=0).astype(BF16)
    v_new = jnp.concatenate([vn8, pad], axis=0).astype(BF16)
    s_list.append(_nt(qbd, k_new) + bias_ref[1])
    s_all = jnp.concatenate(s_list, axis=1)
    m = jnp.max(s_all, axis=1, keepdims=True)
    e = jnp.exp2(s_all - m)
    inv = 1.0 / jnp.sum(e, axis=1, keepdims=True)
    eb = e.astype(BF16)
    rows = 2 * SAMPLE_PAD
    for h in range(n_heads):
        eh = eb[h * rows:(h + 1) * rows]
        acc = _nn(eh[:, n_pages * ps:(n_pages + 1) * ps], v_new[:, h * vd:(h + 1) * vd])
        for p in range(n_pages):
            vh = v_refs[p][pl.ds(h, ps, stride=n_heads), :].astype(BF16)
            acc = acc + _nn(eh[:, p * ps:(p + 1) * ps], vh)
        acc = acc * inv[h * rows:(h + 1) * rows]
        o = acc[0:SAMPLE_PAD] - lam * acc[SAMPLE_PAD:rows]
        ms = jnp.mean(o * o, axis=-1, keepdims=True)
        o_ref[rows_out, h * vd:(h + 1) * vd] = (o * lax.rsqrt(ms + EPS) * (sg * out_scale)).astype(o_ref.dtype)


def _mid_reference(b, level):
    c, w = b.shape
    p = 1 << level
    half = p // 2
    if p >= SUBLANES:
        bp = b.reshape(c // p, p, w)
        return jnp.broadcast_to(bp[:, half - 1:half, :], (c // p, p, w)).reshape(c, w)
    b8 = b.reshape(c // SUBLANES, SUBLANES, w)
    rig = lax.broadcasted_iota(jnp.int32, b8.shape, 1)
    out = None
    for start in range(SUBLANES - p, -1, -p):
        row = jnp.broadcast_to(b8[:, start + half - 1:start + half, :], b8.shape)
        out = row if out is None else jnp.where(rig < start + p, row, out)
    return out.reshape(c, w)


def _upper_lower(up, low, level):
    c, w = up.shape
    half = (1 << level) // 2
    if half >= SUBLANES:
        shape = (c // (2 * half), 2, half, w)
        return jnp.concatenate([low.reshape(shape)[:, 0:1], up.reshape(shape)[:, 1:2]], axis=1).reshape(c, w)
    rowid = lax.broadcasted_iota(jnp.int32, up.shape, 0)
    return jnp.where((rowid & half) != 0, up, low)


def _hgrn_chunk(r_ref, r0, lb_ref, gn_ref, tri, lev, o_ref, st_sc, n_heads, kd):
    c = HG_CHUNK
    rw = n_heads * kd
    n_levels = c.bit_length() - 1
    head = lambda z, h: z[:, h * kd:(h + 1) * kd]
    q = r_ref[0, r0:r0 + c, 0:rw].astype(F32)
    rf = r_ref[0, r0:r0 + c, rw:2 * rw].astype(F32)
    v = r_ref[0, r0:r0 + c, 2 * rw:3 * rw].astype(BF16)
    rg = r_ref[0, r0:r0 + c, 3 * rw:4 * rw].astype(F32)
    lb = lb_ref[...]
    f = lb + (1.0 - lb) * jax.nn.sigmoid(rf)
    logf = jnp.log(f) * LOG2E
    kk = 1.0 - f
    hi = logf.astype(BF16)
    lo = (logf - hi.astype(F32)).astype(BF16)
    b = _nn(tri, hi) + _nn(tri, lo)
    qb = q.astype(BF16)
    kb = kk.astype(BF16)
    diag = lev == 0
    a = [jnp.where(diag, _nt(head(qb, h), head(kb, h)), 0.0) for h in range(n_heads)]
    for level in range(1, n_levels + 1):
        e = jnp.exp2(-jnp.abs(b - _mid_reference(b, level)))
        z = (_upper_lower(q, kk, level) * e).astype(BF16)
        here = lev == level
        a = [jnp.where(here, _nt(head(z, h), head(z, h)), a[h]) for h in range(n_heads)]
    b_last = b[c - 1:c, :]
    q_hat = (q * jnp.exp2(b)).astype(BF16)
    k_hat = (kk * jnp.exp2(b_last - b)).astype(BF16)
    decay = jnp.exp2(b_last)
    outs = []
    for h in range(n_heads):
        st = st_sc[h]
        o = _nn(a[h].astype(BF16), head(v, h)) + _nt(head(q_hat, h), st.astype(BF16))
        st_sc[h] = head(decay, h) * st + _tn(head(v, h), head(k_hat, h))
        ms = jnp.mean(o * o, axis=-1, keepdims=True)
        outs.append(o * lax.rsqrt(ms + EPS) * gn_ref[...])
    on = jnp.concatenate(outs, axis=1) * (rg * jax.nn.sigmoid(rg))
    o_ref[0, r0:r0 + c, :] = on.astype(o_ref.dtype)


def _mix_kernel(pt_ref, lam_ref, r_ref, lb_ref, gn_ref, tri_ref, lev_ref, q_ref, kn_ref, vn_ref, bias_ref, sg_ref,
                rs_ref, s_ref, ck_hbm, cv_hbm, o_rec_ref, sfin_ref, o_dec_ref, o_rs_ref, so_ref,
                st_sc, kbuf, vbuf, ksem, vsem,
                *, n_pages, n_heads, hd, r_heads, kd, out_scale, steps_per_batch, n_tok):
    step = pl.program_id(0)
    slot = lax.rem(step, 2)
    t = lax.rem(step, steps_per_batch)

    def page_copies(at_step, at_slot):
        copies = []
        for j in range(MIX_SEQS):
            for p in range(n_pages):
                page = pt_ref[at_step * MIX_SEQS + j, p]
                i = j * n_pages + p
                copies.append(pltpu.make_async_copy(ck_hbm.at[page], kbuf.at[at_slot, i], ksem.at[at_slot]))
                copies.append(pltpu.make_async_copy(cv_hbm.at[page], vbuf.at[at_slot, i], vsem.at[at_slot]))
        return copies

    @pl.when(step == 0)
    def _():
        for cp in page_copies(0, 0):
            cp.start()

    @pl.when(step + 1 < pl.num_programs(0))
    def _():
        for cp in page_copies(step + 1, 1 - slot):
            cp.start()

    @pl.when(t == 0)
    def _():
        st_sc[...] = jnp.zeros(st_sc.shape, F32)

    for cp in page_copies(step, slot):
        cp.wait()

    lev = lev_ref[...]
    tri = tri_ref[...]
    lam = lam_ref[0, 0]
    for c in range(MIX_SEQS):
        _hgrn_chunk(r_ref, c * HG_CHUNK, lb_ref, gn_ref, tri, lev, o_rec_ref, st_sc, r_heads, kd)
        rows = slice(c * SAMPLE_PAD, (c + 1) * SAMPLE_PAD)
        k_refs = [kbuf.at[slot, c * n_pages + p] for p in range(n_pages)]
        v_refs = [vbuf.at[slot, c * n_pages + p] for p in range(n_pages)]
        _decode_seq(lam, q_ref[rows, :].astype(F32), kn_ref[rows, :].astype(F32), vn_ref[rows, :].astype(F32),
                    bias_ref, sg_ref[...], k_refs, v_refs, o_dec_ref, rows, n_heads, hd, out_scale)
    _hgrn_new_tokens(rs_ref, lb_ref, gn_ref, s_ref, o_rs_ref, so_ref, r_heads, kd, n_tok, MIX_SEQS)

    @pl.when(t == steps_per_batch - 1)
    def _():
        for h in range(r_heads):
            sfin_ref[0, h] = st_sc[h].T


def _mix(page_table, lam, r, lb, gn, q, kn, vn, bias, sg, rs, state, ck, cv, n_heads, hd, r_heads, kd, out_scale,
         n_tok):
    n_seq, n_pages = page_table.shape
    b, t, _ = r.shape
    _, aw, ps = ck.shape
    vd = aw // n_heads
    c = HG_CHUNK
    tokens = MIX_SEQS * c
    steps_per_batch = t // tokens
    n_steps = b * steps_per_batch
    assert t % tokens == 0 and n_seq == n_steps * MIX_SEQS
    idx = np.arange(c)
    x = idx[:, None] ^ idx[None, :]
    lev = np.where(idx[:, None] > idx[None, :], np.floor(np.log2(np.maximum(x, 1))).astype(np.int32) + 1, -1)
    lev = np.where(idx[:, None] == idx[None, :], 0, lev).astype(np.int32)
    tri = (idx[:, None] >= idx[None, :]).astype(np.float32)
    kern = functools.partial(_mix_kernel, n_pages=n_pages, n_heads=n_heads, hd=hd, r_heads=r_heads, kd=kd,
                             out_scale=out_scale, steps_per_batch=steps_per_batch, n_tok=n_tok)
    c2 = lambda s, pt: (0, 0)
    tok = lambda s, pt: (s // steps_per_batch, s % steps_per_batch, 0)
    seq_rows = lambda s, pt: (s, 0)
    seq_state = lambda s, pt: (s, 0, 0, 0)
    n_k = MIX_SEQS * n_pages
    rows = MIX_SEQS * SAMPLE_PAD
    grid_spec = pltpu.PrefetchScalarGridSpec(
        num_scalar_prefetch=1,
        grid=(n_steps,),
        in_specs=[pl.BlockSpec(memory_space=pltpu.SMEM),
                  pl.BlockSpec((1, tokens, r.shape[2]), tok),
                  pl.BlockSpec((1, r_heads * kd), c2),
                  pl.BlockSpec((1, kd), c2),
                  pl.BlockSpec((c, c), c2),
                  pl.BlockSpec((c, c), c2),
                  pl.BlockSpec((rows, aw), seq_rows),
                  pl.BlockSpec((rows, aw), seq_rows),
                  pl.BlockSpec((rows, aw), seq_rows),
                  pl.BlockSpec(bias.shape, lambda s, pt: (0, 0, 0)),
                  pl.BlockSpec((1, vd), c2),
                  pl.BlockSpec((rows, rs.shape[1]), seq_rows),
                  pl.BlockSpec((MIX_SEQS, r_heads, kd, kd), seq_state),
                  pl.BlockSpec(memory_space=pl.ANY),
                  pl.BlockSpec(memory_space=pl.ANY)],
        out_specs=[pl.BlockSpec((1, tokens, r_heads * kd), tok),
                   pl.BlockSpec((1, r_heads, kd, kd), lambda s, pt: (s // steps_per_batch, 0, 0, 0)),
                   pl.BlockSpec((rows, aw), seq_rows),
                   pl.BlockSpec((rows, r_heads * kd), seq_rows),
                   pl.BlockSpec((MIX_SEQS, r_heads, kd, kd), seq_state)],
        scratch_shapes=[pltpu.VMEM((r_heads, kd, kd), F32),
                        pltpu.VMEM((2, n_k) + ck.shape[1:], ck.dtype),
                        pltpu.VMEM((2, n_k) + cv.shape[1:], cv.dtype),
                        pltpu.SemaphoreType.DMA((2,)),
                        pltpu.SemaphoreType.DMA((2,))],
    )
    return pl.pallas_call(
        kern,
        grid_spec=grid_spec,
        out_shape=[jax.ShapeDtypeStruct((b, t, r_heads * kd), BF16),
                   jax.ShapeDtypeStruct((b, r_heads, kd, kd), F32),
                   jax.ShapeDtypeStruct((n_seq * SAMPLE_PAD, aw), F32),
                   jax.ShapeDtypeStruct((n_seq * SAMPLE_PAD, r_heads * kd), F32),
                   jax.ShapeDtypeStruct(state.shape, F32)],
        compiler_params=pltpu.CompilerParams(dimension_semantics=("arbitrary",),
                                             vmem_limit_bytes=VMEM_LIMIT),
        name="hgrn_attn_decode",
    )(page_table, lam, r, lb, gn, jnp.asarray(tri, BF16), jnp.asarray(lev), q, kn, vn, bias, sg, rs, state, ck, cv)


def _hgrn_new_tokens(r_ref, lb_ref, gn_ref, s_ref, o_ref, so_ref, n_heads, kd, n_tok, n_seq):
    rw = n_heads * kd
    rowid = lax.broadcasted_iota(jnp.int32, (SAMPLE_PAD, kd), 0)
    live = rowid < n_tok
    zpad = jnp.zeros((SAMPLE_PAD, kd), F32)
    pad16 = lambda z: jnp.concatenate([z, zpad], axis=0).astype(BF16)
    ones16 = jnp.ones((2 * SAMPLE_PAD, kd), BF16)
    for g in range(n_seq):
        rows = slice(g * SAMPLE_PAD, (g + 1) * SAMPLE_PAD)
        for h in range(n_heads):
            cols = slice(h * kd, (h + 1) * kd)
            q = r_ref[rows, h * kd:(h + 1) * kd]
            rf = r_ref[rows, rw + h * kd:rw + (h + 1) * kd]
            v = r_ref[rows, 2 * rw + h * kd:2 * rw + (h + 1) * kd]
            rg = r_ref[rows, 3 * rw + h * kd:3 * rw + (h + 1) * kd]
            lb = lb_ref[:, cols]
            f = lb + (1.0 - lb) * jax.nn.sigmoid(rf)
            kk = jnp.where(live, 1.0 - f, 0.0)
            b = jnp.where(live, jnp.log(f) * LOG2E, 0.0)
            for sh in (1, 2, 4):
                b = b + jnp.where(rowid >= sh, pltpu.roll(b, sh, 0), 0.0)
            s0 = s_ref[g, h]
            o = jnp.sum(q * kk, axis=1, keepdims=True) * v
            for d in range(1, n_tok):
                x = q * pltpu.roll(kk, d, 0) * jnp.exp2(b - pltpu.roll(b, d, 0))
                w = jnp.sum(jnp.where(rowid >= d, x, 0.0), axis=1, keepdims=True)
                o = o + w * pltpu.roll(v, d, 0)
            o = o + _nn(pad16(q * jnp.exp2(b)), s0.astype(BF16))[0:SAMPLE_PAD]
            b_last = b[n_tok - 1:n_tok, :]
            decay = jnp.exp2(b_last)
            d_hi = decay.astype(BF16).astype(F32)
            d_split = jnp.where(rowid == 0, d_hi, jnp.where(rowid == 1, decay - d_hi, 0.0))
            decay_cols = _tn(pad16(d_split), ones16)
            so_ref[g, h] = decay_cols * s0 + _tn(pad16(kk * jnp.exp2(b_last - b)), pad16(v))
            ms = jnp.mean(o * o, axis=-1, keepdims=True)
            on = o * lax.rsqrt(ms + EPS) * gn_ref[...] * (rg * jax.nn.sigmoid(rg))
            o_ref[rows, cols] = on.astype(o_ref.dtype)


def _ffn_kernel(*refs, sample, aw, seq_rows):
    if sample:
        (x_ref, oa_ref, or_ref, wo_ref, g2_ref, wg_ref, wu_ref, cw_ref, cb_ref, wd_ref, prev_ref,
         y_ref, a_ref) = refs
    else:
        (x_ref, oa_ref, or_ref, wo_ref, g2_ref, wg_ref, wu_ref, cw_ref, cb_ref, wd_ref,
         y_ref, a_ref, carry_sc) = refs

        @pl.when(pl.program_id(1) == 0)
        def _():
            carry_sc[...] = jnp.zeros(carry_sc.shape, F32)

    x = x_ref[...].astype(F32)
    x1 = x + _nn(oa_ref[...].astype(BF16), wo_ref[0:aw, :]) + _nn(or_ref[...].astype(BF16), wo_ref[aw:, :])
    ms = jnp.mean(x1 * x1, axis=-1, keepdims=True)
    h2 = (x1 * lax.rsqrt(ms + EPS) * g2_ref[...]).astype(BF16)
    tm = x.shape[0]
    a = _nn(h2, wg_ref[...])
    u = _nn(h2, wu_ref[...])
    rowid = lax.broadcasted_iota(jnp.int32, a.shape, 0)
    am1 = pltpu.roll(a, 1, 0)
    am2 = pltpu.roll(a, 2, 0)
    if sample:
        rig = rowid & (seq_rows - 1)
        prev = prev_ref[...]
        am1 = jnp.where(rig == 0, pltpu.roll(prev, tm - 1, 0), am1)
        am2 = jnp.where(rig < 2, prev, am2)
        a_ref[...] = a
    else:
        last = carry_sc[...]
        am1 = jnp.where(rowid == 0, last[SUBLANES - 1:SUBLANES, :], am1)
        am2 = jnp.where(rowid == 0, last[SUBLANES - 2:SUBLANES - 1, :],
                        jnp.where(rowid == 1, last[SUBLANES - 1:SUBLANES, :], am2))
        tail = a[tm - SUBLANES:tm, :]
        carry_sc[...] = tail
        a_ref[0] = tail
    cw = cw_ref[...]
    cv = cb_ref[...] + cw[0:1, :] * am2 + cw[1:2, :] * am1 + cw[2:3, :] * a
    gate = (cv * jax.nn.sigmoid(cv) * u).astype(BF16)
    y_ref[...] = x1 + _nn(gate, wd_ref[...])


def _ffn(x2, oa, orec, wo, g2, wg, wu, cw, cb, wd, prev, tm, n_batch, seq_rows):
    n, d = x2.shape
    aw = oa.shape[1]
    dff = wg.shape[1]
    sample = prev is not None
    assert not sample or (seq_rows >= CONV_W - 1 and seq_rows & (seq_rows - 1) == 0 and tm % seq_rows == 0)
    kern = functools.partial(_ffn_kernel, sample=sample, aw=aw, seq_rows=seq_rows)
    scratch = []
    if sample:
        grid = (n // tm,)
        row = lambda i: (i, 0)
        c2 = lambda i: (0, 0)
        extra_in = [pl.BlockSpec((tm, dff), row)]
        extra_args = [prev]
        a_shape = jax.ShapeDtypeStruct((n, dff), F32)
        a_spec = pl.BlockSpec((tm, dff), row)
        sem = ("arbitrary",)
    else:
        per_b = n // n_batch // tm
        grid = (n_batch, per_b)
        row = lambda b, i: (b * per_b + i, 0)
        c2 = lambda b, i: (0, 0)
        extra_in = []
        extra_args = []
        a_shape = jax.ShapeDtypeStruct((n_batch, SUBLANES, dff), F32)
        a_spec = pl.BlockSpec((1, SUBLANES, dff), lambda b, i: (b, 0, 0))
        scratch.append(pltpu.VMEM((SUBLANES, dff), F32))
        sem = ("arbitrary", "arbitrary")
    return pl.pallas_call(
        kern,
        grid=grid,
        in_specs=[pl.BlockSpec((tm, d), row),
                  pl.BlockSpec((tm, aw), row),
                  pl.BlockSpec((tm, aw), row),
                  _const_spec(wo.shape, c2),
                  pl.BlockSpec((1, d), c2),
                  _const_spec(wg.shape, c2),
                  _const_spec(wu.shape, c2),
                  pl.BlockSpec(cw.shape, c2),
                  pl.BlockSpec(cb.shape, c2),
                  _const_spec(wd.shape, c2)] + extra_in,
        out_specs=[pl.BlockSpec((tm, d), row), a_spec],
        out_shape=[jax.ShapeDtypeStruct((n, d), F32), a_shape],
        scratch_shapes=scratch,
        compiler_params=pltpu.CompilerParams(dimension_semantics=sem, vmem_limit_bytes=VMEM_LIMIT),
        name="ffn_sample" if sample else "ffn_prompt",
    )(x2, oa, orec, wo, g2, wg, wu, cw, cb, wd, *extra_args)


def kernel(x_prompt, x_sample, cache_k, cache_v, page_table, state_hgrn, state_conv, rel_bias, norm1_g, w_in,
           qk_norm_g, lambda_qk, subln_g, hgrn_lb, hgrn_onorm_g, w_out, norm2_g, w_gate, w_up, conv_w, conv_b,
           w_down):
    bp, tp, d = x_prompt.shape
    bs, ts, _ = x_sample.shape
    depth = w_in.shape[0]
    assert depth == 1 and CONV_W - 1 <= ts <= SAMPLE_PAD and conv_w.shape[1] == CONV_W
    n_heads, _, hd = cache_k.shape[3:]
    vd = cache_v.shape[4]
    aw = n_heads * vd
    r_heads, kd = state_hgrn.shape[2], state_hgrn.shape[3]
    rw = r_heads * kd
    dff = w_gate.shape[2]
    page_size = cache_k.shape[2]
    assert page_size >= MAX_DISTANCE and tp % TQ == 0
    l = 0
    lam_init = 0.8 - 0.6 * math.exp(-0.3 * l)
    out_scale = 1.0 - lam_init

    biasp, biass, lam, lb = _tables(rel_bias, lambda_qk[l], hgrn_lb, n_heads, ts, page_size, lam_init)

    w_in_bf = w_in[l].astype(BF16)
    wo_bf = w_out[l].astype(BF16)
    wg_bf = w_gate[l].astype(BF16)
    wu_bf = w_up[l].astype(BF16)
    wd_bf = w_down[l].astype(BF16)
    cw = conv_w[l]
    cb = conv_b[l].reshape(1, dff)
    g1 = norm1_g[l].reshape(1, d)
    g2 = norm2_g[l].reshape(1, d)
    qg = jnp.tile(qk_norm_g[l, 0].reshape(1, 2 * hd), (1, n_heads))
    kg = jnp.tile(qk_norm_g[l, 1].reshape(1, 2 * hd), (1, n_heads))
    seg = np.arange(aw) // hd
    bd = jnp.asarray((seg[:, None] == seg[None, :]).astype(np.float32) / hd, BF16)
    sg = subln_g[l].reshape(1, vd)
    gn = hgrn_onorm_g[l].reshape(1, kd)
    q_scale = hd ** -0.5 * LOG2E

    xp2 = x_prompt.reshape(bp * tp, d)
    qbt, kft, vf, kb, vbt, rp = _proj(xp2, g1, w_in_bf, qg, kg, bd, aw, rw, vd, q_scale, True, BF16, BF16, 512, bp)
    xs_pad = jnp.pad(x_sample, ((0, 0), (0, SAMPLE_PAD - ts), (0, 0))).reshape(bs * SAMPLE_PAD, d)
    qs, ksf, vsf, rs = _proj(xs_pad, g1, w_in_bf, qg, kg, bd, aw, rw, vd, q_scale, False, F32, F32, 512, bs)

    o_att = _attn(lam, qbt, kb.reshape(bp, tp, aw), vbt, biasp, sg, n_heads, hd, out_scale)
    ck = jnp.transpose(cache_k[l], (0, 2, 3, 4, 1)).reshape(cache_k.shape[1], aw, page_size)
    cv = cache_v[l].reshape(cache_v.shape[1], page_size * n_heads, vd)
    o_rec, s_p, o_att_s, o_rec_s, s_s = _mix(page_table, lam, rp.reshape(bp, tp, 4 * rw), lb, gn, qs, ksf, vsf, biass,
                                             sg, rs, state_hgrn[l], ck, cv, n_heads, hd, r_heads, kd, out_scale, ts)

    yp, a_tail = _ffn(xp2, o_att.reshape(bp * tp, aw), o_rec.reshape(bp * tp, rw), wo_bf, g2, wg_bf, wu_bf, cw, cb,
                      wd_bf, None, 512, bp, None)
    k_prompt = kft.reshape(bp, n_heads, 2, hd, tp).transpose(0, 4, 1, 2, 3)[None]
    v_prompt = vf.reshape(1, bp, tp, n_heads, vd)
    conv_prompt = a_tail[:, SUBLANES - (CONV_W - 1):][None]
    unpad = lambda z: z.reshape(bs, SAMPLE_PAD, -1)[:, :ts]
    flat = lambda z: unpad(z).reshape(bs * ts, -1)
    prev = jnp.pad(state_conv[l], ((0, 0), (0, ts - (CONV_W - 1)), (0, 0))).reshape(bs * ts, dff)
    ys, a_s = _ffn(x_sample.reshape(bs * ts, d), flat(o_att_s), flat(o_rec_s), wo_bf, g2, wg_bf, wu_bf, cw, cb,
                   wd_bf, prev, 256, 1, ts)
    y_sample = ys.reshape(bs, ts, d)
    k_sample = unpad(ksf).reshape(1, bs, ts, n_heads, 2, hd)
    v_sample = unpad(vsf).reshape(1, bs, ts, n_heads, vd)
    conv_sample = a_s.reshape(bs, ts, dff)[:, ts - (CONV_W - 1):ts][None]

    return (yp.reshape(bp, tp, d), y_sample, k_prompt, v_prompt, s_p[None], conv_prompt,
            k_sample, v_sample, s_s[None], conv_sample)
```

```python
import functools
import math

import numpy as np
import jax
import jax.numpy as jnp
from jax import lax
from jax.experimental import pallas as pl
from jax.experimental.pallas import tpu as pltpu

F32 = jnp.float32
BF16 = jnp.bfloat16

EPS = 1e-6
N_BUCKETS = 32
MAX_DISTANCE = 128
CONV_W = 3
NEG = -1e30
LOG2E = math.log2(math.e)
LANES = 128
SUBLANES = 8

TQ = 512
ONES_ROWS = 16
HG_CHUNK = 128
MIX_SEQS = 2
SAMPLE_PAD = 8
VMEM_LIMIT = 56 * 1024 * 1024


def _const_spec(shape, index_map):
    return pl.BlockSpec(shape, index_map, pipeline_mode=pl.Buffered(1))


def _nt(a, b):
    return lax.dot_general(a, b, (((1,), (1,)), ((), ())), preferred_element_type=F32)


def _tn(a, b):
    return lax.dot_general(a, b, (((0,), (0,)), ((), ())), preferred_element_type=F32)


def _nn(a, b):
    return jnp.dot(a, b, preferred_element_type=F32)


def _rel_bucket_np(n):
    n = np.maximum(n, 0)
    max_exact = N_BUCKETS // 2
    nf = np.maximum(n, 1).astype(np.float32)
    large = max_exact + (np.log(nf / np.float32(max_exact)) / np.float32(math.log(MAX_DISTANCE / max_exact))
                         * np.float32(N_BUCKETS - max_exact)).astype(np.int32)
    large = np.minimum(large, N_BUCKETS - 1)
    return np.where(n < max_exact, n, large).astype(np.int32)


def _tables_kernel(relb_ref, lq_ref, hlb_ref, bkp_ref, bks_ref,
                   biasp_ref, biass_ref, lam_ref, lb_ref, *, n_heads, lam_init, near_blocks):
    for h in range(n_heads):
        far = relb_ref[N_BUCKETS - 1, h]
        for kind in range(2):
            biasp_ref[h, kind] = jnp.where(bkp_ref[kind] < 0, NEG, 0.0).astype(F32)
            for (r0, c0) in near_blocks[kind]:
                bk = bkp_ref[kind, r0:r0 + LANES, c0:c0 + LANES]
                acc = jnp.where(bk < 0, NEG, 0.0).astype(F32)
                for b in range(N_BUCKETS - 1):
                    acc = jnp.where(bk == b, (relb_ref[b, h] - far) * LOG2E, acc)
                biasp_ref[h, kind, r0:r0 + LANES, c0:c0 + LANES] = acc
            rows = bks_ref.shape[1] // n_heads
            bk = bks_ref[kind, h * rows:(h + 1) * rows, :]
            acc = jnp.where(bk < 0, NEG, 0.0).astype(F32)
            for b in range(N_BUCKETS - 1):
                acc = jnp.where(bk == b, (relb_ref[b, h] - far) * LOG2E, acc)
            biass_ref[kind, h * rows:(h + 1) * rows, :] = acc
    lq = lq_ref[...].astype(F32)
    s1 = jnp.sum(lq[0:1] * lq[1:2], axis=1, keepdims=True)
    s2 = jnp.sum(lq[2:3] * lq[3:4], axis=1, keepdims=True)
    lam = jnp.exp(s1) - jnp.exp(s2) + lam_init
    lam_ref[...] = jnp.broadcast_to(lam, lam_ref.shape)
    hl = hlb_ref[...].astype(F32)
    mx = jnp.max(hl, axis=0, keepdims=True)
    e = jnp.exp(hl - mx)
    lb_ref[...] = e[0:1] / jnp.sum(e, axis=0, keepdims=True)


def _tables(rel_bias, lambda_qk_l, hgrn_lb, n_heads, n_tok, page_size, lam_init):
    c = np.arange(TQ)[:, None]
    r = np.arange(TQ)[None, :]
    diag = np.where(c <= r, _rel_bucket_np(r - c), -1)
    prev = _rel_bucket_np(TQ + r - c)
    bkp_np = np.stack([diag, prev]).astype(np.int32)
    bkp = jnp.asarray(bkp_np)
    near_blocks = tuple(
        tuple((r0, c0) for r0 in range(0, TQ, LANES) for c0 in range(0, TQ, LANES)
              if np.any((bkp_np[kind, r0:r0 + LANES, c0:c0 + LANES] >= 0)
                        & (bkp_np[kind, r0:r0 + LANES, c0:c0 + LANES] < N_BUCKETS - 1)))
        for kind in range(2))
    t = np.tile(np.arange(SAMPLE_PAD), 2 * n_heads)[:, None]
    cc = np.arange(page_size)[None, :]
    last_page = _rel_bucket_np(page_size + t - cc)
    new_page = np.where(cc <= np.minimum(t, n_tok - 1), _rel_bucket_np(t - cc), -1)
    bks = jnp.asarray(np.stack([last_page, new_page]).astype(np.int32))
    rows_s = 2 * n_heads * SAMPLE_PAD
    kern = functools.partial(_tables_kernel, n_heads=n_heads, lam_init=lam_init, near_blocks=near_blocks)
    return pl.pallas_call(
        kern,
        out_shape=(jax.ShapeDtypeStruct((n_heads, 2, TQ, TQ), F32),
                   jax.ShapeDtypeStruct((2, rows_s, page_size), F32),
                   jax.ShapeDtypeStruct((SUBLANES, LANES), F32),
                   jax.ShapeDtypeStruct((1, hgrn_lb.shape[1]), F32)),
        in_specs=[pl.BlockSpec(memory_space=pltpu.SMEM),
                  pl.BlockSpec(memory_space=pltpu.VMEM),
                  pl.BlockSpec(memory_space=pltpu.VMEM),
                  pl.BlockSpec(memory_space=pltpu.VMEM),
                  pl.BlockSpec(memory_space=pltpu.VMEM)],
        name="tables",
    )(rel_bias, lambda_qk_l, hgrn_lb, bkp, bks)


def _proj_kernel(x_ref, g1_ref, w_ref, qg_ref, kg_ref, bd_ref, *out_refs, aw, rw, vd, q_scale, emit_bf16):
    if emit_bf16:
        q_ref, kf_ref, vf_ref, kb_ref, vb_ref, r_ref = out_refs
    else:
        q_ref, kf_ref, vf_ref, r_ref = out_refs
    x = x_ref[...].astype(F32)
    ms = jnp.mean(x * x, axis=-1, keepdims=True)
    h = (x * lax.rsqrt(ms + EPS) * g1_ref[...]).astype(BF16)

    def seg_norm(z, g):
        msq = _nn((z * z).astype(BF16), bd_ref[...])
        return z * lax.rsqrt(msq + EPS) * g

    zq = _nn(h, w_ref[:, 0:aw])
    qn = seg_norm(zq, qg_ref[...]) * q_scale
    if emit_bf16:
        q_ref[0] = qn.T.astype(q_ref.dtype)
    else:
        q_ref[...] = qn.astype(q_ref.dtype)
    zk = _nn(h, w_ref[:, aw:2 * aw])
    kn = seg_norm(zk, kg_ref[...])
    zv = _nn(h, w_ref[:, 2 * aw:3 * aw])
    if emit_bf16:
        n_heads = aw // vd
        kf_ref[0] = kn.T
        kb_ref[...] = kn.astype(BF16)
        vb_ref[0] = zv.T.astype(BF16)
        for hh in range(n_heads):
            vf_ref[pl.ds(hh, zv.shape[0], stride=n_heads), :] = zv[:, hh * vd:(hh + 1) * vd]
    else:
        kf_ref[...] = kn
        vf_ref[...] = zv
    for g in range(4):
        z = _nn(h, w_ref[:, 3 * aw + g * rw:3 * aw + (g + 1) * rw])
        r_ref[:, g * rw:(g + 1) * rw] = z.astype(r_ref.dtype)


def _proj(x2, g1, w_bf, qg, kg, bd, aw, rw, vd, q_scale, emit_bf16, q_dtype, r_dtype, tm, n_batch):
    n, d = x2.shape
    kern = functools.partial(_proj_kernel, aw=aw, rw=rw, vd=vd, q_scale=q_scale, emit_bf16=emit_bf16)
    row = lambda i: (i, 0)
    const = lambda i: (0, 0)
    if emit_bf16:
        t = n // n_batch
        per_b = t // tm
        n_heads = aw // vd
        xposed = lambda i: (i // per_b, 0, i % per_b)
        out_shape = [jax.ShapeDtypeStruct((n_batch, aw, t), q_dtype),
                     jax.ShapeDtypeStruct((n_batch, aw, t), F32),
                     jax.ShapeDtypeStruct((n * n_heads, vd), F32),
                     jax.ShapeDtypeStruct((n, aw), BF16),
                     jax.ShapeDtypeStruct((n_batch, aw, t), BF16)]
        out_specs = [pl.BlockSpec((1, aw, tm), xposed),
                     pl.BlockSpec((1, aw, tm), xposed),
                     pl.BlockSpec((tm * n_heads, vd), row),
                     pl.BlockSpec((tm, aw), row),
                     pl.BlockSpec((1, aw, tm), xposed)]
    else:
        out_shape = [jax.ShapeDtypeStruct((n, aw), q_dtype),
                     jax.ShapeDtypeStruct((n, aw), F32),
                     jax.ShapeDtypeStruct((n, aw), F32)]
        out_specs = [pl.BlockSpec((tm, aw), row)] * 3
    out_shape.append(jax.ShapeDtypeStruct((n, 4 * rw), r_dtype))
    out_specs.append(pl.BlockSpec((tm, 4 * rw), row))
    return pl.pallas_call(
        kern,
        grid=(n // tm,),
        in_specs=[pl.BlockSpec((tm, d), row),
                  pl.BlockSpec((1, d), const),
                  _const_spec(w_bf.shape, const),
                  pl.BlockSpec((1, aw), const),
                  pl.BlockSpec((1, aw), const),
                  pl.BlockSpec((aw, aw), const)],
        out_specs=out_specs,
        out_shape=out_shape,
        compiler_params=pltpu.CompilerParams(dimension_semantics=("arbitrary",),
                                             vmem_limit_bytes=VMEM_LIMIT),
        name="proj_bf16" if emit_bf16 else "proj_f32",
    )(x2, g1, w_bf, qg, kg, bd)


def _attn_kernel(lam_ref, q_ref, k_ref, vt_ref, bias_ref, sg_ref, o_ref,
                 qst_sc, sa_sc, ma_sc, sb_sc, mb_sc, m_sc, acc_sc, *, hd, out_scale):
    i = pl.program_id(2)
    half = TQ // 2

    qt = q_ref[0].astype(F32)
    row = lax.broadcasted_iota(jnp.int32, qt.shape, 0)
    q0 = jnp.where(row < hd, qt, 0.0)
    q1 = jnp.where(row >= hd, qt, 0.0)
    qst_sc[...] = jnp.concatenate([q0[:, 0:half], q1[:, 0:half], q0[:, half:TQ], q1[:, half:TQ]],
                                  axis=1).astype(BF16)
    m_sc[...] = jnp.full(m_sc.shape, NEG, F32)
    acc_sc[...] = jnp.zeros(acc_sc.shape, F32)

    bufs = ((sa_sc, ma_sc), (sb_sc, mb_sc))

    def issue(buf, g, with_max):
        start = pl.multiple_of(g * TQ, TQ)
        s = _nn(k_ref[0, pl.ds(start, TQ), :], qst_sc[...])
        buf[0][...] = s
        if with_max:
            buf[1][...] = jnp.max(s, axis=0, keepdims=True)

    def issue_diag(buf, g):
        start = pl.multiple_of(g * TQ, TQ)
        buf[0][0:half, :] = _nn(k_ref[0, pl.ds(start, half), :], qst_sc[...])
        buf[0][half:TQ, TQ:2 * TQ] = _nn(k_ref[0, pl.ds(pl.multiple_of(start + half, half), half), :],
                                          qst_sc[:, TQ:2 * TQ])

    def update(s, s_max, lane0, n_lanes, key0, n_keys):
        lanes = slice(lane0, lane0 + n_lanes)
        m_old = m_sc[:, lanes]
        m_new = jnp.maximum(m_old, s_max)
        alpha = jnp.exp2(m_old - m_new)
        p = jnp.exp2(s - m_new)
        vt_ones = jnp.concatenate([vt_ref[0, :, pl.ds(key0, n_keys)], jnp.ones((ONES_ROWS, n_keys), BF16)], axis=0)
        acc_sc[:, lanes] = alpha * acc_sc[:, lanes] + _nn(vt_ones, p.astype(BF16))
        m_sc[:, lanes] = m_new

    def both_maps(s, bias):
        w = bias.shape[1]
        return jnp.concatenate([s[:, 0:w] + bias, s[:, w:2 * w] + bias], axis=1)

    def consume(buf, g, kind):
        start = pl.multiple_of(g * TQ, TQ)
        if kind == 0:
            s_lo = both_maps(buf[0][0:half, 0:TQ], bias_ref[0, 0, 0:half, 0:half])
            update(s_lo, jnp.max(s_lo, axis=0, keepdims=True), 0, TQ, start, half)
            s_hi = both_maps(buf[0][:, TQ:2 * TQ], bias_ref[0, 0, :, half:TQ])
            update(s_hi, jnp.max(s_hi, axis=0, keepdims=True), TQ, TQ, start, TQ)
            return
        s = buf[0][...]
        s_max = buf[1][...]
        if kind == 1:
            n = MAX_DISTANCE
            bias = bias_ref[0, 1, TQ - n:TQ, 0:n]
            near = s[TQ - n:TQ]
            near = jnp.concatenate([near[:, 0:n] + bias, near[:, n:half], near[:, half:half + n] + bias,
                                    near[:, half + n:2 * TQ]], axis=1)
            s = jnp.concatenate([s[0:TQ - n], near], axis=0)
            s_max = jnp.concatenate([jnp.max(s[:, 0:n], axis=0, keepdims=True), s_max[:, n:half],
                                     jnp.max(s[:, half:half + n], axis=0, keepdims=True), s_max[:, half + n:2 * TQ]],
                                    axis=1)
        update(s, s_max, 0, 2 * TQ, start, TQ)

    n_pure = jnp.maximum(i - 1, 0)
    issue(bufs[0], 0, True)

    def pair(jj, carry):
        g = 2 * jj
        issue(bufs[1], g + 1, True)
        consume(bufs[0], g, None)
        issue(bufs[0], g + 2, True)
        consume(bufs[1], g + 1, None)
        return carry

    lax.fori_loop(0, n_pure // 2, pair, 0)
    odd = (n_pure & 1) == 1

    def finish(cur, other):
        @pl.when(i > 0)
        def _():
            issue_diag(other, i)
            consume(cur, i - 1, 1)
            consume(other, i, 0)

        @pl.when(i == 0)
        def _():
            consume(cur, 0, 0)

    @pl.when(odd)
    def _():
        issue(bufs[1], n_pure, True)
        consume(bufs[0], n_pure - 1, None)
        finish(bufs[1], bufs[0])

    @pl.when(jnp.logical_not(odd))
    def _():
        finish(bufs[0], bufs[1])

    vd = qt.shape[0]
    a = acc_sc[0:vd, :] * (1.0 / acc_sc[vd:vd + 1, :])
    map0 = jnp.concatenate([a[:, 0:half], a[:, TQ:TQ + half]], axis=1)
    map1 = jnp.concatenate([a[:, half:TQ], a[:, TQ + half:2 * TQ]], axis=1)
    ot = map0 - lam_ref[0, 0] * map1
    ms = jnp.mean(ot * ot, axis=0, keepdims=True)
    o = (ot * lax.rsqrt(ms + EPS)).T
    o_ref[0] = (o * (sg_ref[...] * out_scale)).astype(o_ref.dtype)


def _attn(lam, qt, k, vt, bias, sg, n_heads, hd, out_scale):
    b, t, aw = k.shape
    vd = aw // n_heads
    assert t % TQ == 0
    kern = functools.partial(_attn_kernel, hd=hd, out_scale=out_scale)
    return pl.pallas_call(
        kern,
        grid=(b, n_heads, t // TQ),
        in_specs=[pl.BlockSpec(memory_space=pltpu.SMEM),
                  pl.BlockSpec((1, vd, TQ), lambda bi, h, i: (bi, h, i)),
                  pl.BlockSpec((1, t, vd), lambda bi, h, i: (bi, 0, h)),
                  pl.BlockSpec((1, vd, t), lambda bi, h, i: (bi, h, 0)),
                  pl.BlockSpec((1, 2, TQ, TQ), lambda bi, h, i: (h, 0, 0, 0)),
                  pl.BlockSpec((1, vd), lambda bi, h, i: (0, 0))],
        out_specs=pl.BlockSpec((1, TQ, vd), lambda bi, h, i: (bi, i, h)),
        out_shape=jax.ShapeDtypeStruct((b, t, aw), BF16),
        scratch_shapes=[pltpu.VMEM((vd, 2 * TQ), BF16),
                        pltpu.VMEM((TQ, 2 * TQ), F32),
                        pltpu.VMEM((1, 2 * TQ), F32),
                        pltpu.VMEM((TQ, 2 * TQ), F32),
                        pltpu.VMEM((1, 2 * TQ), F32),
                        pltpu.VMEM((1, 2 * TQ), F32),
                        pltpu.VMEM((vd + ONES_ROWS, 2 * TQ), F32)],
        compiler_params=pltpu.CompilerParams(dimension_semantics=("arbitrary",) * 3,
                                             vmem_limit_bytes=VMEM_LIMIT),
        name="attn_prompt",
    )(lam, qt, k, vt, bias, sg)


def _decode_seq(lam, q8, kn8, vn8, bias_ref, sg, k_refs, v_refs, o_ref, rows_out, n_heads, hd, out_scale):
    n_pages = len(k_refs)
    aw = q8.shape[1]
    vd = aw // n_heads
    ps = k_refs[0].shape[1]
    lane = lax.broadcasted_iota(jnp.int32, q8.shape, 1)
    pieces = [jnp.where((lane >= hm * hd) & (lane < (hm + 1) * hd), q8, 0.0) for hm in range(2 * n_heads)]
    qbd = jnp.concatenate(pieces, axis=0).astype(BF16)
    s_list = []
    for p in range(n_pages):
        s = _nn(qbd, k_refs[p][...].astype(BF16))
        if p == n_pages - 1:
            s = s + bias_ref[0]
        s_list.append(s)
    pad = jnp.zeros((ps - SAMPLE_PAD, aw), F32)
    k_new = jnp.concatenate([kn8, pad], axis=0).astype(BF16)
    v_new = jnp.concatenate([vn8, pad], axis=0).astype(BF16)
    s_list.append(_nt(qbd, k_new) + bias_ref[1])
    s_all = jnp.concatenate(s_list, axis=1)
    m = jnp.max(s_all, axis=1, keepdims=True)
    e = jnp.exp2(s_all - m)
    inv = 1.0 / jnp.sum(e, axis=1, keepdims=True)
    eb = e.astype(BF16)
    rows = 2 * SAMPLE_PAD
    for h in range(n_heads):
        eh = eb[h * rows:(h + 1) * rows]
        acc = _nn(eh[:, n_pages * ps:(n_pages + 1) * ps], v_new[:, h * vd:(h + 1) * vd])
        for p in range(n_pages):
            vh = v_refs[p][pl.ds(h, ps, stride=n_heads), :].astype(BF16)
            acc = acc + _nn(eh[:, p * ps:(p + 1) * ps], vh)
        acc = acc * inv[h * rows:(h + 1) * rows]
        o = acc[0:SAMPLE_PAD] - lam * acc[SAMPLE_PAD:rows]
        ms = jnp.mean(o * o, axis=-1, keepdims=True)
        o_ref[rows_out, h * vd:(h + 1) * vd] = (o * lax.rsqrt(ms + EPS) * (sg * out_scale)).astype(o_ref.dtype)


def _mid_reference(b, level):
    c, w = b.shape
    p = 1 << level
    half = p // 2
    if p >= SUBLANES:
        bp = b.reshape(c // p, p, w)
        return jnp.broadcast_to(bp[:, half - 1:half, :], (c // p, p, w)).reshape(c, w)
    b8 = b.reshape(c // SUBLANES, SUBLANES, w)
    rig = lax.broadcasted_iota(jnp.int32, b8.shape, 1)
    out = None
    for start in range(SUBLANES - p, -1, -p):
        row = jnp.broadcast_to(b8[:, start + half - 1:start + half, :], b8.shape)
        out = row if out is None else jnp.where(rig < start + p, row, out)
    return out.reshape(c, w)


def _upper_lower(up, low, level):
    c, w = up.shape
    half = (1 << level) // 2
    if half >= SUBLANES:
        shape = (c // (2 * half), 2, half, w)
        return jnp.concatenate([low.reshape(shape)[:, 0:1], up.reshape(shape)[:, 1:2]], axis=1).reshape(c, w)
    rowid = lax.broadcasted_iota(jnp.int32, up.shape, 0)
    return jnp.where((rowid & half) != 0, up, low)


def _hgrn_chunk(r_ref, r0, lb_ref, gn_ref, tri, lev, o_ref, st_sc, n_heads, kd):
    c = HG_CHUNK
    rw = n_heads * kd
    n_levels = c.bit_length() - 1
    head = lambda z, h: z[:, h * kd:(h + 1) * kd]
    q = r_ref[0, r0:r0 + c, 0:rw].astype(F32)
    rf = r_ref[0, r0:r0 + c, rw:2 * rw].astype(F32)
    v = r_ref[0, r0:r0 + c, 2 * rw:3 * rw].astype(BF16)
    rg = r_ref[0, r0:r0 + c, 3 * rw:4 * rw].astype(F32)
    lb = lb_ref[...]
    f = lb + (1.0 - lb) * jax.nn.sigmoid(rf)
    logf = jnp.log(f) * LOG2E
    kk = 1.0 - f
    hi = logf.astype(BF16)
    lo = (logf - hi.astype(F32)).astype(BF16)
    b = _nn(tri, hi) + _nn(tri, lo)
    qb = q.astype(BF16)
    kb = kk.astype(BF16)
    diag = lev == 0
    a = [jnp.where(diag, _nt(head(qb, h), head(kb, h)), 0.0) for h in range(n_heads)]
    for level in range(1, n_levels + 1):
        e = jnp.exp2(-jnp.abs(b - _mid_reference(b, level)))
        z = (_upper_lower(q, kk, level) * e).astype(BF16)
        here = lev == level
        a = [jnp.where(here, _nt(head(z, h), head(z, h)), a[h]) for h in range(n_heads)]
    b_last = b[c - 1:c, :]
    q_hat = (q * jnp.exp2(b)).astype(BF16)
    k_hat = (kk * jnp.exp2(b_last - b)).astype(BF16)
    decay = jnp.exp2(b_last)
    outs = []
    for h in range(n_heads):
        st = st_sc[h]
        o = _nn(a[h].astype(BF16), head(v, h)) + _nt(head(q_hat, h), st.astype(BF16))
        st_sc[h] = head(decay, h) * st + _tn(head(v, h), head(k_hat, h))
        ms = jnp.mean(o * o, axis=-1, keepdims=True)
        outs.append(o * lax.rsqrt(ms + EPS) * gn_ref[...])
    on = jnp.concatenate(outs, axis=1) * (rg * jax.nn.sigmoid(rg))
    o_ref[0, r0:r0 + c, :] = on.astype(o_ref.dtype)


def _mix_kernel(pt_ref, lam_ref, r_ref, lb_ref, gn_ref, tri_ref, lev_ref, q_ref, kn_ref, vn_ref, bias_ref, sg_ref,
                rs_ref, s_ref, ck_hbm, cv_hbm, o_rec_ref, sfin_ref, o_dec_ref, o_rs_ref, so_ref,
                st_sc, kbuf, vbuf, ksem, vsem,
                *, n_pages, n_heads, hd, r_heads, kd, out_scale, steps_per_batch, n_tok):
    step = pl.program_id(0)
    slot = lax.rem(step, 2)
    t = lax.rem(step, steps_per_batch)

    def page_copies(at_step, at_slot):
        copies = []
        for j in range(MIX_SEQS):
            for p in range(n_pages):
                page = pt_ref[at_step * MIX_SEQS + j, p]
                i = j * n_pages + p
                copies.append(pltpu.make_async_copy(ck_hbm.at[page], kbuf.at[at_slot, i], ksem.at[at_slot]))
                copies.append(pltpu.make_async_copy(cv_hbm.at[page], vbuf.at[at_slot, i], vsem.at[at_slot]))
        return copies

    @pl.when(step == 0)
    def _():
        for cp in page_copies(0, 0):
            cp.start()

    @pl.when(step + 1 < pl.num_programs(0))
    def _():
        for cp in page_copies(step + 1, 1 - slot):
            cp.start()

    @pl.when(t == 0)
    def _():
        st_sc[...] = jnp.zeros(st_sc.shape, F32)

    for cp in page_copies(step, slot):
        cp.wait()

    lev = lev_ref[...]
    tri = tri_ref[...]
    lam = lam_ref[0, 0]
    for c in range(MIX_SEQS):
        _hgrn_chunk(r_ref, c * HG_CHUNK, lb_ref, gn_ref, tri, lev, o_rec_ref, st_sc, r_heads, kd)
        rows = slice(c * SAMPLE_PAD, (c + 1) * SAMPLE_PAD)
        k_refs = [kbuf.at[slot, c * n_pages + p] for p in range(n_pages)]
        v_refs = [vbuf.at[slot, c * n_pages + p] for p in range(n_pages)]
        _decode_seq(lam, q_ref[rows, :].astype(F32), kn_ref[rows, :].astype(F32), vn_ref[rows, :].astype(F32),
                    bias_ref, sg_ref[...], k_refs, v_refs, o_dec_ref, rows, n_heads, hd, out_scale)
    _hgrn_new_tokens(rs_ref, lb_ref, gn_ref, s_ref, o_rs_ref, so_ref, r_heads, kd, n_tok, MIX_SEQS)

    @pl.when(t == steps_per_batch - 1)
    def _():
        for h in range(r_heads):
            sfin_ref[0, h] = st_sc[h].T


def _mix(page_table, lam, r, lb, gn, q, kn, vn, bias, sg, rs, state, ck, cv, n_heads, hd, r_heads, kd, out_scale,
         n_tok):
    n_seq, n_pages = page_table.shape
    b, t, _ = r.shape
    _, aw, ps = ck.shape
    vd = aw // n_heads
    c = HG_CHUNK
    tokens = MIX_SEQS * c
    steps_per_batch = t // tokens
    n_steps = b * steps_per_batch
    assert t % tokens == 0 and n_seq == n_steps * MIX_SEQS
    idx = np.arange(c)
    x = idx[:, None] ^ idx[None, :]
    lev = np.where(idx[:, None] > idx[None, :], np.floor(np.log2(np.maximum(x, 1))).astype(np.int32) + 1, -1)
    lev = np.where(idx[:, None] == idx[None, :], 0, lev).astype(np.int32)
    tri = (idx[:, None] >= idx[None, :]).astype(np.float32)
    kern = functools.partial(_mix_kernel, n_pages=n_pages, n_heads=n_heads, hd=hd, r_heads=r_heads, kd=kd,
                             out_scale=out_scale, steps_per_batch=steps_per_batch, n_tok=n_tok)
    c2 = lambda s, pt: (0, 0)
    tok = lambda s, pt: (s // steps_per_batch, s % steps_per_batch, 0)
    seq_rows = lambda s, pt: (s, 0)
    seq_state = lambda s, pt: (s, 0, 0, 0)
    n_k = MIX_SEQS * n_pages
    rows = MIX_SEQS * SAMPLE_PAD
    grid_spec = pltpu.PrefetchScalarGridSpec(
        num_scalar_prefetch=1,
        grid=(n_steps,),
        in_specs=[pl.BlockSpec(memory_space=pltpu.SMEM),
                  pl.BlockSpec((1, tokens, r.shape[2]), tok),
                  pl.BlockSpec((1, r_heads * kd), c2),
                  pl.BlockSpec((1, kd), c2),
                  pl.BlockSpec((c, c), c2),
                  pl.BlockSpec((c, c), c2),
                  pl.BlockSpec((rows, aw), seq_rows),
                  pl.BlockSpec((rows, aw), seq_rows),
                  pl.BlockSpec((rows, aw), seq_rows),
                  pl.BlockSpec(bias.shape, lambda s, pt: (0, 0, 0)),
                  pl.BlockSpec((1, vd), c2),
                  pl.BlockSpec((rows, rs.shape[1]), seq_rows),
                  pl.BlockSpec((MIX_SEQS, r_heads, kd, kd), seq_state),
                  pl.BlockSpec(memory_space=pl.ANY),
                  pl.BlockSpec(memory_space=pl.ANY)],
        out_specs=[pl.BlockSpec((1, tokens, r_heads * kd), tok),
                   pl.BlockSpec((1, r_heads, kd, kd), lambda s, pt: (s // steps_per_batch, 0, 0, 0)),
                   pl.BlockSpec((rows, aw), seq_rows),
                   pl.BlockSpec((rows, r_heads * kd), seq_rows),
                   pl.BlockSpec((MIX_SEQS, r_heads, kd, kd), seq_state)],
        scratch_shapes=[pltpu.VMEM((r_heads, kd, kd), F32),
                        pltpu.VMEM((2, n_k) + ck.shape[1:], ck.dtype),
                        pltpu.VMEM((2, n_k) + cv.shape[1:], cv.dtype),
                        pltpu.SemaphoreType.DMA((2,)),
                        pltpu.SemaphoreType.DMA((2,))],
    )
    return pl.pallas_call(
        kern,
        grid_spec=grid_spec,
        out_shape=[jax.ShapeDtypeStruct((b, t, r_heads * kd), BF16),
                   jax.ShapeDtypeStruct((b, r_heads, kd, kd), F32),
                   jax.ShapeDtypeStruct((n_seq * SAMPLE_PAD, aw), F32),
                   jax.ShapeDtypeStruct((n_seq * SAMPLE_PAD, r_heads * kd), F32),
                   jax.ShapeDtypeStruct(state.shape, F32)],
        compiler_params=pltpu.CompilerParams(dimension_semantics=("arbitrary",),
                                             vmem_limit_bytes=VMEM_LIMIT),
        name="hgrn_attn_decode",
    )(page_table, lam, r, lb, gn, jnp.asarray(tri, BF16), jnp.asarray(lev), q, kn, vn, bias, sg, rs, state, ck, cv)


def _hgrn_new_tokens(r_ref, lb_ref, gn_ref, s_ref, o_ref, so_ref, n_heads, kd, n_tok, n_seq):
    rw = n_heads * kd
    rowid = lax.broadcasted_iota(jnp.int32, (SAMPLE_PAD, kd), 0)
    live = rowid < n_tok
    zpad = jnp.zeros((SAMPLE_PAD, kd), F32)
    pad16 = lambda z: jnp.concatenate([z, zpad], axis=0).astype(BF16)
    ones16 = jnp.ones((2 * SAMPLE_PAD, kd), BF16)
    for g in range(n_seq):
        rows = slice(g * SAMPLE_PAD, (g + 1) * SAMPLE_PAD)
        for h in range(n_heads):
            cols = slice(h * kd, (h + 1) * kd)
            q = r_ref[rows, h * kd:(h + 1) * kd]
            rf = r_ref[rows, rw + h * kd:rw + (h + 1) * kd]
            v = r_ref[rows, 2 * rw + h * kd:2 * rw + (h + 1) * kd]
            rg = r_ref[rows, 3 * rw + h * kd:3 * rw + (h + 1) * kd]
            lb = lb_ref[:, cols]
            f = lb + (1.0 - lb) * jax.nn.sigmoid(rf)
            kk = jnp.where(live, 1.0 - f, 0.0)
            b = jnp.where(live, jnp.log(f) * LOG2E, 0.0)
            for sh in (1, 2, 4):
                b = b + jnp.where(rowid >= sh, pltpu.roll(b, sh, 0), 0.0)
            s0 = s_ref[g, h]
            o = jnp.sum(q * kk, axis=1, keepdims=True) * v
            for d in range(1, n_tok):
                x = q * pltpu.roll(kk, d, 0) * jnp.exp2(b - pltpu.roll(b, d, 0))
                w = jnp.sum(jnp.where(rowid >= d, x, 0.0), axis=1, keepdims=True)
                o = o + w * pltpu.roll(v, d, 0)
            o = o + _nn(pad16(q * jnp.exp2(b)), s0.astype(BF16))[0:SAMPLE_PAD]
            b_last = b[n_tok - 1:n_tok, :]
            decay = jnp.exp2(b_last)
            d_hi = decay.astype(BF16).astype(F32)
            d_split = jnp.where(rowid == 0, d_hi, jnp.where(rowid == 1, decay - d_hi, 0.0))
            decay_cols = _tn(pad16(d_split), ones16)
            so_ref[g, h] = decay_cols * s0 + _tn(pad16(kk * jnp.exp2(b_last - b)), pad16(v))
            ms = jnp.mean(o * o, axis=-1, keepdims=True)
            on = o * lax.rsqrt(ms + EPS) * gn_ref[...] * (rg * jax.nn.sigmoid(rg))
            o_ref[rows, cols] = on.astype(o_ref.dtype)


def _ffn_kernel(*refs, sample, aw, seq_rows):
    if sample:
        (x_ref, oa_ref, or_ref, wo_ref, g2_ref, wg_ref, wu_ref, cw_ref, cb_ref, wd_ref, prev_ref,
         y_ref, a_ref) = refs
    else:
        (x_ref, oa_ref, or_ref, wo_ref, g2_ref, wg_ref, wu_ref, cw_ref, cb_ref, wd_ref,
         y_ref, a_ref, carry_sc) = refs

        @pl.when(pl.program_id(1) == 0)
        def _():
            carry_sc[...] = jnp.zeros(carry_sc.shape, F32)

    x = x_ref[...].astype(F32)
    x1 = x + _nn(oa_ref[...].astype(BF16), wo_ref[0:aw, :]) + _nn(or_ref[...].astype(BF16), wo_ref[aw:, :])
    ms = jnp.mean(x1 * x1, axis=-1, keepdims=True)
    h2 = (x1 * lax.rsqrt(ms + EPS) * g2_ref[...]).astype(BF16)
    tm = x.shape[0]
    a = _nn(h2, wg_ref[...])
    u = _nn(h2, wu_ref[...])
    rowid = lax.broadcasted_iota(jnp.int32, a.shape, 0)
    am1 = pltpu.roll(a, 1, 0)
    am2 = pltpu.roll(a, 2, 0)
    if sample:
        rig = rowid & (seq_rows - 1)
        prev = prev_ref[...]
        am1 = jnp.where(rig == 0, pltpu.roll(prev, tm - 1, 0), am1)
        am2 = jnp.where(rig < 2, prev, am2)
        a_ref[...] = a
    else:
        last = carry_sc[...]
        am1 = jnp.where(rowid == 0, last[SUBLANES - 1:SUBLANES, :], am1)
        am2 = jnp.where(rowid == 0, last[SUBLANES - 2:SUBLANES - 1, :],
                        jnp.where(rowid == 1, last[SUBLANES - 1:SUBLANES, :], am2))
        tail = a[tm - SUBLANES:tm, :]
        carry_sc[...] = tail
        a_ref[0] = tail
    cw = cw_ref[...]
    cv = cb_ref[...] + cw[0:1, :] * am2 + cw[1:2, :] * am1 + cw[2:3, :] * a
    gate = (cv * jax.nn.sigmoid(cv) * u).astype(BF16)
    y_ref[...] = x1 + _nn(gate, wd_ref[...])


def _ffn(x2, oa, orec, wo, g2, wg, wu, cw, cb, wd, prev, tm, n_batch, seq_rows):
    n, d = x2.shape
    aw = oa.shape[1]
    dff = wg.shape[1]
    sample = prev is not None
    assert not sample or (seq_rows >= CONV_W - 1 and seq_rows & (seq_rows - 1) == 0 and tm % seq_rows == 0)
    kern = functools.partial(_ffn_kernel, sample=sample, aw=aw, seq_rows=seq_rows)
    scratch = []
    if sample:
        grid = (n // tm,)
        row = lambda i: (i, 0)
        c2 = lambda i: (0, 0)
        extra_in = [pl.BlockSpec((tm, dff), row)]
        extra_args = [prev]
        a_shape = jax.ShapeDtypeStruct((n, dff), F32)
        a_spec = pl.BlockSpec((tm, dff), row)
        sem = ("arbitrary",)
    else:
        per_b = n // n_batch // tm
        grid = (n_batch, per_b)
        row = lambda b, i: (b * per_b + i, 0)
        c2 = lambda b, i: (0, 0)
        extra_in = []
        extra_args = []
        a_shape = jax.ShapeDtypeStruct((n_batch, SUBLANES, dff), F32)
        a_spec = pl.BlockSpec((1, SUBLANES, dff), lambda b, i: (b, 0, 0))
        scratch.append(pltpu.VMEM((SUBLANES, dff), F32))
        sem = ("arbitrary", "arbitrary")
    return pl.pallas_call(
        kern,
        grid=grid,
        in_specs=[pl.BlockSpec((tm, d), row),
                  pl.BlockSpec((tm, aw), row),
                  pl.BlockSpec((tm, aw), row),
                  _const_spec(wo.shape, c2),
                  pl.BlockSpec((1, d), c2),
                  _const_spec(wg.shape, c2),
                  _const_spec(wu.shape, c2),
                  pl.BlockSpec(cw.shape, c2),
                  pl.BlockSpec(cb.shape, c2),
                  _const_spec(wd.shape, c2)] + extra_in,
        out_specs=[pl.BlockSpec((tm, d), row), a_spec],
        out_shape=[jax.ShapeDtypeStruct((n, d), F32), a_shape],
        scratch_shapes=scratch,
        compiler_params=pltpu.CompilerParams(dimension_semantics=sem, vmem_limit_bytes=VMEM_LIMIT),
        name="ffn_sample" if sample else "ffn_prompt",
    )(x2, oa, orec, wo, g2, wg, wu, cw, cb, wd, *extra_args)


def kernel(x_prompt, x_sample, cache_k, cache_v, page_table, state_hgrn, state_conv, rel_bias, norm1_g, w_in,
           qk_norm_g, lambda_qk, subln_g, hgrn_lb, hgrn_onorm_g, w_out, norm2_g, w_gate, w_up, conv_w, conv_b,
           w_down):
    bp, tp, d = x_prompt.shape
    bs, ts, _ = x_sample.shape
    depth = w_in.shape[0]
    assert depth == 1 and CONV_W - 1 <= ts <= SAMPLE_PAD and conv_w.shape[1] == CONV_W
    n_heads, _, hd = cache_k.shape[3:]
    vd = cache_v.shape[4]
    aw = n_heads * vd
    r_heads, kd = state_hgrn.shape[2], state_hgrn.shape[3]
    rw = r_heads * kd
    dff = w_gate.shape[2]
    page_size = cache_k.shape[2]
    assert page_size >= MAX_DISTANCE and tp % TQ == 0
    l = 0
    lam_init = 0.8 - 0.6 * math.exp(-0.3 * l)
    out_scale = 1.0 - lam_init

    biasp, biass, lam, lb = _tables(rel_bias, lambda_qk[l], hgrn_lb, n_heads, ts, page_size, lam_init)

    w_in_bf = w_in[l].astype(BF16)
    wo_bf = w_out[l].astype(BF16)
    wg_bf = w_gate[l].astype(BF16)
    wu_bf = w_up[l].astype(BF16)
    wd_bf = w_down[l].astype(BF16)
    cw = conv_w[l]
    cb = conv_b[l].reshape(1, dff)
    g1 = norm1_g[l].reshape(1, d)
    g2 = norm2_g[l].reshape(1, d)
    qg = jnp.tile(qk_norm_g[l, 0].reshape(1, 2 * hd), (1, n_heads))
    kg = jnp.tile(qk_norm_g[l, 1].reshape(1, 2 * hd), (1, n_heads))
    seg = np.arange(aw) // hd
    bd = jnp.asarray((seg[:, None] == seg[None, :]).astype(np.float32) / hd, BF16)
    sg = subln_g[l].reshape(1, vd)
    gn = hgrn_onorm_g[l].reshape(1, kd)
    q_scale = hd ** -0.5 * LOG2E

    xp2 = x_prompt.reshape(bp * tp, d)
    qbt, kft, vf, kb, vbt, rp = _proj(xp2, g1, w_in_bf, qg, kg, bd, aw, rw, vd, q_scale, True, BF16, BF16, 512, bp)
    xs_pad = jnp.pad(x_sample, ((0, 0), (0, SAMPLE_PAD - ts), (0, 0))).reshape(bs * SAMPLE_PAD, d)
    qs, ksf, vsf, rs = _proj(xs_pad, g1, w_in_bf, qg, kg, bd, aw, rw, vd, q_scale, False, F32, F32, 512, bs)

    o_att = _attn(lam, qbt, kb.reshape(bp, tp, aw), vbt, biasp, sg, n_heads, hd, out_scale)
    ck = jnp.transpose(cache_k[l], (0, 2, 3, 4, 1)).reshape(cache_k.shape[1], aw, page_size)
    cv = cache_v[l].reshape(cache_v.shape[1], page_size * n_heads, vd)
    o_rec, s_p, o_att_s, o_rec_s, s_s = _mix(page_table, lam, rp.reshape(bp, tp, 4 * rw), lb, gn, qs, ksf, vsf, biass,
                                             sg, rs, state_hgrn[l], ck, cv, n_heads, hd, r_heads, kd, out_scale, ts)

    yp, a_tail = _ffn(xp2, o_att.reshape(bp * tp, aw), o_rec.reshape(bp * tp, rw), wo_bf, g2, wg_bf, wu_bf, cw, cb,
                      wd_bf, None, 512, bp, None)
    k_prompt = kft.reshape(bp, n_heads, 2, hd, tp).transpose(0, 4, 1, 2, 3)[None]
    v_prompt = vf.reshape(1, bp, tp, n_heads, vd)
    conv_prompt = a_tail[:, SUBLANES - (CONV_W - 1):][None]
    unpad = lambda z: z.reshape(bs, SAMPLE_PAD, -1)[:, :ts]
    flat = lambda z: unpad(z).reshape(bs * ts, -1)
    prev = jnp.pad(state_conv[l], ((0, 0), (0, ts - (CONV_W - 1)), (0, 0))).reshape(bs * ts, dff)
    ys, a_s = _ffn(x_sample.reshape(bs * ts, d), flat(o_att_s), flat(o_rec_s), wo_bf, g2, wg_bf, wu_bf, cw, cb,
                   wd_bf, prev, 256, 1, ts)
    y_sample = ys.reshape(bs, ts, d)
    k_sample = unpad(ksf).reshape(1, bs, ts, n_heads, 2, hd)
    v_sample = unpad(vsf).reshape(1, bs, ts, n_heads, vd)
    conv_sample = a_s.reshape(bs, ts, dff)[:, ts - (CONV_W - 1):ts][None]

    return (yp.reshape(bp, tp, d), y_sample, k_prompt, v_prompt, s_p[None], conv_prompt,
            k_sample, v_sample, s_s[None], conv_sample)
```

```python
import functools
import math

import numpy as np
import jax
import jax.numpy as jnp
from jax import lax
from jax.experimental import pallas as pl
from jax.experimental.pallas import tpu as pltpu

F32 = jnp.float32
BF16 = jnp.bfloat16

EPS = 1e-6
N_BUCKETS = 32
MAX_DISTANCE = 128
CONV_W = 3
NEG = -1e30
LOG2E = math.log2(math.e)
LANES = 128
SUBLANES = 8

TQ = 512
ONES_ROWS = 16
HG_CHUNK = 128
MIX_SEQS = 1
PAGE_SLOTS = 4
SAMPLE_PAD = 8
VMEM_LIMIT = 56 * 1024 * 1024


def _const_spec(shape, index_map):
    return pl.BlockSpec(shape, index_map, pipeline_mode=pl.Buffered(1))


def _nt(a, b):
    return lax.dot_general(a, b, (((1,), (1,)), ((), ())), preferred_element_type=F32)


def _tn(a, b):
    return lax.dot_general(a, b, (((0,), (0,)), ((), ())), preferred_element_type=F32)


def _nn(a, b):
    return jnp.dot(a, b, preferred_element_type=F32)


def _rel_bucket_np(n):
    n = np.maximum(n, 0)
    max_exact = N_BUCKETS // 2
    nf = np.maximum(n, 1).astype(np.float32)
    large = max_exact + (np.log(nf / np.float32(max_exact)) / np.float32(math.log(MAX_DISTANCE / max_exact))
                         * np.float32(N_BUCKETS - max_exact)).astype(np.int32)
    large = np.minimum(large, N_BUCKETS - 1)
    return np.where(n < max_exact, n, large).astype(np.int32)


def _tables_kernel(relb_ref, lq_ref, hlb_ref, bkp_ref, bks_ref,
                   biasp_ref, biass_ref, lam_ref, lb_ref, *, n_heads, lam_init, near_blocks):
    for h in range(n_heads):
        far = relb_ref[N_BUCKETS - 1, h]
        for kind in range(2):
            biasp_ref[h, kind] = jnp.where(bkp_ref[kind] < 0, NEG, 0.0).astype(F32)
            for (r0, c0) in near_blocks[kind]:
                bk = bkp_ref[kind, r0:r0 + LANES, c0:c0 + LANES]
                acc = jnp.where(bk < 0, NEG, 0.0).astype(F32)
                for b in range(N_BUCKETS - 1):
                    acc = jnp.where(bk == b, (relb_ref[b, h] - far) * LOG2E, acc)
                biasp_ref[h, kind, r0:r0 + LANES, c0:c0 + LANES] = acc
            rows = bks_ref.shape[1] // n_heads
            bk = bks_ref[kind, h * rows:(h + 1) * rows, :]
            acc = jnp.where(bk < 0, NEG, 0.0).astype(F32)
            for b in range(N_BUCKETS - 1):
                acc = jnp.where(bk == b, (relb_ref[b, h] - far) * LOG2E, acc)
            biass_ref[kind, h * rows:(h + 1) * rows, :] = acc
    lq = lq_ref[...].astype(F32)
    s1 = jnp.sum(lq[0:1] * lq[1:2], axis=1, keepdims=True)
    s2 = jnp.sum(lq[2:3] * lq[3:4], axis=1, keepdims=True)
    lam = jnp.exp(s1) - jnp.exp(s2) + lam_init
    lam_ref[...] = jnp.broadcast_to(lam, lam_ref.shape)
    hl = hlb_ref[...].astype(F32)
    mx = jnp.max(hl, axis=0, keepdims=True)
    e = jnp.exp(hl - mx)
    lb_ref[...] = e[0:1] / jnp.sum(e, axis=0, keepdims=True)


def _tables(rel_bias, lambda_qk_l, hgrn_lb, n_heads, n_tok, page_size, lam_init):
    c = np.arange(TQ)[:, None]
    r = np.arange(TQ)[None, :]
    diag = np.where(c <= r, _rel_bucket_np(r - c), -1)
    prev = _rel_bucket_np(TQ + r - c)
    bkp_np = np.stack([diag, prev]).astype(np.int32)
    bkp = jnp.asarray(bkp_np)
    near_blocks = tuple(
        tuple((r0, c0) for r0 in range(0, TQ, LANES) for c0 in range(0, TQ, LANES)
              if np.any((bkp_np[kind, r0:r0 + LANES, c0:c0 + LANES] >= 0)
                        & (bkp_np[kind, r0:r0 + LANES, c0:c0 + LANES] < N_BUCKETS - 1)))
        for kind in range(2))
    t = np.tile(np.arange(SAMPLE_PAD), 2 * n_heads)[:, None]
    cc = np.arange(page_size)[None, :]
    last_page = _rel_bucket_np(page_size + t - cc)
    new_page = np.where(cc <= np.minimum(t, n_tok - 1), _rel_bucket_np(t - cc), -1)
    bks = jnp.asarray(np.stack([last_page, new_page]).astype(np.int32))
    rows_s = 2 * n_heads * SAMPLE_PAD
    kern = functools.partial(_tables_kernel, n_heads=n_heads, lam_init=lam_init, near_blocks=near_blocks)
    return pl.pallas_call(
        kern,
        out_shape=(jax.ShapeDtypeStruct((n_heads, 2, TQ, TQ), F32),
                   jax.ShapeDtypeStruct((2, rows_s, page_size), F32),
                   jax.ShapeDtypeStruct((SUBLANES, LANES), F32),
                   jax.ShapeDtypeStruct((1, hgrn_lb.shape[1]), F32)),
        in_specs=[pl.BlockSpec(memory_space=pltpu.SMEM),
                  pl.BlockSpec(memory_space=pltpu.VMEM),
                  pl.BlockSpec(memory_space=pltpu.VMEM),
                  pl.BlockSpec(memory_space=pltpu.VMEM),
                  pl.BlockSpec(memory_space=pltpu.VMEM)],
        name="tables",
    )(rel_bias, lambda_qk_l, hgrn_lb, bkp, bks)


def _proj_kernel(x_ref, g1_ref, w_ref, qg_ref, kg_ref, bd_ref, *out_refs, aw, rw, vd, q_scale, emit_bf16):
    if emit_bf16:
        q_ref, kf_ref, vf_ref, kb_ref, vb_ref, r_ref = out_refs
    else:
        q_ref, kf_ref, vf_ref, r_ref = out_refs
    x = x_ref[...].astype(F32)
    ms = jnp.mean(x * x, axis=-1, keepdims=True)
    h = (x * lax.rsqrt(ms + EPS) * g1_ref[...]).astype(BF16)

    def seg_norm(z, g):
        msq = _nn((z * z).astype(BF16), bd_ref[...])
        return z * lax.rsqrt(msq + EPS) * g

    zq = _nn(h, w_ref[:, 0:aw])
    qn = seg_norm(zq, qg_ref[...]) * q_scale
    if emit_bf16:
        q_ref[0] = qn.T.astype(q_ref.dtype)
    else:
        q_ref[...] = qn.astype(q_ref.dtype)
    zk = _nn(h, w_ref[:, aw:2 * aw])
    kn = seg_norm(zk, kg_ref[...])
    zv = _nn(h, w_ref[:, 2 * aw:3 * aw])
    if emit_bf16:
        n_heads = aw // vd
        kf_ref[0] = kn.T
        kb_ref[...] = kn.astype(BF16)
        vb_ref[0] = zv.T.astype(BF16)
        for hh in range(n_heads):
            vf_ref[pl.ds(hh, zv.shape[0], stride=n_heads), :] = zv[:, hh * vd:(hh + 1) * vd]
    else:
        kf_ref[...] = kn
        vf_ref[...] = zv
    for g in range(4):
        z = _nn(h, w_ref[:, 3 * aw + g * rw:3 * aw + (g + 1) * rw])
        r_ref[:, g * rw:(g + 1) * rw] = z.astype(r_ref.dtype)


def _proj(x2, g1, w_bf, qg, kg, bd, aw, rw, vd, q_scale, emit_bf16, q_dtype, r_dtype, tm, n_batch):
    n, d = x2.shape
    kern = functools.partial(_proj_kernel, aw=aw, rw=rw, vd=vd, q_scale=q_scale, emit_bf16=emit_bf16)
    row = lambda i: (i, 0)
    const = lambda i: (0, 0)
    if emit_bf16:
        t = n // n_batch
        per_b = t // tm
        n_heads = aw // vd
        xposed = lambda i: (i // per_b, 0, i % per_b)
        out_shape = [jax.ShapeDtypeStruct((n_batch, aw, t), q_dtype),
                     jax.ShapeDtypeStruct((n_batch, aw, t), F32),
                     jax.ShapeDtypeStruct((n * n_heads, vd), F32),
                     jax.ShapeDtypeStruct((n, aw), BF16),
                     jax.ShapeDtypeStruct((n_batch, aw, t), BF16)]
        out_specs = [pl.BlockSpec((1, aw, tm), xposed),
                     pl.BlockSpec((1, aw, tm), xposed),
                     pl.BlockSpec((tm * n_heads, vd), row),
                     pl.BlockSpec((tm, aw), row),
                     pl.BlockSpec((1, aw, tm), xposed)]
    else:
        out_shape = [jax.ShapeDtypeStruct((n, aw), q_dtype),
                     jax.ShapeDtypeStruct((n, aw), F32),
                     jax.ShapeDtypeStruct((n, aw), F32)]
        out_specs = [pl.BlockSpec((tm, aw), row)] * 3
    out_shape.append(jax.ShapeDtypeStruct((n, 4 * rw), r_dtype))
    out_specs.append(pl.BlockSpec((tm, 4 * rw), row))
    return pl.pallas_call(
        kern,
        grid=(n // tm,),
        in_specs=[pl.BlockSpec((tm, d), row),
                  pl.BlockSpec((1, d), const),
                  _const_spec(w_bf.shape, const),
                  pl.BlockSpec((1, aw), const),
                  pl.BlockSpec((1, aw), const),
                  pl.BlockSpec((aw, aw), const)],
        out_specs=out_specs,
        out_shape=out_shape,
        compiler_params=pltpu.CompilerParams(dimension_semantics=("arbitrary",),
                                             vmem_limit_bytes=VMEM_LIMIT),
        name="proj_bf16" if emit_bf16 else "proj_f32",
    )(x2, g1, w_bf, qg, kg, bd)


def _attn_kernel(lam_ref, q_ref, k_ref, vt_ref, bias_ref, sg_ref, o_ref,
                 qst_sc, sa_sc, ma_sc, sb_sc, mb_sc, m_sc, acc_sc, *, hd, out_scale):
    i = pl.program_id(2)
    half = TQ // 2

    qt = q_ref[0].astype(F32)
    row = lax.broadcasted_iota(jnp.int32, qt.shape, 0)
    q0 = jnp.where(row < hd, qt, 0.0)
    q1 = jnp.where(row >= hd, qt, 0.0)
    qst_sc[...] = jnp.concatenate([q0[:, 0:half], q1[:, 0:half], q0[:, half:TQ], q1[:, half:TQ]],
                                  axis=1).astype(BF16)
    m_sc[...] = jnp.full(m_sc.shape, NEG, F32)
    acc_sc[...] = jnp.zeros(acc_sc.shape, F32)

    bufs = ((sa_sc, ma_sc), (sb_sc, mb_sc))

    def issue(buf, g, with_max):
        start = pl.multiple_of(g * TQ, TQ)
        s = _nn(k_ref[0, pl.ds(start, TQ), :], qst_sc[...])
        buf[0][...] = s
        if with_max:
            buf[1][...] = jnp.max(s, axis=0, keepdims=True)

    def issue_diag(buf, g):
        start = pl.multiple_of(g * TQ, TQ)
        buf[0][0:half, :] = _nn(k_ref[0, pl.ds(start, half), :], qst_sc[...])
        buf[0][half:TQ, TQ:2 * TQ] = _nn(k_ref[0, pl.ds(pl.multiple_of(start + half, half), half), :],
                                          qst_sc[:, TQ:2 * TQ])

    def update(s, s_max, lane0, n_lanes, key0, n_keys):
        lanes = slice(lane0, lane0 + n_lanes)
        m_old = m_sc[:, lanes]
        m_new = jnp.maximum(m_old, s_max)
        alpha = jnp.exp2(m_old - m_new)
        p = jnp.exp2(s - m_new)
        vt_ones = jnp.concatenate([vt_ref[0, :, pl.ds(key0, n_keys)], jnp.ones((ONES_ROWS, n_keys), BF16)], axis=0)
        acc_sc[:, lanes] = alpha * acc_sc[:, lanes] + _nn(vt_ones, p.astype(BF16))
        m_sc[:, lanes] = m_new

    def both_maps(s, bias):
        w = bias.shape[1]
        return jnp.concatenate([s[:, 0:w] + bias, s[:, w:2 * w] + bias], axis=1)

    def consume(buf, g, kind):
        start = pl.multiple_of(g * TQ, TQ)
        if kind == 0:
            s_lo = both_maps(buf[0][0:half, 0:TQ], bias_ref[0, 0, 0:half, 0:half])
            update(s_lo, jnp.max(s_lo, axis=0, keepdims=True), 0, TQ, start, half)
            s_hi = both_maps(buf[0][:, TQ:2 * TQ], bias_ref[0, 0, :, half:TQ])
            update(s_hi, jnp.max(s_hi, axis=0, keepdims=True), TQ, TQ, start, TQ)
            return
        s = buf[0][...]
        s_max = buf[1][...]
        if kind == 1:
            n = MAX_DISTANCE
            bias = bias_ref[0, 1, TQ - n:TQ, 0:n]
            near = s[TQ - n:TQ]
            near = jnp.concatenate([near[:, 0:n] + bias, near[:, n:half], near[:, half:half + n] + bias,
                                    near[:, half + n:2 * TQ]], axis=1)
            s = jnp.concatenate([s[0:TQ - n], near], axis=0)
            s_max = jnp.concatenate([jnp.max(s[:, 0:n], axis=0, keepdims=True), s_max[:, n:half],
                                     jnp.max(s[:, half:half + n], axis=0, keepdims=True), s_max[:, half + n:2 * TQ]],
                                    axis=1)
        update(s, s_max, 0, 2 * TQ, start, TQ)

    n_pure = jnp.maximum(i - 1, 0)
    issue(bufs[0], 0, True)

    def pair(jj, carry):
        g = 2 * jj
        issue(bufs[1], g + 1, True)
        consume(bufs[0], g, None)
        issue(bufs[0], g + 2, True)
        consume(bufs[1], g + 1, None)
        return carry

    lax.fori_loop(0, n_pure // 2, pair, 0)
    odd = (n_pure & 1) == 1

    def finish(cur, other):
        @pl.when(i > 0)
        def _():
            issue_diag(other, i)
            consume(cur, i - 1, 1)
            consume(other, i, 0)

        @pl.when(i == 0)
        def _():
            consume(cur, 0, 0)

    @pl.when(odd)
    def _():
        issue(bufs[1], n_pure, True)
        consume(bufs[0], n_pure - 1, None)
        finish(bufs[1], bufs[0])

    @pl.when(jnp.logical_not(odd))
    def _():
        finish(bufs[0], bufs[1])

    vd = qt.shape[0]
    a = acc_sc[0:vd, :] * (1.0 / acc_sc[vd:vd + 1, :])
    map0 = jnp.concatenate([a[:, 0:half], a[:, TQ:TQ + half]], axis=1)
    map1 = jnp.concatenate([a[:, half:TQ], a[:, TQ + half:2 * TQ]], axis=1)
    ot = map0 - lam_ref[0, 0] * map1
    ms = jnp.mean(ot * ot, axis=0, keepdims=True)
    o = (ot * lax.rsqrt(ms + EPS)).T
    o_ref[0] = (o * (sg_ref[...] * out_scale)).astype(o_ref.dtype)


def _attn(lam, qt, k, vt, bias, sg, n_heads, hd, out_scale):
    b, t, aw = k.shape
    vd = aw // n_heads
    assert t % TQ == 0
    kern = functools.partial(_attn_kernel, hd=hd, out_scale=out_scale)
    return pl.pallas_call(
        kern,
        grid=(b, n_heads, t // TQ),
        in_specs=[pl.BlockSpec(memory_space=pltpu.SMEM),
                  pl.BlockSpec((1, vd, TQ), lambda bi, h, i: (bi, h, i)),
                  pl.BlockSpec((1, t, vd), lambda bi, h, i: (bi, 0, h)),
                  pl.BlockSpec((1, vd, t), lambda bi, h, i: (bi, h, 0)),
                  pl.BlockSpec((1, 2, TQ, TQ), lambda bi, h, i: (h, 0, 0, 0)),
                  pl.BlockSpec((1, vd), lambda bi, h, i: (0, 0))],
        out_specs=pl.BlockSpec((1, TQ, vd), lambda bi, h, i: (bi, i, h)),
        out_shape=jax.ShapeDtypeStruct((b, t, aw), BF16),
        scratch_shapes=[pltpu.VMEM((vd, 2 * TQ), BF16),
                        pltpu.VMEM((TQ, 2 * TQ), F32),
                        pltpu.VMEM((1, 2 * TQ), F32),
                        pltpu.VMEM((TQ, 2 * TQ), F32),
                        pltpu.VMEM((1, 2 * TQ), F32),
                        pltpu.VMEM((1, 2 * TQ), F32),
                        pltpu.VMEM((vd + ONES_ROWS, 2 * TQ), F32)],
        compiler_params=pltpu.CompilerParams(dimension_semantics=("arbitrary",) * 3,
                                             vmem_limit_bytes=VMEM_LIMIT),
        name="attn_prompt",
    )(lam, qt, k, vt, bias, sg)


def _decode_seq(lam, q8, kn8, vn8, bias_ref, sg, k_refs, v_refs, o_ref, rows_out, n_heads, hd, out_scale):
    n_pages = len(k_refs)
    aw = q8.shape[1]
    vd = aw // n_heads
    ps = k_refs[0].shape[1]
    lane = lax.broadcasted_iota(jnp.int32, q8.shape, 1)
    pieces = [jnp.where((lane >= hm * hd) & (lane < (hm + 1) * hd), q8, 0.0) for hm in range(2 * n_heads)]
    qbd = jnp.concatenate(pieces, axis=0).astype(BF16)
    s_list = []
    for p in range(n_pages):
        s = _nn(qbd, k_refs[p][...].astype(BF16))
        if p == n_pages - 1:
            s = s + bias_ref[0]
        s_list.append(s)
    pad = jnp.zeros((ps - SAMPLE_PAD, aw), F32)
    k_new = jnp.concatenate([kn8, pad], axis=0).astype(BF16)
    v_new = jnp.concatenate([vn8, pad], axis=0).astype(BF16)
    s_list.append(_nt(qbd, k_new) + bias_ref[1])
    s_all = jnp.concatenate(s_list, axis=1)
    m = jnp.max(s_all, axis=1, keepdims=True)
    e = jnp.exp2(s_all - m)
    inv = 1.0 / jnp.sum(e, axis=1, keepdims=True)
    eb = e.astype(BF16)
    rows = 2 * SAMPLE_PAD
    for h in range(n_heads):
        eh = eb[h * rows:(h + 1) * rows]
        acc = _nn(eh[:, n_pages * ps:(n_pages + 1) * ps], v_new[:, h * vd:(h + 1) * vd])
        for p in range(n_pages):
            vh = v_refs[p][pl.ds(h, ps, stride=n_heads), :].astype(BF16)
            acc = acc + _nn(eh[:, p * ps:(p + 1) * ps], vh)
        acc = acc * inv[h * rows:(h + 1) * rows]
        o = acc[0:SAMPLE_PAD] - lam * acc[SAMPLE_PAD:rows]
        ms = jnp.mean(o * o, axis=-1, keepdims=True)
        o_ref[rows_out, h * vd:(h + 1) * vd] = (o * lax.rsqrt(ms + EPS) * (sg * out_scale)).astype(o_ref.dtype)


def _mid_reference(b, level):
    c, w = b.shape
    p = 1 << level
    half = p // 2
    if p >= SUBLANES:
        bp = b.reshape(c // p, p, w)
        return jnp.broadcast_to(bp[:, half - 1:half, :], (c // p, p, w)).reshape(c, w)
    b8 = b.reshape(c // SUBLANES, SUBLANES, w)
    rig = lax.broadcasted_iota(jnp.int32, b8.shape, 1)
    out = None
    for start in range(SUBLANES - p, -1, -p):
        row = jnp.broadcast_to(b8[:, start + half - 1:start + half, :], b8.shape)
        out = row if out is None else jnp.where(rig < start + p, row, out)
    return out.reshape(c, w)


def _upper_lower(up, low, level):
    c, w = up.shape
    half = (1 << level) // 2
    if half >= SUBLANES:
        shape = (c // (2 * half), 2, half, w)
        return jnp.concatenate([low.reshape(shape)[:, 0:1], up.reshape(shape)[:, 1:2]], axis=1).reshape(c, w)
    rowid = lax.broadcasted_iota(jnp.int32, up.shape, 0)
    return jnp.where((rowid & half) != 0, up, low)


def _hgrn_chunk(r_ref, r0, lb_ref, gn_ref, tri, lev, o_ref, st_sc, n_heads, kd):
    c = HG_CHUNK
    rw = n_heads * kd
    n_levels = c.bit_length() - 1
    head = lambda z, h: z[:, h * kd:(h + 1) * kd]
    q = r_ref[0, r0:r0 + c, 0:rw].astype(F32)
    rf = r_ref[0, r0:r0 + c, rw:2 * rw].astype(F32)
    v = r_ref[0, r0:r0 + c, 2 * rw:3 * rw].astype(BF16)
    rg = r_ref[0, r0:r0 + c, 3 * rw:4 * rw].astype(F32)
    lb = lb_ref[...]
    f = lb + (1.0 - lb) * jax.nn.sigmoid(rf)
    logf = jnp.log(f) * LOG2E
    kk = 1.0 - f
    hi = logf.astype(BF16)
    lo = (logf - hi.astype(F32)).astype(BF16)
    b = _nn(tri, hi) + _nn(tri, lo)
    qb = q.astype(BF16)
    kb = kk.astype(BF16)
    diag = lev == 0
    a = [jnp.where(diag, _nt(head(qb, h), head(kb, h)), 0.0) for h in range(n_heads)]
    for level in range(1, n_levels + 1):
        e = jnp.exp2(-jnp.abs(b - _mid_reference(b, level)))
        z = (_upper_lower(q, kk, level) * e).astype(BF16)
        here = lev == level
        a = [jnp.where(here, _nt(head(z, h), head(z, h)), a[h]) for h in range(n_heads)]
    b_last = b[c - 1:c, :]
    q_hat = (q * jnp.exp2(b)).astype(BF16)
    k_hat = (kk * jnp.exp2(b_last - b)).astype(BF16)
    decay = jnp.exp2(b_last)
    outs = []
    for h in range(n_heads):
        st = st_sc[h]
        o = _nn(a[h].astype(BF16), head(v, h)) + _nt(head(q_hat, h), st.astype(BF16))
        st_sc[h] = head(decay, h) * st + _tn(head(v, h), head(k_hat, h))
        ms = jnp.mean(o * o, axis=-1, keepdims=True)
        outs.append(o * lax.rsqrt(ms + EPS) * gn_ref[...])
    on = jnp.concatenate(outs, axis=1) * (rg * jax.nn.sigmoid(rg))
    o_ref[0, r0:r0 + c, :] = on.astype(o_ref.dtype)


def _mix_kernel(pt_ref, lam_ref, r_ref, lb_ref, gn_ref, tri_ref, lev_ref, q_ref, kn_ref, vn_ref, bias_ref, sg_ref,
                rs_ref, s_ref, ck_hbm, cv_hbm, o_rec_ref, sfin_ref, o_dec_ref, o_rs_ref, so_ref,
                st_sc, kbuf, vbuf, ksem, vsem,
                *, n_pages, n_heads, hd, r_heads, kd, out_scale, steps_per_batch, n_tok):
    step = pl.program_id(0)
    slot = lax.rem(step, PAGE_SLOTS)
    ahead = PAGE_SLOTS - 1
    t = lax.rem(step, steps_per_batch)

    def page_copies(at_step, at_slot):
        copies = []
        for j in range(MIX_SEQS):
            for p in range(n_pages):
                page = pt_ref[at_step * MIX_SEQS + j, p]
                i = j * n_pages + p
                copies.append(pltpu.make_async_copy(ck_hbm.at[page], kbuf.at[at_slot, i], ksem.at[at_slot]))
                copies.append(pltpu.make_async_copy(cv_hbm.at[page], vbuf.at[at_slot, i], vsem.at[at_slot]))
        return copies

    for first in range(ahead):
        @pl.when((step == 0) & (first < pl.num_programs(0)))
        def _(first=first):
            for cp in page_copies(first, first):
                cp.start()

    @pl.when(step + ahead < pl.num_programs(0))
    def _():
        for cp in page_copies(step + ahead, lax.rem(step + ahead, PAGE_SLOTS)):
            cp.start()

    @pl.when(t == 0)
    def _():
        st_sc[...] = jnp.zeros(st_sc.shape, F32)

    for cp in page_copies(step, slot):
        cp.wait()

    lev = lev_ref[...]
    tri = tri_ref[...]
    lam = lam_ref[0, 0]
    for c in range(MIX_SEQS):
        _hgrn_chunk(r_ref, c * HG_CHUNK, lb_ref, gn_ref, tri, lev, o_rec_ref, st_sc, r_heads, kd)
        rows = slice(c * SAMPLE_PAD, (c + 1) * SAMPLE_PAD)
        k_refs = [kbuf.at[slot, c * n_pages + p] for p in range(n_pages)]
        v_refs = [vbuf.at[slot, c * n_pages + p] for p in range(n_pages)]
        _decode_seq(lam, q_ref[rows, :].astype(F32), kn_ref[rows, :].astype(F32), vn_ref[rows, :].astype(F32),
                    bias_ref, sg_ref[...], k_refs, v_refs, o_dec_ref, rows, n_heads, hd, out_scale)
    _hgrn_new_tokens(rs_ref, lb_ref, gn_ref, s_ref, o_rs_ref, so_ref, r_heads, kd, n_tok, MIX_SEQS)

    @pl.when(t == steps_per_batch - 1)
    def _():
        for h in range(r_heads):
            sfin_ref[0, h] = st_sc[h].T


def _mix(page_table, lam, r, lb, gn, q, kn, vn, bias, sg, rs, state, ck, cv, n_heads, hd, r_heads, kd, out_scale,
         n_tok):
    n_seq, n_pages = page_table.shape
    b, t, _ = r.shape
    _, aw, ps = ck.shape
    vd = aw // n_heads
    c = HG_CHUNK
    tokens = MIX_SEQS * c
    steps_per_batch = t // tokens
    n_steps = b * steps_per_batch
    assert t % tokens == 0 and n_seq == n_steps * MIX_SEQS
    idx = np.arange(c)
    x = idx[:, None] ^ idx[None, :]
    lev = np.where(idx[:, None] > idx[None, :], np.floor(np.log2(np.maximum(x, 1))).astype(np.int32) + 1, -1)
    lev = np.where(idx[:, None] == idx[None, :], 0, lev).astype(np.int32)
    tri = (idx[:, None] >= idx[None, :]).astype(np.float32)
    kern = functools.partial(_mix_kernel, n_pages=n_pages, n_heads=n_heads, hd=hd, r_heads=r_heads, kd=kd,
                             out_scale=out_scale, steps_per_batch=steps_per_batch, n_tok=n_tok)
    c2 = lambda s, pt: (0, 0)
    tok = lambda s, pt: (s // steps_per_batch, s % steps_per_batch, 0)
    seq_rows = lambda s, pt: (s, 0)
    seq_state = lambda s, pt: (s, 0, 0, 0)
    n_k = MIX_SEQS * n_pages
    rows = MIX_SEQS * SAMPLE_PAD
    grid_spec = pltpu.PrefetchScalarGridSpec(
        num_scalar_prefetch=1,
        grid=(n_steps,),
        in_specs=[pl.BlockSpec(memory_space=pltpu.SMEM),
                  pl.BlockSpec((1, tokens, r.shape[2]), tok),
                  pl.BlockSpec((1, r_heads * kd), c2),
                  pl.BlockSpec((1, kd), c2),
                  pl.BlockSpec((c, c), c2),
                  pl.BlockSpec((c, c), c2),
                  pl.BlockSpec((rows, aw), seq_rows),
                  pl.BlockSpec((rows, aw), seq_rows),
                  pl.BlockSpec((rows, aw), seq_rows),
                  pl.BlockSpec(bias.shape, lambda s, pt: (0, 0, 0)),
                  pl.BlockSpec((1, vd), c2),
                  pl.BlockSpec((rows, rs.shape[1]), seq_rows),
                  pl.BlockSpec((MIX_SEQS, r_heads, kd, kd), seq_state),
                  pl.BlockSpec(memory_space=pl.ANY),
                  pl.BlockSpec(memory_space=pl.ANY)],
        out_specs=[pl.BlockSpec((1, tokens, r_heads * kd), tok),
                   pl.BlockSpec((1, r_heads, kd, kd), lambda s, pt: (s // steps_per_batch, 0, 0, 0)),
                   pl.BlockSpec((rows, aw), seq_rows),
                   pl.BlockSpec((rows, r_heads * kd), seq_rows),
                   pl.BlockSpec((MIX_SEQS, r_heads, kd, kd), seq_state)],
        scratch_shapes=[pltpu.VMEM((r_heads, kd, kd), F32),
                        pltpu.VMEM((PAGE_SLOTS, n_k) + ck.shape[1:], ck.dtype),
                        pltpu.VMEM((PAGE_SLOTS, n_k) + cv.shape[1:], cv.dtype),
                        pltpu.SemaphoreType.DMA((PAGE_SLOTS,)),
                        pltpu.SemaphoreType.DMA((PAGE_SLOTS,))],
    )
    return pl.pallas_call(
        kern,
        grid_spec=grid_spec,
        out_shape=[jax.ShapeDtypeStruct((b, t, r_heads * kd), BF16),
                   jax.ShapeDtypeStruct((b, r_heads, kd, kd), F32),
                   jax.ShapeDtypeStruct((n_seq * SAMPLE_PAD, aw), F32),
                   jax.ShapeDtypeStruct((n_seq * SAMPLE_PAD, r_heads * kd), F32),
                   jax.ShapeDtypeStruct(state.shape, F32)],
        compiler_params=pltpu.CompilerParams(dimension_semantics=("arbitrary",),
                                             vmem_limit_bytes=VMEM_LIMIT),
        name="hgrn_attn_decode",
    )(page_table, lam, r, lb, gn, jnp.asarray(tri, BF16), jnp.asarray(lev), q, kn, vn, bias, sg, rs, state, ck, cv)


def _hgrn_new_tokens(r_ref, lb_ref, gn_ref, s_ref, o_ref, so_ref, n_heads, kd, n_tok, n_seq):
    rw = n_heads * kd
    rowid = lax.broadcasted_iota(jnp.int32, (SAMPLE_PAD, kd), 0)
    live = rowid < n_tok
    zpad = jnp.zeros((SAMPLE_PAD, kd), F32)
    pad16 = lambda z: jnp.concatenate([z, zpad], axis=0).astype(BF16)
    ones16 = jnp.ones((2 * SAMPLE_PAD, kd), BF16)
    for g in range(n_seq):
        rows = slice(g * SAMPLE_PAD, (g + 1) * SAMPLE_PAD)
        for h in range(n_heads):
            cols = slice(h * kd, (h + 1) * kd)
            q = r_ref[rows, h * kd:(h + 1) * kd]
            rf = r_ref[rows, rw + h * kd:rw + (h + 1) * kd]
            v = r_ref[rows, 2 * rw + h * kd:2 * rw + (h + 1) * kd]
            rg = r_ref[rows, 3 * rw + h * kd:3 * rw + (h + 1) * kd]
            lb = lb_ref[:, cols]
            f = lb + (1.0 - lb) * jax.nn.sigmoid(rf)
            kk = jnp.where(live, 1.0 - f, 0.0)
            b = jnp.where(live, jnp.log(f) * LOG2E, 0.0)
            for sh in (1, 2, 4):
                b = b + jnp.where(rowid >= sh, pltpu.roll(b, sh, 0), 0.0)
            s0 = s_ref[g, h]
            o = jnp.sum(q * kk, axis=1, keepdims=True) * v
            for d in range(1, n_tok):
                x = q * pltpu.roll(kk, d, 0) * jnp.exp2(b - pltpu.roll(b, d, 0))
                w = jnp.sum(jnp.where(rowid >= d, x, 0.0), axis=1, keepdims=True)
                o = o + w * pltpu.roll(v, d, 0)
            o = o + _nn(pad16(q * jnp.exp2(b)), s0.astype(BF16))[0:SAMPLE_PAD]
            b_last = b[n_tok - 1:n_tok, :]
            decay = jnp.exp2(b_last)
            d_hi = decay.astype(BF16).astype(F32)
            d_split = jnp.where(rowid == 0, d_hi, jnp.where(rowid == 1, decay - d_hi, 0.0))
            decay_cols = _tn(pad16(d_split), ones16)
            so_ref[g, h] = decay_cols * s0 + _tn(pad16(kk * jnp.exp2(b_last - b)), pad16(v))
            ms = jnp.mean(o * o, axis=-1, keepdims=True)
            on = o * lax.rsqrt(ms + EPS) * gn_ref[...] * (rg * jax.nn.sigmoid(rg))
            o_ref[rows, cols] = on.astype(o_ref.dtype)


def _ffn_kernel(*refs, sample, aw, seq_rows):
    if sample:
        (x_ref, oa_ref, or_ref, wo_ref, g2_ref, wg_ref, wu_ref, cw_ref, cb_ref, wd_ref, prev_ref,
         y_ref, a_ref) = refs
    else:
        (x_ref, oa_ref, or_ref, wo_ref, g2_ref, wg_ref, wu_ref, cw_ref, cb_ref, wd_ref,
         y_ref, a_ref, carry_sc) = refs

        @pl.when(pl.program_id(1) == 0)
        def _():
            carry_sc[...] = jnp.zeros(carry_sc.shape, F32)

    x = x_ref[...].astype(F32)
    x1 = x + _nn(oa_ref[...].astype(BF16), wo_ref[0:aw, :]) + _nn(or_ref[...].astype(BF16), wo_ref[aw:, :])
    ms = jnp.mean(x1 * x1, axis=-1, keepdims=True)
    h2 = (x1 * lax.rsqrt(ms + EPS) * g2_ref[...]).astype(BF16)
    tm = x.shape[0]
    a = _nn(h2, wg_ref[...])
    u = _nn(h2, wu_ref[...])
    rowid = lax.broadcasted_iota(jnp.int32, a.shape, 0)
    am1 = pltpu.roll(a, 1, 0)
    am2 = pltpu.roll(a, 2, 0)
    if sample:
        rig = rowid & (seq_rows - 1)
        prev = prev_ref[...]
        am1 = jnp.where(rig == 0, pltpu.roll(prev, tm - 1, 0), am1)
        am2 = jnp.where(rig < 2, prev, am2)
        a_ref[...] = a
    else:
        last = carry_sc[...]
        am1 = jnp.where(rowid == 0, last[SUBLANES - 1:SUBLANES, :], am1)
        am2 = jnp.where(rowid == 0, last[SUBLANES - 2:SUBLANES - 1, :],
                        jnp.where(rowid == 1, last[SUBLANES - 1:SUBLANES, :], am2))
        tail = a[tm - SUBLANES:tm, :]
        carry_sc[...] = tail
        a_ref[0] = tail
    cw = cw_ref[...]
    cv = cb_ref[...] + cw[0:1, :] * am2 + cw[1:2, :] * am1 + cw[2:3, :] * a
    gate = (cv * jax.nn.sigmoid(cv) * u).astype(BF16)
    y_ref[...] = x1 + _nn(gate, wd_ref[...])


def _ffn(x2, oa, orec, wo, g2, wg, wu, cw, cb, wd, prev, tm, n_batch, seq_rows):
    n, d = x2.shape
    aw = oa.shape[1]
    dff = wg.shape[1]
    sample = prev is not None
    assert not sample or (seq_rows >= CONV_W - 1 and seq_rows & (seq_rows - 1) == 0 and tm % seq_rows == 0)
    kern = functools.partial(_ffn_kernel, sample=sample, aw=aw, seq_rows=seq_rows)
    scratch = []
    if sample:
        grid = (n // tm,)
        row = lambda i: (i, 0)
        c2 = lambda i: (0, 0)
        extra_in = [pl.BlockSpec((tm, dff), row)]
        extra_args = [prev]
        a_shape = jax.ShapeDtypeStruct((n, dff), F32)
        a_spec = pl.BlockSpec((tm, dff), row)
        sem = ("arbitrary",)
    else:
        per_b = n // n_batch // tm
        grid = (n_batch, per_b)
        row = lambda b, i: (b * per_b + i, 0)
        c2 = lambda b, i: (0, 0)
        extra_in = []
        extra_args = []
        a_shape = jax.ShapeDtypeStruct((n_batch, SUBLANES, dff), F32)
        a_spec = pl.BlockSpec((1, SUBLANES, dff), lambda b, i: (b, 0, 0))
        scratch.append(pltpu.VMEM((SUBLANES, dff), F32))
        sem = ("arbitrary", "arbitrary")
    return pl.pallas_call(
        kern,
        grid=grid,
        in_specs=[pl.BlockSpec((tm, d), row),
                  pl.BlockSpec((tm, aw), row),
                  pl.BlockSpec((tm, aw), row),
                  _const_spec(wo.shape, c2),
                  pl.BlockSpec((1, d), c2),
                  _const_spec(wg.shape, c2),
                  _const_spec(wu.shape, c2),
                  pl.BlockSpec(cw.shape, c2),
                  pl.BlockSpec(cb.shape, c2),
                  _const_spec(wd.shape, c2)] + extra_in,
        out_specs=[pl.BlockSpec((tm, d), row), a_spec],
        out_shape=[jax.ShapeDtypeStruct((n, d), F32), a_shape],
        scratch_shapes=scratch,
        compiler_params=pltpu.CompilerParams(dimension_semantics=sem, vmem_limit_bytes=VMEM_LIMIT),
        name="ffn_sample" if sample else "ffn_prompt",
    )(x2, oa, orec, wo, g2, wg, wu, cw, cb, wd, *extra_args)


def kernel(x_prompt, x_sample, cache_k, cache_v, page_table, state_hgrn, state_conv, rel_bias, norm1_g, w_in,
           qk_norm_g, lambda_qk, subln_g, hgrn_lb, hgrn_onorm_g, w_out, norm2_g, w_gate, w_up, conv_w, conv_b,
           w_down):
    bp, tp, d = x_prompt.shape
    bs, ts, _ = x_sample.shape
    depth = w_in.shape[0]
    assert depth == 1 and CONV_W - 1 <= ts <= SAMPLE_PAD and conv_w.shape[1] == CONV_W
    n_heads, _, hd = cache_k.shape[3:]
    vd = cache_v.shape[4]
    aw = n_heads * vd
    r_heads, kd = state_hgrn.shape[2], state_hgrn.shape[3]
    rw = r_heads * kd
    dff = w_gate.shape[2]
    page_size = cache_k.shape[2]
    assert page_size >= MAX_DISTANCE and tp % TQ == 0
    l = 0
    lam_init = 0.8 - 0.6 * math.exp(-0.3 * l)
    out_scale = 1.0 - lam_init

    biasp, biass, lam, lb = _tables(rel_bias, lambda_qk[l], hgrn_lb, n_heads, ts, page_size, lam_init)

    w_in_bf = w_in[l].astype(BF16)
    wo_bf = w_out[l].astype(BF16)
    wg_bf = w_gate[l].astype(BF16)
    wu_bf = w_up[l].astype(BF16)
    wd_bf = w_down[l].astype(BF16)
    cw = conv_w[l]
    cb = conv_b[l].reshape(1, dff)
    g1 = norm1_g[l].reshape(1, d)
    g2 = norm2_g[l].reshape(1, d)
    qg = jnp.tile(qk_norm_g[l, 0].reshape(1, 2 * hd), (1, n_heads))
    kg = jnp.tile(qk_norm_g[l, 1].reshape(1, 2 * hd), (1, n_heads))
    seg = np.arange(aw) // hd
    bd = jnp.asarray((seg[:, None] == seg[None, :]).astype(np.float32) / hd, BF16)
    sg = subln_g[l].reshape(1, vd)
    gn = hgrn_onorm_g[l].reshape(1, kd)
    q_scale = hd ** -0.5 * LOG2E

    xp2 = x_prompt.reshape(bp * tp, d)
    qbt, kft, vf, kb, vbt, rp = _proj(xp2, g1, w_in_bf, qg, kg, bd, aw, rw, vd, q_scale, True, BF16, BF16, 512, bp)
    xs_pad = jnp.pad(x_sample, ((0, 0), (0, SAMPLE_PAD - ts), (0, 0))).reshape(bs * SAMPLE_PAD, d)
    qs, ksf, vsf, rs = _proj(xs_pad, g1, w_in_bf, qg, kg, bd, aw, rw, vd, q_scale, False, F32, F32, 512, bs)

    o_att = _attn(lam, qbt, kb.reshape(bp, tp, aw), vbt, biasp, sg, n_heads, hd, out_scale)
    ck = jnp.transpose(cache_k[l], (0, 2, 3, 4, 1)).reshape(cache_k.shape[1], aw, page_size)
    cv = cache_v[l].reshape(cache_v.shape[1], page_size * n_heads, vd)
    o_rec, s_p, o_att_s, o_rec_s, s_s = _mix(page_table, lam, rp.reshape(bp, tp, 4 * rw), lb, gn, qs, ksf, vsf, biass,
                                             sg, rs, state_hgrn[l], ck, cv, n_heads, hd, r_heads, kd, out_scale, ts)

    yp, a_tail = _ffn(xp2, o_att.reshape(bp * tp, aw), o_rec.reshape(bp * tp, rw), wo_bf, g2, wg_bf, wu_bf, cw, cb,
                      wd_bf, None, 512, bp, None)
    k_prompt = kft.reshape(bp, n_heads, 2, hd, tp).transpose(0, 4, 1, 2, 3)[None]
    v_prompt = vf.reshape(1, bp, tp, n_heads, vd)
    conv_prompt = a_tail[:, SUBLANES - (CONV_W - 1):][None]
    unpad = lambda z: z.reshape(bs, SAMPLE_PAD, -1)[:, :ts]
    flat = lambda z: unpad(z).reshape(bs * ts, -1)
    prev = jnp.pad(state_conv[l], ((0, 0), (0, ts - (CONV_W - 1)), (0, 0))).reshape(bs * ts, dff)
    ys, a_s = _ffn(x_sample.reshape(bs * ts, d), flat(o_att_s), flat(o_rec_s), wo_bf, g2, wg_bf, wu_bf, cw, cb,
                   wd_bf, prev, 256, 1, ts)
    y_sample = ys.reshape(bs, ts, d)
    k_sample = unpad(ksf).reshape(1, bs, ts, n_heads, 2, hd)
    v_sample = unpad(vsf).reshape(1, bs, ts, n_heads, vd)
    conv_sample = a_s.reshape(bs, ts, dff)[:, ts - (CONV_W - 1):ts][None]

    return (yp.reshape(bp, tp, d), y_sample, k_prompt, v_prompt, s_p[None], conv_prompt,
            k_sample, v_sample, s_s[None], conv_sample)
```
